```python
import jax, jax.numpy as jnp
from jax import lax
import numpy as np

D_MODEL = 1024
BATCH = 8
SEQ = 2048
DEPTH = 2
DEC_BATCH = 128
DEC_SEQ = 1
PAST_LEN = 16384
PAGE_SIZE = 128

N_HEADS = 8
HEAD_DIM_K = 128
HEAD_DIM_V = 128
QK_DIM = N_HEADS * HEAD_DIM_K
V_DIM = N_HEADS * HEAD_DIM_V
QKV_DIM = 2 * QK_DIM + V_DIM
SHORT_CONV = 4
CHUNK = 64
CONF_DIM = D_MODEL // 2
CONF_CONV = 31
D_FF = 2816
FFN_CONV = 3
IN_COLS = QKV_DIM + V_DIM + 2 * N_HEADS + 2 * CONF_DIM + 2 * D_MODEL
NORM_EPS = 1e-6

kernel_name = "hybrid_gated_deltanet_conformer_convffn_step"


def rmsnorm(x, w):
    x32 = x.astype(jnp.float32)
    y = x32 * lax.rsqrt(jnp.mean(x32 * x32, axis=-1, keepdims=True) + NORM_EPS)
    return (y * w.astype(jnp.float32)).astype(x.dtype)


def layernorm(x, w, b):
    x32 = x.astype(jnp.float32)
    mu = jnp.mean(x32, axis=-1, keepdims=True)
    xc = x32 - mu
    y = xc * lax.rsqrt(jnp.mean(xc * xc, axis=-1, keepdims=True) + NORM_EPS)
    return (y * w.astype(jnp.float32) + b.astype(jnp.float32)).astype(x.dtype)


def l2norm(x):
    x32 = x.astype(jnp.float32)
    return x32 * lax.rsqrt(jnp.sum(x32 * x32, axis=-1, keepdims=True) + NORM_EPS)


def split_cols(a, sizes):
    out, start = [], 0
    for s in sizes:
        out.append(a[..., start:start + s])
        start += s
    return out


def causal_dwconv(x, buf, w):
    n_ch = x.shape[-1]
    width = w.shape[0]
    xp = jnp.concatenate([buf.astype(x.dtype), x], axis=1)
    y = lax.conv_general_dilated(xp, w.astype(x.dtype)[:, None, :], window_strides=(1,), padding='VALID',
                                 dimension_numbers=('NWC', 'WIO', 'NWC'), feature_group_count=n_ch)
    return y, xp[:, xp.shape[1] - (width - 1):]


def to_chunks(a, n):
    b, h = a.shape[0], a.shape[2]
    a = a.reshape((b, n, CHUNK, h) + a.shape[3:])
    return jnp.moveaxis(a, 3, 1)


def gated_delta_rule(q, k, v, beta, g, s0):
    b, t, h = q.shape[0], q.shape[1], q.shape[2]
    n = -(-t // CHUNK)
    pad = n * CHUNK - t

    def prep(a):
        a = a.astype(jnp.float32)
        a = jnp.pad(a, [(0, 0), (0, pad)] + [(0, 0)] * (a.ndim - 2))
        return to_chunks(a, n)

    q = prep(q) * (HEAD_DIM_K ** -0.5)
    k, v, beta, g = prep(k), prep(v), prep(beta), prep(g)
    gc = jnp.cumsum(g, axis=-1)
    idx = jnp.arange(CHUNK)
    causal = idx[:, None] >= idx[None, :]
    strict = idx[:, None] > idx[None, :]
    decay = jnp.exp(jnp.where(causal, gc[..., :, None] - gc[..., None, :], -jnp.inf))
    kb = k * beta[..., None]
    vb = v * beta[..., None]
    low = jnp.where(strict, jnp.einsum('bhnik,bhnjk->bhnij', kb, k) * decay, 0.0)
    a_mat = low + jnp.eye(CHUNK, dtype=jnp.float32)
    rhs = jnp.concatenate([vb, kb * jnp.exp(gc)[..., None]], axis=-1)
    sol = lax.linalg.triangular_solve(a_mat, rhs, left_side=True, lower=True, unit_diagonal=True)
    u, w = sol[..., :HEAD_DIM_V], sol[..., HEAD_DIM_V:]
    qk = jnp.einsum('bhnik,bhnjk->bhnij', q, k) * decay
    qg = q * jnp.exp(gc)[..., None]
    kg = k * jnp.exp(gc[..., -1:] - gc)[..., None]
    glast = jnp.exp(gc[..., -1])
    xs = tuple(jnp.moveaxis(a, 2, 0) for a in (qg, kg, u, w, qk, glast))

    def step(s, xc):
        qg_c, kg_c, u_c, w_c, qk_c, gl_c = xc
        v_new = u_c - jnp.einsum('bhck,bhkv->bhcv', w_c, s)
        o = jnp.einsum('bhck,bhkv->bhcv', qg_c, s) + jnp.einsum('bhij,bhjv->bhiv', qk_c, v_new)
        s = s * gl_c[..., None, None] + jnp.einsum('bhck,bhcv->bhkv', kg_c, v_new)
        return s, o

    s_final, o = lax.scan(step, s0.astype(jnp.float32), xs)
    o = jnp.transpose(o, (1, 0, 3, 2, 4)).reshape(b, n * CHUNK, h, HEAD_DIM_V)[:, :t]
    return o, s_final


def token_mixer(h, s_delta, s_qkv, s_conf, w_in, conv_qkv_w, a_log, dt_bias, delta_norm_w, w_o_delta,
                conf_conv_w, conf_conv_b, conf_ln_w, conf_ln_b, w_o_conf, w_out):
    b, t = h.shape[0], h.shape[1]
    proj = h @ w_in
    qkv, z, beta_in, alpha_in, glu, gate_a, gate_b = split_cols(
        proj, [QKV_DIM, V_DIM, N_HEADS, N_HEADS, 2 * CONF_DIM, D_MODEL, D_MODEL])
    qkv_c, new_qkv = causal_dwconv(qkv, s_qkv, conv_qkv_w)
    qkv_c = jax.nn.silu(qkv_c)
    q, k, v = split_cols(qkv_c, [QK_DIM, QK_DIM, V_DIM])
    q = l2norm(q.reshape(b, t, N_HEADS, HEAD_DIM_K))
    k = l2norm(k.reshape(b, t, N_HEADS, HEAD_DIM_K))
    v = v.reshape(b, t, N_HEADS, HEAD_DIM_V)
    beta = jax.nn.sigmoid(beta_in.astype(jnp.float32))
    g = -jnp.exp(a_log.astype(jnp.float32)) * jax.nn.softplus(alpha_in.astype(jnp.float32) + dt_bias.astype(jnp.float32))
    o, new_delta = gated_delta_rule(q, k, v, beta, g, s_delta)
    o = o * lax.rsqrt(jnp.mean(o * o, axis=-1, keepdims=True) + NORM_EPS) * delta_norm_w.astype(jnp.float32)
    o = o * jax.nn.silu(z.astype(jnp.float32).reshape(b, t, N_HEADS, HEAD_DIM_V))
    y_a = o.reshape(b, t, V_DIM).astype(h.dtype) @ w_o_delta
    u = glu[..., :CONF_DIM] * jax.nn.sigmoid(glu[..., CONF_DIM:])
    c, new_conf = causal_dwconv(u, s_conf, conf_conv_w)
    c = jax.nn.silu(layernorm(c + conf_conv_b, conf_ln_w, conf_ln_b))
    y_b = c @ w_o_conf
    merged = jax.nn.sigmoid(gate_a) * y_a + jax.nn.sigmoid(gate_b) * y_b
    return merged @ w_out, new_delta.astype(s_delta.dtype), new_qkv, new_conf


def conv_ffn(h, s_ffn, w_up, ffn_conv_w, ffn_conv_b, w_down):
    up, new_ffn = causal_dwconv(h @ w_up, s_ffn, ffn_conv_w)
    up = up + ffn_conv_b
    return (jax.nn.silu(up[..., :D_FF]) * up[..., D_FF:]) @ w_down, new_ffn


def setup_inputs(seed: int = 0) -> dict:
    key = jax.random.key(seed)
    ks = jax.random.split(key, 32)
    f32 = jnp.float32

    def nrm(k, shape, scale):
        return jax.random.normal(k, shape, f32) * scale

    def gain(k, shape):
        return 1.0 + 0.01 * jax.random.normal(k, shape, f32)

    dt = jnp.exp(jax.random.uniform(ks[10], (DEPTH, N_HEADS), f32, np.log(1e-3), np.log(1e-1)))
    return {
        'x_prompt': nrm(ks[0], (BATCH, SEQ, D_MODEL), 1.0),
        'x_sample': nrm(ks[1], (DEC_BATCH, DEC_SEQ, D_MODEL), 1.0),
        'state_delta': nrm(ks[2], (DEPTH, DEC_BATCH, N_HEADS, HEAD_DIM_K, HEAD_DIM_V), 0.1),
        'state_qkv_conv': nrm(ks[3], (DEPTH, DEC_BATCH, SHORT_CONV - 1, QKV_DIM), 1.0),
        'state_conf_conv': nrm(ks[4], (DEPTH, DEC_BATCH, CONF_CONV - 1, CONF_DIM), 0.5),
        'state_ffn_conv': nrm(ks[5], (DEPTH, DEC_BATCH, FFN_CONV - 1, 2 * D_FF), 1.0),
        'norm_mix_w': gain(ks[6], (DEPTH, D_MODEL)),
        'w_in': nrm(ks[7], (DEPTH, D_MODEL, IN_COLS), D_MODEL ** -0.5),
        'conv_qkv_w': nrm(ks[8], (DEPTH, SHORT_CONV, QKV_DIM), SHORT_CONV ** -0.5),
        'a_log': jnp.log(jax.random.uniform(ks[9], (DEPTH, N_HEADS), f32, 1.0, 16.0)),
        'dt_bias': dt + jnp.log(-jnp.expm1(-dt)),
        'delta_norm_w': gain(ks[11], (DEPTH, HEAD_DIM_V)),
        'w_o_delta': nrm(ks[12], (DEPTH, V_DIM, D_MODEL), V_DIM ** -0.5),
        'conf_conv_w': nrm(ks[13], (DEPTH, CONF_CONV, CONF_DIM), CONF_CONV ** -0.5),
        'conf_conv_b': nrm(ks[14], (DEPTH, CONF_DIM), 0.01),
        'conf_ln_w': gain(ks[15], (DEPTH, CONF_DIM)),
        'conf_ln_b': nrm(ks[16], (DEPTH, CONF_DIM), 0.01),
        'w_o_conf': nrm(ks[17], (DEPTH, CONF_DIM, D_MODEL), CONF_DIM ** -0.5),
        'w_out': nrm(ks[18], (DEPTH, D_MODEL, D_MODEL), D_MODEL ** -0.5),
        'norm_ffn_w': gain(ks[19], (DEPTH, D_MODEL)),
        'w_up': nrm(ks[20], (DEPTH, D_MODEL, 2 * D_FF), D_MODEL ** -0.5),
        'ffn_conv_w': nrm(ks[21], (DEPTH, FFN_CONV, 2 * D_FF), FFN_CONV ** -0.5),
        'ffn_conv_b': nrm(ks[22], (DEPTH, 2 * D_FF), 0.01),
        'w_down': nrm(ks[23], (DEPTH, D_FF, D_MODEL), D_FF ** -0.5),
        'norm_final_w': gain(ks[24], (D_MODEL,)),
    }


def reference(x_prompt, x_sample, state_delta, state_qkv_conv, state_conf_conv, state_ffn_conv,
              norm_mix_w, w_in, conv_qkv_w, a_log, dt_bias, delta_norm_w, w_o_delta,
              conf_conv_w, conf_conv_b, conf_ln_w, conf_ln_b, w_o_conf, w_out,
              norm_ffn_w, w_up, ffn_conv_w, ffn_conv_b, w_down, norm_final_w):
    b_p = x_prompt.shape[0]
    sd = x_prompt.dtype
    init_p = (jnp.zeros((b_p, N_HEADS, HEAD_DIM_K, HEAD_DIM_V), jnp.float32),
              jnp.zeros((b_p, SHORT_CONV - 1, QKV_DIM), sd),
              jnp.zeros((b_p, CONF_CONV - 1, CONF_DIM), sd),
              jnp.zeros((b_p, FFN_CONV - 1, 2 * D_FF), sd))
    xp, xs = x_prompt, x_sample
    new_p = ([], [], [], [])
    new_s = ([], [], [], [])
    for l in range(DEPTH):
        groups = ((xp, init_p, new_p),
                  (xs, (state_delta[l], state_qkv_conv[l], state_conf_conv[l], state_ffn_conv[l]), new_s))
        outs = []
        for x, st, sink in groups:
            h = rmsnorm(x, norm_mix_w[l])
            mix, n_delta, n_qkv, n_conf = token_mixer(
                h, st[0], st[1], st[2], w_in[l], conv_qkv_w[l], a_log[l], dt_bias[l], delta_norm_w[l],
                w_o_delta[l], conf_conv_w[l], conf_conv_b[l], conf_ln_w[l], conf_ln_b[l], w_o_conf[l], w_out[l])
            x = x + mix
            ff, n_ffn = conv_ffn(rmsnorm(x, norm_ffn_w[l]), st[3], w_up[l], ffn_conv_w[l], ffn_conv_b[l], w_down[l])
            x = x + ff
            for lst, val in zip(sink, (n_delta, n_qkv, n_conf, n_ffn)):
                lst.append(val)
            outs.append(x)
        xp, xs = outs[0], outs[1]
    y_prompt = rmsnorm(xp, norm_final_w)
    y_sample = rmsnorm(xs, norm_final_w)
    return (y_prompt, y_sample,
            jnp.stack(new_p[0]), jnp.stack(new_p[1]), jnp.stack(new_p[2]), jnp.stack(new_p[3]),
            jnp.stack(new_s[0]), jnp.stack(new_s[1]), jnp.stack(new_s[2]), jnp.stack(new_s[3]))
```

```python
import functools

import jax
import jax.numpy as jnp
from jax import lax
from jax.experimental import pallas as pl
from jax.experimental.pallas import tpu as pltpu

F32 = jnp.float32
BF16 = jnp.bfloat16

D_MODEL = 1024
N_HEADS = 8
HEAD_DIM = 128
QKV_DIM = 3 * N_HEADS * HEAD_DIM
SHORT_CONV = 4
CONF_DIM = D_MODEL // 2
CONF_CONV = 31
D_FF = 2816
FFN_CONV = 3
NORM_EPS = 1e-6
Q_SCALE = HEAD_DIM ** -0.5

LANES = 128
CHUNK = 128
TILE_M = 256
QKV_HIST = 8
CONF_HIST = 32
FFN_HIST = 8
VMEM_LIMIT = 60000 * 1024


def _mm(a, b):
    return jnp.dot(a.astype(BF16), b.astype(BF16), preferred_element_type=F32)


def _mm_nt(a, b):
    return lax.dot_general(a.astype(BF16), b.astype(BF16), (((1,), (1,)), ((), ())),
                           preferred_element_type=F32)


def _mm_tn(a, b):
    return lax.dot_general(a.astype(BF16), b.astype(BF16), (((0,), (0,)), ((), ())),
                           preferred_element_type=F32)


def _split3(x):
    hi = x.astype(BF16)
    r1 = x - hi.astype(F32)
    mid = r1.astype(BF16)
    lo = (r1 - mid.astype(F32)).astype(BF16)
    return hi, mid, lo


def _silu(x):
    return x * jax.nn.sigmoid(x)


def _softplus(x):
    return jnp.maximum(x, 0.0) + jnp.log1p(jnp.exp(-jnp.abs(x)))


def _rmsnorm(x, w):
    return x * lax.rsqrt(jnp.mean(x * x, axis=-1, keepdims=True) + NORM_EPS) * w


def _layernorm(x, w, b):
    mu = jnp.mean(x, axis=-1, keepdims=True)
    xc = x - mu
    return xc * lax.rsqrt(jnp.mean(xc * xc, axis=-1, keepdims=True) + NORM_EPS) * w + b


def _chunk_tri(n, upper):
    r = lax.broadcasted_iota(jnp.int32, (n, n), 0)
    c = lax.broadcasted_iota(jnp.int32, (n, n), 1)
    same = (r // CHUNK) == (c // CHUNK)
    tri = (r <= c) if upper else (r >= c)
    return jnp.where(same & tri, 1.0, 0.0).astype(BF16)


def _qkv_post(y, c):
    y = _silu(y)
    if c < 2 * N_HEADS:
        y = y * lax.rsqrt(jnp.sum(y * y, axis=-1, keepdims=True) + NORM_EPS)
        if c < N_HEADS:
            y = y * Q_SCALE
    return y


def _beta_decay(ba, alog, dtb):
    lane = lax.broadcasted_iota(jnp.int32, ba.shape, 1)
    beta = jnp.where(lane < N_HEADS, jax.nn.sigmoid(ba), 0.0)
    g = -jnp.exp(alog) * _softplus(ba + dtb)
    g = jnp.where((lane >= N_HEADS) & (lane < 2 * N_HEADS), g, 0.0)
    return beta, g


def _mixer_in_prompt_kernel(x_ref, nw_ref, wqkv_ref, wz_ref, wba_ref, wat_ref, wglu_ref, wg_ref,
                            cw_ref, alog_ref, dtb_ref, alogc_ref, dtbc_ref,
                            ccw_ref, ccb_ref, lnw_ref, lnb_ref, woc_ref,
                            q_ref, k_ref, v_ref, z_ref, sga_ref, mb_ref, bgc_ref, gct_ref,
                            nqkv_ref, nconf_ref,
                            qkv_buf, u_buf, cc_buf, *, tm):
    t = pl.program_id(1)

    @pl.when(t == 0)
    def _():
        qkv_buf[0:QKV_HIST, :] = jnp.zeros((QKV_HIST, QKV_DIM), F32)
        u_buf[0:CONF_HIST, :] = jnp.zeros((CONF_HIST, CONF_DIM), F32)

    h = _rmsnorm(x_ref[...], nw_ref[...]).astype(BF16)

    qkv_buf[QKV_HIST:QKV_HIST + tm, :] = jnp.dot(h, wqkv_ref[...], preferred_element_type=F32)
    nqkv_ref[...] = qkv_buf[QKV_HIST + tm - (SHORT_CONV - 1):QKV_HIST + tm, :]
    base = QKV_HIST - (SHORT_CONV - 1)
    for c in range(QKV_DIM // LANES):
        cs = slice(c * LANES, (c + 1) * LANES)
        acc = qkv_buf[base:base + tm, cs] * cw_ref[0:1, cs]
        for j in range(1, SHORT_CONV):
            acc = acc + qkv_buf[base + j:base + j + tm, cs] * cw_ref[j:j + 1, cs]
        y = _qkv_post(acc, c)
        if c < N_HEADS:
            q_ref[:, cs] = y
        elif c < 2 * N_HEADS:
            k_ref[:, (c - N_HEADS) * LANES:(c - N_HEADS + 1) * LANES] = y
        else:
            v_ref[:, (c - 2 * N_HEADS) * LANES:(c - 2 * N_HEADS + 1) * LANES] = y
    qkv_buf[0:QKV_HIST, :] = qkv_buf[tm:tm + QKV_HIST, :]

    z_ref[...] = jnp.dot(h, wz_ref[...], preferred_element_type=F32)

    ba = jnp.dot(h, wba_ref[...], preferred_element_type=F32)
    beta, g = _beta_decay(ba, alog_ref[...], dtb_ref[...])
    tril = _chunk_tri(tm, upper=False)
    ghi, gmid, glo = _split3(g)
    gc = (jnp.dot(tril, ghi, preferred_element_type=F32) + jnp.dot(tril, gmid, preferred_element_type=F32)
          + jnp.dot(tril, glo, preferred_element_type=F32))
    bgc_ref[...] = beta + gc
    at = _mm_nt(wat_ref[...], h)
    gt = -jnp.exp(alogc_ref[...]) * _softplus(at + dtbc_ref[...])
    triu = _chunk_tri(tm, upper=True)
    thi, tmid, tlo = _split3(gt)
    gct_ref[...] = (jnp.dot(thi, triu, preferred_element_type=F32) + jnp.dot(tmid, triu, preferred_element_type=F32)
                    + jnp.dot(tlo, triu, preferred_element_type=F32))

    gl = jnp.dot(h, wglu_ref[...], preferred_element_type=F32)
    u_buf[CONF_HIST:CONF_HIST + tm, :] = gl[:, :CONF_DIM] * jax.nn.sigmoid(gl[:, CONF_DIM:])
    nconf_ref[...] = u_buf[CONF_HIST + tm - (CONF_CONV - 1):CONF_HIST + tm, :]
    cbase = CONF_HIST - (CONF_CONV - 1)
    for c in range(CONF_DIM // LANES):
        cs = slice(c * LANES, (c + 1) * LANES)
        acc = u_buf[cbase:cbase + tm, cs] * ccw_ref[0:1, cs]
        for j in range(1, CONF_CONV):
            acc = acc + u_buf[cbase + j:cbase + j + tm, cs] * ccw_ref[j:j + 1, cs]
        cc_buf[:, cs] = acc + ccb_ref[:, cs]
    u_buf[0:CONF_HIST, :] = u_buf[tm:tm + CONF_HIST, :]
    cact = _silu(_layernorm(cc_buf[...], lnw_ref[...], lnb_ref[...]))
    yb = _mm(cact, woc_ref[...])

    gates = jnp.dot(h, wg_ref[...], preferred_element_type=F32)
    sga_ref[...] = jax.nn.sigmoid(gates[:, :D_MODEL])
    mb_ref[...] = jax.nn.sigmoid(gates[:, D_MODEL:]) * yb


def _delta_chunk_kernel(q_ref, k_ref, v_ref, bgc_ref, gct_ref, o_ref, s_ref):
    t = pl.program_id(1)

    @pl.when(t == 0)
    def _():
        s_ref[...] = jnp.zeros(s_ref.shape, F32)

    row = lax.broadcasted_iota(jnp.int32, (CHUNK, CHUNK), 0)
    col = lax.broadcasted_iota(jnp.int32, (CHUNK, CHUNK), 1)
    causal = row >= col
    strict = row > col
    sub_diag = {}
    s_blk = 1
    while s_blk < CHUNK:
        sub_diag[s_blk] = ((row // (2 * s_blk)) == (col // (2 * s_blk))) & ((row // s_blk) % 2 == 1) & ((col // s_blk) % 2 == 0)
        s_blk *= 2
    for h in range(N_HEADS):
        hs = slice(h * HEAD_DIM, (h + 1) * HEAD_DIM)
        q = q_ref[:, hs]
        k = k_ref[:, hs]
        v = v_ref[:, hs]
        beta = bgc_ref[:, h:h + 1]
        gc = bgc_ref[:, N_HEADS + h:N_HEADS + h + 1]
        gcr = gct_ref[h:h + 1, :]
        glast = gcr[:, CHUNK - 1:CHUNK]
        decay = jnp.where(causal, jnp.exp(jnp.where(causal, gc - gcr, 0.0)), 0.0)
        eg = jnp.exp(gc)
        kb = k * beta
        kq = _mm_nt(jnp.concatenate([kb, q], axis=0), k)
        low = jnp.where(strict, kq[:CHUNK] * decay, 0.0)
        qk = kq[CHUNK:] * decay
        tinv = jnp.where(row == col, 1.0, 0.0) - jnp.where(sub_diag[1], low, 0.0)
        s_blk = 2
        while s_blk < CHUNK:
            lc = jnp.where(sub_diag[s_blk], low, 0.0)
            tinv = tinv - _mm(tinv, _mm(lc, tinv))
            s_blk *= 2
        y = _mm(tinv, jnp.concatenate([v * beta, kb * eg], axis=1))
        u = y[:, :HEAD_DIM]
        w = y[:, HEAD_DIM:]
        s = s_ref[h]
        ws_qs = _mm(jnp.concatenate([w, q * eg], axis=0), s)
        v_new = u - ws_qs[:CHUNK]
        o_ref[:, hs] = ws_qs[CHUNK:] + _mm(qk, v_new)
        kg = k * jnp.exp(glast - gc)
        s_ref[h] = s * jnp.exp(glast) + _mm_tn(kg, v_new)


def _delta_out_norm(o_ref, z_ref, dnw):
    parts = []
    for h in range(N_HEADS):
        hs = slice(h * HEAD_DIM, (h + 1) * HEAD_DIM)
        o = o_ref[:, hs]
        on = o * lax.rsqrt(jnp.mean(o * o, axis=-1, keepdims=True) + NORM_EPS) * dnw
        parts.append((on * _silu(z_ref[:, hs])).astype(BF16))
    return jnp.concatenate(parts, axis=1)


def _mixer_out_ffn_prompt_kernel(o_ref, z_ref, sga_ref, mb_ref, x_ref, dnw_ref, wod_ref, wout_ref,
                                 nfw_ref, wup_ref, fcw_ref, fcb_ref, wdown_ref, finw_ref,
                                 out_ref, nffn_ref, up_buf, act_buf, *, tm, final):
    t = pl.program_id(1)

    @pl.when(t == 0)
    def _():
        up_buf[0:FFN_HIST, :] = jnp.zeros((FFN_HIST, 2 * D_FF), F32)

    on = _delta_out_norm(o_ref, z_ref, dnw_ref[...])
    ya = jnp.dot(on, wod_ref[...], preferred_element_type=F32)
    merged = sga_ref[...] * ya + mb_ref[...]
    x1 = x_ref[...] + _mm(merged, wout_ref[...])

    h2 = _rmsnorm(x1, nfw_ref[...]).astype(BF16)
    up_buf[FFN_HIST:FFN_HIST + tm, :] = jnp.dot(h2, wup_ref[...], preferred_element_type=F32)
    nffn_ref[...] = up_buf[FFN_HIST + tm - (FFN_CONV - 1):FFN_HIST + tm, :]
    base = FFN_HIST - (FFN_CONV - 1)
    for c in range(D_FF // LANES):
        acts = []
        for off in (0, D_FF):
            cs = slice(off + c * LANES, off + (c + 1) * LANES)
            acc = up_buf[base:base + tm, cs] * fcw_ref[0:1, cs]
            for j in range(1, FFN_CONV):
                acc = acc + up_buf[base + j:base + j + tm, cs] * fcw_ref[j:j + 1, cs]
            acts.append(acc + fcb_ref[:, cs])
        act_buf[:, c * LANES:(c + 1) * LANES] = (_silu(acts[0]) * acts[1]).astype(BF16)
    up_buf[0:FFN_HIST, :] = up_buf[tm:tm + FFN_HIST, :]
    x2 = x1 + jnp.dot(act_buf[...], wdown_ref[...], preferred_element_type=F32)
    out_ref[...] = _rmsnorm(x2, finw_ref[...]) if final else x2


def _mixer_in_sample_kernel(x_ref, sqkv_ref, sconf_ref, nw_ref, wqkv_ref, wz_ref, wba_ref, wglu_ref, wg_ref,
                            cw_ref, alog_ref, dtb_ref, ccw_ref, ccb_ref, lnw_ref, lnb_ref, woc_ref,
                            q_ref, k_ref, v_ref, z_ref, sga_ref, mb_ref, ba_ref, nqkv_ref, nconf_ref):
    h = _rmsnorm(x_ref[...], nw_ref[...]).astype(BF16)
    hist = SHORT_CONV - 1
    for c in range(QKV_DIM // LANES):
        cs = slice(c * LANES, (c + 1) * LANES)
        new = jnp.dot(h, wqkv_ref[:, cs], preferred_element_type=F32)
        acc = new * cw_ref[hist:hist + 1, cs]
        for j in range(hist):
            row = sqkv_ref[:, j * QKV_DIM + c * LANES:j * QKV_DIM + (c + 1) * LANES]
            acc = acc + row * cw_ref[j:j + 1, cs]
            if j > 0:
                nqkv_ref[:, (j - 1) * QKV_DIM + c * LANES:(j - 1) * QKV_DIM + (c + 1) * LANES] = row
        nqkv_ref[:, (hist - 1) * QKV_DIM + c * LANES:(hist - 1) * QKV_DIM + (c + 1) * LANES] = new
        y = _qkv_post(acc, c)
        if c < N_HEADS:
            q_ref[:, cs] = y
        elif c < 2 * N_HEADS:
            k_ref[:, (c - N_HEADS) * LANES:(c - N_HEADS + 1) * LANES] = y
        else:
            v_ref[:, (c - 2 * N_HEADS) * LANES:(c - 2 * N_HEADS + 1) * LANES] = y

    z_ref[...] = jnp.dot(h, wz_ref[...], preferred_element_type=F32)
    ba = jnp.dot(h, wba_ref[...], preferred_element_type=F32)
    beta, g = _beta_decay(ba, alog_ref[...], dtb_ref[...])
    lane = lax.broadcasted_iota(jnp.int32, ba.shape, 1)
    ba_ref[...] = jnp.where(lane < N_HEADS, beta, jnp.exp(g))

    gl = jnp.dot(h, wglu_ref[...], preferred_element_type=F32)
    u = gl[:, :CONF_DIM] * jax.nn.sigmoid(gl[:, CONF_DIM:])
    chist = CONF_CONV - 1
    acc = u * ccw_ref[chist:chist + 1, :]
    for j in range(chist):
        row = sconf_ref[:, j * CONF_DIM:(j + 1) * CONF_DIM]
        acc = acc + row * ccw_ref[j:j + 1, :]
        if j > 0:
            nconf_ref[:, (j - 1) * CONF_DIM:j * CONF_DIM] = row
    nconf_ref[:, (chist - 1) * CONF_DIM:chist * CONF_DIM] = u
    cact = _silu(_layernorm(acc + ccb_ref[...], lnw_ref[...], lnb_ref[...]))
    yb = _mm(cact, woc_ref[...])
    gates = jnp.dot(h, wg_ref[...], preferred_element_type=F32)
    sga_ref[...] = jax.nn.sigmoid(gates[:, :D_MODEL])
    mb_ref[...] = jax.nn.sigmoid(gates[:, D_MODEL:]) * yb


def _delta_step_kernel(beta_ref, a_ref, qt_ref, kt_ref, v_ref, s_ref, o_ref, snew_ref, *, bt):
    b0 = pl.program_id(0) * bt
    for i in range(bt):
        kt = kt_ref[i]
        qt = qt_ref[i]
        for h in range(N_HEADS):
            beta = beta_ref[(b0 + i) * N_HEADS + h]
            a = a_ref[(b0 + i) * N_HEADS + h]
            s = s_ref[i, h]
            kcol = kt[:, h:h + 1]
            r = jnp.sum(s * kcol, axis=0, keepdims=True)
            d = beta * (v_ref[i, h:h + 1, :] - a * r)
            sn = a * s + kcol * d
            snew_ref[i, h] = sn
            o_ref[i, h:h + 1, :] = jnp.sum(sn * qt[:, h:h + 1], axis=0, keepdims=True)


def _mixer_out_ffn_sample_kernel(o_ref, z_ref, sga_ref, mb_ref, x_ref, sffn_ref, dnw_ref, wod_ref, wout_ref,
                                 nfw_ref, wup_ref, fcw_ref, fcb_ref, wdown_ref, finw_ref,
                                 out_ref, nffn_ref, act_buf, *, final):
    on = _delta_out_norm(o_ref, z_ref, dnw_ref[...])
    ya = jnp.dot(on, wod_ref[...], preferred_element_type=F32)
    merged = sga_ref[...] * ya + mb_ref[...]
    x1 = x_ref[...] + _mm(merged, wout_ref[...])
    h2 = _rmsnorm(x1, nfw_ref[...]).astype(BF16)
    hist = FFN_CONV - 1
    for c in range(D_FF // LANES):
        acts = []
        for off in (0, D_FF):
            cs = slice(off + c * LANES, off + (c + 1) * LANES)
            new = jnp.dot(h2, wup_ref[:, cs], preferred_element_type=F32)
            acc = new * fcw_ref[hist:hist + 1, cs]
            for j in range(hist):
                row = sffn_ref[:, j * 2 * D_FF + off + c * LANES:j * 2 * D_FF + off + (c + 1) * LANES]
                acc = acc + row * fcw_ref[j:j + 1, cs]
                if j > 0:
                    nffn_ref[:, (j - 1) * 2 * D_FF + off + c * LANES:(j - 1) * 2 * D_FF + off + (c + 1) * LANES] = row
            nffn_ref[:, (hist - 1) * 2 * D_FF + off + c * LANES:(hist - 1) * 2 * D_FF + off + (c + 1) * LANES] = new
            acts.append(acc + fcb_ref[:, cs])
        act_buf[:, c * LANES:(c + 1) * LANES] = (_silu(acts[0]) * acts[1]).astype(BF16)
    x2 = x1 + jnp.dot(act_buf[...], wdown_ref[...], preferred_element_type=F32)
    out_ref[...] = _rmsnorm(x2, finw_ref[...]) if final else x2


def _params(n_grid):
    return pltpu.CompilerParams(dimension_semantics=("arbitrary",) * n_grid, vmem_limit_bytes=VMEM_LIMIT)


def _resident(shape):
    nd = len(shape)
    return pl.BlockSpec(shape, lambda *_: (0,) * nd, pipeline_mode=pl.Buffered(1))


def _mixer_in_prompt(x, lw):
    nb, seq, _ = x.shape
    tm = TILE_M
    tok = lambda n: pl.BlockSpec((None, tm, n), lambda b, t: (b, t, 0))
    consts = [lw[k] for k in ("norm_mix_w", "w_qkv", "w_z", "w_ba", "w_at", "w_glu", "w_g", "conv_qkv_w",
                              "alog_row", "dtb_row", "alog_col", "dtb_col",
                              "conf_conv_w", "conf_conv_b", "conf_ln_w", "conf_ln_b", "w_o_conf")]
    act = jax.ShapeDtypeStruct((nb, seq, D_MODEL), F32)
    return pl.pallas_call(
        functools.partial(_mixer_in_prompt_kernel, tm=tm),
        grid=(nb, seq // tm),
        in_specs=[tok(D_MODEL)] + [_resident(c.shape) for c in consts],
        out_specs=[tok(D_MODEL)] * 6 + [
            tok(LANES),
            pl.BlockSpec((None, N_HEADS, tm), lambda b, t: (b, 0, t)),
            pl.BlockSpec((None, SHORT_CONV - 1, QKV_DIM), lambda b, t: (b, 0, 0)),
            pl.BlockSpec((None, CONF_CONV - 1, CONF_DIM), lambda b, t: (b, 0, 0)),
        ],
        out_shape=[act] * 6 + [
            jax.ShapeDtypeStruct((nb, seq, LANES), F32),
            jax.ShapeDtypeStruct((nb, N_HEADS, seq), F32),
            jax.ShapeDtypeStruct((nb, SHORT_CONV - 1, QKV_DIM), F32),
            jax.ShapeDtypeStruct((nb, CONF_CONV - 1, CONF_DIM), F32),
        ],
        scratch_shapes=[pltpu.VMEM((QKV_HIST + tm, QKV_DIM), F32),
                        pltpu.VMEM((CONF_HIST + tm, CONF_DIM), F32),
                        pltpu.VMEM((tm, CONF_DIM), F32)],
        compiler_params=_params(2),
        name="mixer_in_prompt",
    )(x, *consts)


def _delta_prompt(q, k, v, bgc, gct):
    nb, seq, _ = q.shape
    tok = lambda n: pl.BlockSpec((None, CHUNK, n), lambda b, t: (b, t, 0))
    return pl.pallas_call(
        _delta_chunk_kernel,
        grid=(nb, seq // CHUNK),
        in_specs=[tok(D_MODEL)] * 3 + [tok(LANES), pl.BlockSpec((None, N_HEADS, CHUNK), lambda b, t: (b, 0, t))],
        out_specs=[tok(D_MODEL),
                   pl.BlockSpec((None, N_HEADS, HEAD_DIM, HEAD_DIM), lambda b, t: (b, 0, 0, 0))],
        out_shape=[jax.ShapeDtypeStruct((nb, seq, D_MODEL), F32),
                   jax.ShapeDtypeStruct((nb, N_HEADS, HEAD_DIM, HEAD_DIM), F32)],
        compiler_params=_params(2),
        name="delta_prompt",
    )(q, k, v, bgc, gct)


_OUT_CONSTS = ("delta_norm_w", "w_o_delta", "w_out", "norm_ffn_w", "w_up", "ffn_conv_w", "ffn_conv_b", "w_down")


def _mixer_out_ffn_prompt(o, z, sga, mb, x, lw, fin_w, final):
    nb, seq, _ = x.shape
    tm = TILE_M
    tok = pl.BlockSpec((None, tm, D_MODEL), lambda b, t: (b, t, 0))
    consts = [lw[k] for k in _OUT_CONSTS] + [fin_w]
    return pl.pallas_call(
        functools.partial(_mixer_out_ffn_prompt_kernel, tm=tm, final=final),
        grid=(nb, seq // tm),
        in_specs=[tok] * 5 + [_resident(c.shape) for c in consts],
        out_specs=[tok, pl.BlockSpec((None, FFN_CONV - 1, 2 * D_FF), lambda b, t: (b, 0, 0))],
        out_shape=[jax.ShapeDtypeStruct((nb, seq, D_MODEL), F32),
                   jax.ShapeDtypeStruct((nb, FFN_CONV - 1, 2 * D_FF), F32)],
        scratch_shapes=[pltpu.VMEM((FFN_HIST + tm, 2 * D_FF), F32), pltpu.VMEM((tm, D_FF), BF16)],
        compiler_params=_params(2),
        name="mixer_out_ffn_prompt",
    )(o, z, sga, mb, x, *consts)


def _mixer_in_sample(x, sqkv, sconf, lw):
    nb = x.shape[0]
    consts = [lw[k] for k in ("norm_mix_w", "w_qkv", "w_z", "w_ba", "w_glu", "w_g", "conv_qkv_w",
                              "alog_row", "dtb_row",
                              "conf_conv_w", "conf_conv_b", "conf_ln_w", "conf_ln_b", "w_o_conf")]
    act = jax.ShapeDtypeStruct((nb, D_MODEL), F32)
    return pl.pallas_call(
        _mixer_in_sample_kernel,
        out_shape=[act] * 6 + [jax.ShapeDtypeStruct((nb, LANES), F32),
                               jax.ShapeDtypeStruct(sqkv.shape, F32),
                               jax.ShapeDtypeStruct(sconf.shape, F32)],
        compiler_params=pltpu.CompilerParams(vmem_limit_bytes=VMEM_LIMIT),
        name="mixer_in_sample",
    )(x, sqkv, sconf, *consts)


def _delta_sample(beta, a, qt, kt, v, s):
    nb = s.shape[0]
    bt = 8
    smem = pl.BlockSpec(memory_space=pltpu.SMEM)
    return pl.pallas_call(
        functools.partial(_delta_step_kernel, bt=bt),
        grid=(nb // bt,),
        in_specs=[smem, smem,
                  pl.BlockSpec((bt, HEAD_DIM, N_HEADS), lambda i: (i, 0, 0)),
                  pl.BlockSpec((bt, HEAD_DIM, N_HEADS), lambda i: (i, 0, 0)),
                  pl.BlockSpec((bt, N_HEADS, HEAD_DIM), lambda i: (i, 0, 0)),
                  pl.BlockSpec((bt, N_HEADS, HEAD_DIM, HEAD_DIM), lambda i: (i, 0, 0, 0))],
        out_specs=[pl.BlockSpec((bt, N_HEADS, HEAD_DIM), lambda i: (i, 0, 0)),
                   pl.BlockSpec((bt, N_HEADS, HEAD_DIM, HEAD_DIM), lambda i: (i, 0, 0, 0))],
        out_shape=[jax.ShapeDtypeStruct((nb, N_HEADS, HEAD_DIM), F32),
                   jax.ShapeDtypeStruct(s.shape, F32)],
        compiler_params=_params(1),
        name="delta_sample",
    )(beta, a, qt, kt, v, s)


def _mixer_out_ffn_sample(o, z, sga, mb, x, sffn, lw, fin_w, final):
    nb = x.shape[0]
    consts = [lw[k] for k in _OUT_CONSTS] + [fin_w]
    return pl.pallas_call(
        functools.partial(_mixer_out_ffn_sample_kernel, final=final),
        out_shape=[jax.ShapeDtypeStruct((nb, D_MODEL), F32), jax.ShapeDtypeStruct(sffn.shape, F32)],
        scratch_shapes=[pltpu.VMEM((nb, D_FF), BF16)],
        compiler_params=pltpu.CompilerParams(vmem_limit_bytes=VMEM_LIMIT),
        name="mixer_out_ffn_sample",
    )(o, z, sga, mb, x, sffn, *consts)


def _layer_weights(l, norm_mix_w, w_in, conv_qkv_w, a_log, dt_bias, delta_norm_w, w_o_delta,
                   conf_conv_w, conf_conv_b, conf_ln_w, conf_ln_b, w_o_conf, w_out,
                   norm_ffn_w, w_up, ffn_conv_w, ffn_conv_b, w_down):
    w = w_in[l]
    o_z = QKV_DIM
    o_b = o_z + N_HEADS * HEAD_DIM
    o_a = o_b + N_HEADS
    o_glu = o_a + N_HEADS
    o_g = o_glu + 2 * CONF_DIM
    row = lambda p: p.reshape(1, -1)
    pad = LANES - 2 * N_HEADS
    lane_pad = lambda p: jnp.pad(p.reshape(1, N_HEADS), ((0, 0), (N_HEADS, pad)))
    return {
        "norm_mix_w": row(norm_mix_w[l]),
        "w_qkv": w[:, :o_z].astype(BF16),
        "w_z": w[:, o_z:o_b].astype(BF16),
        "w_ba": jnp.pad(w[:, o_b:o_glu], ((0, 0), (0, pad))).astype(BF16),
        "w_at": w[:, o_a:o_glu].T.astype(BF16),
        "w_glu": w[:, o_glu:o_g].astype(BF16),
        "w_g": w[:, o_g:].astype(BF16),
        "conv_qkv_w": conv_qkv_w[l],
        "alog_row": lane_pad(a_log[l]), "dtb_row": lane_pad(dt_bias[l]),
        "alog_col": a_log[l].reshape(N_HEADS, 1), "dtb_col": dt_bias[l].reshape(N_HEADS, 1),
        "conf_conv_w": conf_conv_w[l], "conf_conv_b": row(conf_conv_b[l]),
        "conf_ln_w": row(conf_ln_w[l]), "conf_ln_b": row(conf_ln_b[l]),
        "w_o_conf": w_o_conf[l].astype(BF16),
        "delta_norm_w": row(delta_norm_w[l]),
        "w_o_delta": w_o_delta[l].astype(BF16),
        "w_out": w_out[l].astype(BF16),
        "norm_ffn_w": row(norm_ffn_w[l]),
        "w_up": w_up[l].astype(BF16),
        "ffn_conv_w": ffn_conv_w[l], "ffn_conv_b": row(ffn_conv_b[l]),
        "w_down": w_down[l].astype(BF16),
    }


def kernel(x_prompt, x_sample, state_delta, state_qkv_conv, state_conf_conv, state_ffn_conv, norm_mix_w, w_in, conv_qkv_w, a_log, dt_bias, delta_norm_w, w_o_delta, conf_conv_w, conf_conv_b, conf_ln_w, conf_ln_b, w_o_conf, w_out, norm_ffn_w, w_up, ffn_conv_w, ffn_conv_b, w_down, norm_final_w):
    depth = w_in.shape[0]
    nb_s = x_sample.shape[0]
    fin_w = norm_final_w.reshape(1, -1)
    xp = x_prompt
    xs = x_sample.reshape(nb_s, D_MODEL)
    new_p = ([], [], [], [])
    new_s = ([], [], [], [])
    for l in range(depth):
        final = l == depth - 1
        lw = _layer_weights(l, norm_mix_w, w_in, conv_qkv_w, a_log, dt_bias, delta_norm_w, w_o_delta,
                            conf_conv_w, conf_conv_b, conf_ln_w, conf_ln_b, w_o_conf, w_out,
                            norm_ffn_w, w_up, ffn_conv_w, ffn_conv_b, w_down)
        q, k, v, z, sga, mb, bgc, gct, n_qkv, n_conf = _mixer_in_prompt(xp, lw)
        o, n_delta = _delta_prompt(q, k, v, bgc, gct)
        xp, n_ffn = _mixer_out_ffn_prompt(o, z, sga, mb, xp, lw, fin_w, final)
        for lst, val in zip(new_p, (n_delta, n_qkv, n_conf, n_ffn)):
            lst.append(val)
        q, k, v, z, sga, mb, ba, n_qkv, n_conf = _mixer_in_sample(
            xs, state_qkv_conv[l].reshape(nb_s, -1), state_conf_conv[l].reshape(nb_s, -1), lw)
        heads = lambda a: a.reshape(nb_s, N_HEADS, HEAD_DIM)
        o, n_delta = _delta_sample(ba[:, :N_HEADS].reshape(-1), ba[:, N_HEADS:2 * N_HEADS].reshape(-1),
                                   heads(q).transpose(0, 2, 1), heads(k).transpose(0, 2, 1), heads(v),
                                   state_delta[l])
        xs, n_ffn = _mixer_out_ffn_sample(o.reshape(nb_s, D_MODEL), z, sga, mb, xs,
                                          state_ffn_conv[l].reshape(nb_s, -1), lw, fin_w, final)
        for lst, val in zip(new_s, (n_delta, n_qkv.reshape(state_qkv_conv.shape[1:]),
                                    n_conf.reshape(state_conf_conv.shape[1:]),
                                    n_ffn.reshape(state_ffn_conv.shape[1:]))):
            lst.append(val)
    return (xp, xs.reshape(x_sample.shape),
            jnp.stack(new_p[0]), jnp.stack(new_p[1]), jnp.stack(new_p[2]), jnp.stack(new_p[3]),
            jnp.stack(new_s[0]), jnp.stack(new_s[1]), jnp.stack(new_s[2]), jnp.stack(new_s[3]))
```

```python
import functools

import jax
import jax.numpy as jnp
from jax import lax
from jax.experimental import pallas as pl
from jax.experimental.pallas import tpu as pltpu

F32 = jnp.float32
BF16 = jnp.bfloat16

D_MODEL = 1024
N_HEADS = 8
HEAD_DIM = 128
QKV_DIM = 3 * N_HEADS * HEAD_DIM
SHORT_CONV = 4
CONF_DIM = D_MODEL // 2
CONF_CONV = 31
D_FF = 2816
FFN_CONV = 3
NORM_EPS = 1e-6
Q_SCALE = HEAD_DIM ** -0.5

LANES = 128
CHUNK = 128
TILE_M = 256
QKV_HIST = 8
CONF_HIST = 32
FFN_HIST = 8
VMEM_LIMIT = 60000 * 1024


def _mm(a, b):
    return jnp.dot(a.astype(BF16), b.astype(BF16), preferred_element_type=F32)


def _mm_nt(a, b):
    return lax.dot_general(a.astype(BF16), b.astype(BF16), (((1,), (1,)), ((), ())),
                           preferred_element_type=F32)


def _mm_tn(a, b):
    return lax.dot_general(a.astype(BF16), b.astype(BF16), (((0,), (0,)), ((), ())),
                           preferred_element_type=F32)


def _split3(x):
    hi = x.astype(BF16)
    r1 = x - hi.astype(F32)
    mid = r1.astype(BF16)
    lo = (r1 - mid.astype(F32)).astype(BF16)
    return hi, mid, lo


def _silu(x):
    return x * jax.nn.sigmoid(x)


def _softplus(x):
    return jnp.maximum(x, 0.0) + jnp.log1p(jnp.exp(-jnp.abs(x)))


def _rmsnorm(x, w):
    return x * lax.rsqrt(jnp.mean(x * x, axis=-1, keepdims=True) + NORM_EPS) * w


def _layernorm(x, w, b):
    mu = jnp.mean(x, axis=-1, keepdims=True)
    xc = x - mu
    return xc * lax.rsqrt(jnp.mean(xc * xc, axis=-1, keepdims=True) + NORM_EPS) * w + b


def _chunk_tri(n, upper):
    r = lax.broadcasted_iota(jnp.int32, (n, n), 0)
    c = lax.broadcasted_iota(jnp.int32, (n, n), 1)
    same = (r // CHUNK) == (c // CHUNK)
    tri = (r <= c) if upper else (r >= c)
    return jnp.where(same & tri, 1.0, 0.0).astype(BF16)


def _qkv_post(y, c):
    y = _silu(y)
    if c < 2 * N_HEADS:
        y = y * lax.rsqrt(jnp.sum(y * y, axis=-1, keepdims=True) + NORM_EPS)
        if c < N_HEADS:
            y = y * Q_SCALE
    return y


def _beta_decay(ba, alog, dtb):
    lane = lax.broadcasted_iota(jnp.int32, ba.shape, 1)
    beta = jnp.where(lane < N_HEADS, jax.nn.sigmoid(ba), 0.0)
    g = -jnp.exp(alog) * _softplus(ba + dtb)
    g = jnp.where((lane >= N_HEADS) & (lane < 2 * N_HEADS), g, 0.0)
    return beta, g


def _mixer_in_prompt_kernel(x_ref, nw_ref, wqkv_ref, wz_ref, wba_ref, wat_ref, wglu_ref, wga_ref, wgb_ref,
                            cw_ref, alog_ref, dtb_ref, alogc_ref, dtbc_ref,
                            ccw_ref, ccb_ref, lnw_ref, lnb_ref, woc_ref,
                            q_ref, k_ref, v_ref, z_ref, sga_ref, mb_ref, bgc_ref, gct_ref,
                            nqkv_ref, nconf_ref,
                            qkv_buf, u_buf, cc_buf, *, tm):
    t = pl.program_id(1)

    @pl.when(t == 0)
    def _():
        qkv_buf[0:QKV_HIST, :] = jnp.zeros((QKV_HIST, QKV_DIM), F32)
        u_buf[0:CONF_HIST, :] = jnp.zeros((CONF_HIST, CONF_DIM), F32)

    h = _rmsnorm(x_ref[...], nw_ref[...]).astype(BF16)

    qkv_buf[QKV_HIST:QKV_HIST + tm, :] = jnp.dot(h, wqkv_ref[...], preferred_element_type=F32)
    nqkv_ref[...] = qkv_buf[QKV_HIST + tm - (SHORT_CONV - 1):QKV_HIST + tm, :]
    base = QKV_HIST - (SHORT_CONV - 1)
    for c in range(QKV_DIM // LANES):
        cs = slice(c * LANES, (c + 1) * LANES)
        acc = qkv_buf[base:base + tm, cs] * cw_ref[0:1, cs]
        for j in range(1, SHORT_CONV):
            acc = acc + qkv_buf[base + j:base + j + tm, cs] * cw_ref[j:j + 1, cs]
        y = _qkv_post(acc, c)
        if c < N_HEADS:
            q_ref[:, cs] = y
        elif c < 2 * N_HEADS:
            k_ref[:, (c - N_HEADS) * LANES:(c - N_HEADS + 1) * LANES] = y
        else:
            v_ref[:, (c - 2 * N_HEADS) * LANES:(c - 2 * N_HEADS + 1) * LANES] = y
    qkv_buf[0:QKV_HIST, :] = qkv_buf[tm:tm + QKV_HIST, :]

    z_ref[...] = jnp.dot(h, wz_ref[...], preferred_element_type=F32)

    ba = jnp.dot(h, wba_ref[...], preferred_element_type=F32)
    beta, g = _beta_decay(ba, alog_ref[...], dtb_ref[...])
    tril = _chunk_tri(tm, upper=False)
    ghi, gmid, glo = _split3(g)
    gc = (jnp.dot(tril, ghi, preferred_element_type=F32) + jnp.dot(tril, gmid, preferred_element_type=F32)
          + jnp.dot(tril, glo, preferred_element_type=F32))
    bgc_ref[...] = beta + gc
    at = _mm_nt(wat_ref[...], h)
    gt = -jnp.exp(alogc_ref[...]) * _softplus(at + dtbc_ref[...])
    triu = _chunk_tri(tm, upper=True)
    thi, tmid, tlo = _split3(gt)
    gct_ref[...] = (jnp.dot(thi, triu, preferred_element_type=F32) + jnp.dot(tmid, triu, preferred_element_type=F32)
                    + jnp.dot(tlo, triu, preferred_element_type=F32))

    gl = jnp.dot(h, wglu_ref[...], preferred_element_type=F32)
    u_buf[CONF_HIST:CONF_HIST + tm, :] = gl[:, :CONF_DIM] * jax.nn.sigmoid(gl[:, CONF_DIM:])
    nconf_ref[...] = u_buf[CONF_HIST + tm - (CONF_CONV - 1):CONF_HIST + tm, :]
    cbase = CONF_HIST - (CONF_CONV - 1)
    for c in range(CONF_DIM // LANES):
        cs = slice(c * LANES, (c + 1) * LANES)
        acc = u_buf[cbase:cbase + tm, cs] * ccw_ref[0:1, cs]
        for j in range(1, CONF_CONV):
            acc = acc + u_buf[cbase + j:cbase + j + tm, cs] * ccw_ref[j:j + 1, cs]
        cc_buf[:, cs] = acc + ccb_ref[:, cs]
    u_buf[0:CONF_HIST, :] = u_buf[tm:tm + CONF_HIST, :]
    cact = _silu(_layernorm(cc_buf[...], lnw_ref[...], lnb_ref[...]))
    yb = _mm(cact, woc_ref[...])

    sga_ref[...] = jax.nn.sigmoid(jnp.dot(h, wga_ref[...], preferred_element_type=F32))
    mb_ref[...] = jax.nn.sigmoid(jnp.dot(h, wgb_ref[...], preferred_element_type=F32)) * yb


def _delta_chunk_kernel(q_ref, k_ref, v_ref, bgc_ref, gct_ref, o_ref, s_ref):
    t = pl.program_id(1)

    @pl.when(t == 0)
    def _():
        s_ref[...] = jnp.zeros(s_ref.shape, F32)

    row = lax.broadcasted_iota(jnp.int32, (CHUNK, CHUNK), 0)
    col = lax.broadcasted_iota(jnp.int32, (CHUNK, CHUNK), 1)
    causal = row >= col
    strict = row > col
    sub_diag = {}
    s_blk = 1
    while s_blk < CHUNK:
        sub_diag[s_blk] = ((row // (2 * s_blk)) == (col // (2 * s_blk))) & ((row // s_blk) % 2 == 1) & ((col // s_blk) % 2 == 0)
        s_blk *= 2
    heads = range(N_HEADS)
    hsl = [slice(h * HEAD_DIM, (h + 1) * HEAD_DIM) for h in heads]
    beta = [bgc_ref[:, h:h + 1] for h in heads]
    gc = [bgc_ref[:, N_HEADS + h:N_HEADS + h + 1] for h in heads]
    glast = [gct_ref[h:h + 1, CHUNK - 1:CHUNK] for h in heads]
    eye = jnp.where(row == col, 1.0, 0.0)

    low, qk, rhs = [], [], []
    for h in heads:
        k = k_ref[:, hsl[h]]
        kb = k * beta[h]
        decay = jnp.where(causal, jnp.exp(jnp.where(causal, gc[h] - gct_ref[h:h + 1, :], 0.0)), 0.0)
        kq = _mm_nt(jnp.concatenate([kb, q_ref[:, hsl[h]]], axis=0), k)
        low.append(jnp.where(strict, kq[:CHUNK] * decay, 0.0))
        qk.append((kq[CHUNK:] * decay).astype(BF16))
        rhs.append(jnp.concatenate([v_ref[:, hsl[h]] * beta[h], kb * jnp.exp(gc[h])], axis=1).astype(BF16))

    tinv = [eye - jnp.where(sub_diag[1], low[h], 0.0) for h in heads]
    s_blk = 2
    while s_blk < CHUNK:
        t16 = [tinv[h].astype(BF16) for h in heads]
        m1 = [jnp.dot(jnp.where(sub_diag[s_blk], low[h], 0.0).astype(BF16), t16[h],
                      preferred_element_type=F32) for h in heads]
        tinv = [tinv[h] - jnp.dot(t16[h], m1[h].astype(BF16), preferred_element_type=F32) for h in heads]
        s_blk *= 2
    y = [jnp.dot(tinv[h].astype(BF16), rhs[h], preferred_element_type=F32) for h in heads]

    ws_qs = [_mm(jnp.concatenate([y[h][:, HEAD_DIM:], q_ref[:, hsl[h]] * jnp.exp(gc[h])], axis=0), s_ref[h])
             for h in heads]
    v_new = [(y[h][:, :HEAD_DIM] - ws_qs[h][:CHUNK]).astype(BF16) for h in heads]
    for h in heads:
        o_ref[:, hsl[h]] = ws_qs[h][CHUNK:] + jnp.dot(qk[h], v_new[h], preferred_element_type=F32)
    for h in heads:
        kg = k_ref[:, hsl[h]] * jnp.exp(glast[h] - gc[h])
        s_ref[h] = s_ref[h] * jnp.exp(glast[h]) + _mm_tn(kg, v_new[h])


def _delta_out_norm(o_ref, z_ref, dnw):
    parts = []
    for h in range(N_HEADS):
        hs = slice(h * HEAD_DIM, (h + 1) * HEAD_DIM)
        o = o_ref[:, hs]
        on = o * lax.rsqrt(jnp.mean(o * o, axis=-1, keepdims=True) + NORM_EPS) * dnw
        parts.append((on * _silu(z_ref[:, hs])).astype(BF16))
    return jnp.concatenate(parts, axis=1)


def _mixer_out_ffn_prompt_kernel(o_ref, z_ref, sga_ref, mb_ref, x_ref, dnw_ref, wod_ref, wout_ref,
                                 nfw_ref, wup_ref, fcw_ref, fcb_ref, wdown_ref, finw_ref,
                                 out_ref, nffn_ref, up_buf, act_buf, *, tm, final):
    t = pl.program_id(1)

    @pl.when(t == 0)
    def _():
        up_buf[0:FFN_HIST, :] = jnp.zeros((FFN_HIST, 2 * D_FF), F32)

    on = _delta_out_norm(o_ref, z_ref, dnw_ref[...])
    ya = jnp.dot(on, wod_ref[...], preferred_element_type=F32)
    merged = sga_ref[...] * ya + mb_ref[...]
    x1 = x_ref[...] + _mm(merged, wout_ref[...])

    h2 = _rmsnorm(x1, nfw_ref[...]).astype(BF16)
    up_buf[FFN_HIST:FFN_HIST + tm, :] = jnp.dot(h2, wup_ref[...], preferred_element_type=F32)
    nffn_ref[...] = up_buf[FFN_HIST + tm - (FFN_CONV - 1):FFN_HIST + tm, :]
    base = FFN_HIST - (FFN_CONV - 1)
    for c in range(D_FF // LANES):
        acts = []
        for off in (0, D_FF):
            cs = slice(off + c * LANES, off + (c + 1) * LANES)
            acc = up_buf[base:base + tm, cs] * fcw_ref[0:1, cs]
            for j in range(1, FFN_CONV):
                acc = acc + up_buf[base + j:base + j + tm, cs] * fcw_ref[j:j + 1, cs]
            acts.append(acc + fcb_ref[:, cs])
        act_buf[:, c * LANES:(c + 1) * LANES] = (_silu(acts[0]) * acts[1]).astype(BF16)
    up_buf[0:FFN_HIST, :] = up_buf[tm:tm + FFN_HIST, :]
    x2 = x1 + jnp.dot(act_buf[...], wdown_ref[...], preferred_element_type=F32)
    out_ref[...] = _rmsnorm(x2, finw_ref[...]) if final else x2


def _mixer_in_sample_kernel(x_ref, sqkv_ref, sconf_ref, nw_ref, wqkv_ref, wz_ref, wba_ref, wglu_ref, wga_ref, wgb_ref,
                            cw_ref, alog_ref, dtb_ref, ccw_ref, ccb_ref, lnw_ref, lnb_ref, woc_ref,
                            q_ref, k_ref, v_ref, z_ref, sga_ref, mb_ref, ba_ref, nqkv_ref, nconf_ref):
    h = _rmsnorm(x_ref[...], nw_ref[...]).astype(BF16)
    hist = SHORT_CONV - 1
    for c in range(QKV_DIM // LANES):
        cs = slice(c * LANES, (c + 1) * LANES)
        new = jnp.dot(h, wqkv_ref[:, cs], preferred_element_type=F32)
        acc = new * cw_ref[hist:hist + 1, cs]
        for j in range(hist):
            row = sqkv_ref[:, j * QKV_DIM + c * LANES:j * QKV_DIM + (c + 1) * LANES]
            acc = acc + row * cw_ref[j:j + 1, cs]
            if j > 0:
                nqkv_ref[:, (j - 1) * QKV_DIM + c * LANES:(j - 1) * QKV_DIM + (c + 1) * LANES] = row
        nqkv_ref[:, (hist - 1) * QKV_DIM + c * LANES:(hist - 1) * QKV_DIM + (c + 1) * LANES] = new
        y = _qkv_post(acc, c)
        if c < N_HEADS:
            q_ref[:, cs] = y
        elif c < 2 * N_HEADS:
            k_ref[:, (c - N_HEADS) * LANES:(c - N_HEADS + 1) * LANES] = y
        else:
            v_ref[:, (c - 2 * N_HEADS) * LANES:(c - 2 * N_HEADS + 1) * LANES] = y

    z_ref[...] = jnp.dot(h, wz_ref[...], preferred_element_type=F32)
    ba = jnp.dot(h, wba_ref[...], preferred_element_type=F32)
    beta, g = _beta_decay(ba, alog_ref[...], dtb_ref[...])
    lane = lax.broadcasted_iota(jnp.int32, ba.shape, 1)
    ba_ref[...] = jnp.where(lane < N_HEADS, beta, jnp.exp(g))

    gl = jnp.dot(h, wglu_ref[...], preferred_element_type=F32)
    u = gl[:, :CONF_DIM] * jax.nn.sigmoid(gl[:, CONF_DIM:])
    chist = CONF_CONV - 1
    acc = u * ccw_ref[chist:chist + 1, :]
    for j in range(chist):
        row = sconf_ref[:, j * CONF_DIM:(j + 1) * CONF_DIM]
        acc = acc + row * ccw_ref[j:j + 1, :]
        if j > 0:
            nconf_ref[:, (j - 1) * CONF_DIM:j * CONF_DIM] = row
    nconf_ref[:, (chist - 1) * CONF_DIM:chist * CONF_DIM] = u
    cact = _silu(_layernorm(acc + ccb_ref[...], lnw_ref[...], lnb_ref[...]))
    yb = _mm(cact, woc_ref[...])
    sga_ref[...] = jax.nn.sigmoid(jnp.dot(h, wga_ref[...], preferred_element_type=F32))
    mb_ref[...] = jax.nn.sigmoid(jnp.dot(h, wgb_ref[...], preferred_element_type=F32)) * yb


def _delta_step_kernel(beta_ref, a_ref, qt_ref, kt_ref, v_ref, s_ref, o_ref, snew_ref, *, bt):
    b0 = pl.program_id(0) * bt
    for i in range(bt):
        kt = kt_ref[i]
        qt = qt_ref[i]
        for h in range(N_HEADS):
            beta = beta_ref[(b0 + i) * N_HEADS + h]
            a = a_ref[(b0 + i) * N_HEADS + h]
            s = s_ref[i, h]
            kcol = kt[:, h:h + 1]
            r = jnp.sum(s * kcol, axis=0, keepdims=True)
            d = beta * (v_ref[i, h:h + 1, :] - a * r)
            sn = a * s + kcol * d
            snew_ref[i, h] = sn
            o_ref[i, h:h + 1, :] = jnp.sum(sn * qt[:, h:h + 1], axis=0, keepdims=True)


def _mixer_out_ffn_sample_kernel(o_ref, z_ref, sga_ref, mb_ref, x_ref, sffn_ref, dnw_ref, wod_ref, wout_ref,
                                 nfw_ref, wup_ref, fcw_ref, fcb_ref, wdown_ref, finw_ref,
                                 out_ref, nffn_ref, act_buf, *, final):
    on = _delta_out_norm(o_ref, z_ref, dnw_ref[...])
    ya = jnp.dot(on, wod_ref[...], preferred_element_type=F32)
    merged = sga_ref[...] * ya + mb_ref[...]
    x1 = x_ref[...] + _mm(merged, wout_ref[...])
    h2 = _rmsnorm(x1, nfw_ref[...]).astype(BF16)
    hist = FFN_CONV - 1
    for c in range(D_FF // LANES):
        acts = []
        for off in (0, D_FF):
            cs = slice(off + c * LANES, off + (c + 1) * LANES)
            new = jnp.dot(h2, wup_ref[:, cs], preferred_element_type=F32)
            acc = new * fcw_ref[hist:hist + 1, cs]
            for j in range(hist):
                row = sffn_ref[:, j * 2 * D_FF + off + c * LANES:j * 2 * D_FF + off + (c + 1) * LANES]
                acc = acc + row * fcw_ref[j:j + 1, cs]
                if j > 0:
                    nffn_ref[:, (j - 1) * 2 * D_FF + off + c * LANES:(j - 1) * 2 * D_FF + off + (c + 1) * LANES] = row
            nffn_ref[:, (hist - 1) * 2 * D_FF + off + c * LANES:(hist - 1) * 2 * D_FF + off + (c + 1) * LANES] = new
            acts.append(acc + fcb_ref[:, cs])
        act_buf[:, c * LANES:(c + 1) * LANES] = (_silu(acts[0]) * acts[1]).astype(BF16)
    x2 = x1 + jnp.dot(act_buf[...], wdown_ref[...], preferred_element_type=F32)
    out_ref[...] = _rmsnorm(x2, finw_ref[...]) if final else x2


def _params(n_grid):
    return pltpu.CompilerParams(dimension_semantics=("arbitrary",) * n_grid, vmem_limit_bytes=VMEM_LIMIT)


def _layer_block(arr, l, width=None, blk=0):
    rows, cols = arr.shape[1:]
    return pl.BlockSpec((None, rows, cols if width is None else width), lambda *_: (l, 0, blk),
                        pipeline_mode=pl.Buffered(1))


def _whole(arr):
    nd = len(arr.shape)
    return pl.BlockSpec(arr.shape, lambda *_: (0,) * nd, pipeline_mode=pl.Buffered(1))


def _mixer_in_consts(pw, l, head_major):
    names = ["norm_mix_w", ("w_in", QKV_DIM, 0), ("w_in", D_MODEL, QKV_DIM // D_MODEL), "w_ba"]
    names += ["w_at"] if head_major else []
    names += [("w_tail", D_MODEL, 0), ("w_tail", D_MODEL, 1), ("w_tail", D_MODEL, 2),
              "conv_qkv_w", "alog_row", "dtb_row"]
    names += ["alog_col", "dtb_col"] if head_major else []
    names += ["conf_conv_w", "conf_conv_b", "conf_ln_w", "conf_ln_b", "w_o_conf"]
    out = []
    for n in names:
        if isinstance(n, tuple):
            out.append((pw[n[0]], _layer_block(pw[n[0]], l, n[1], n[2])))
        else:
            out.append((pw[n], _layer_block(pw[n], l)))
    return out


def _mixer_out_consts(pw, l):
    names = ("delta_norm_w", "w_o_delta", "w_out", "norm_ffn_w", "w_up", "ffn_conv_w", "ffn_conv_b", "w_down")
    return [(pw[n], _layer_block(pw[n], l)) for n in names] + [(pw["norm_final_w"], _layer_block(pw["norm_final_w"], 0))]


def _mixer_in_prompt(x, pw, l):
    nb, seq, _ = x.shape
    tm = TILE_M
    tok = lambda n: pl.BlockSpec((None, tm, n), lambda b, t: (b, t, 0))
    consts = _mixer_in_consts(pw, l, head_major=True)
    act = jax.ShapeDtypeStruct((nb, seq, D_MODEL), F32)
    return pl.pallas_call(
        functools.partial(_mixer_in_prompt_kernel, tm=tm),
        grid=(nb, seq // tm),
        in_specs=[tok(D_MODEL)] + [c[1] for c in consts],
        out_specs=[tok(D_MODEL)] * 6 + [
            tok(LANES),
            pl.BlockSpec((None, N_HEADS, tm), lambda b, t: (b, 0, t)),
            pl.BlockSpec((None, SHORT_CONV - 1, QKV_DIM), lambda b, t: (b, 0, 0)),
            pl.BlockSpec((None, CONF_CONV - 1, CONF_DIM), lambda b, t: (b, 0, 0)),
        ],
        out_shape=[act] * 6 + [
            jax.ShapeDtypeStruct((nb, seq, LANES), F32),
            jax.ShapeDtypeStruct((nb, N_HEADS, seq), F32),
            jax.ShapeDtypeStruct((nb, SHORT_CONV - 1, QKV_DIM), F32),
            jax.ShapeDtypeStruct((nb, CONF_CONV - 1, CONF_DIM), F32),
        ],
        scratch_shapes=[pltpu.VMEM((QKV_HIST + tm, QKV_DIM), F32),
                        pltpu.VMEM((CONF_HIST + tm, CONF_DIM), F32),
                        pltpu.VMEM((tm, CONF_DIM), F32)],
        compiler_params=_params(2),
        name="mixer_in_prompt",
    )(x, *[c[0] for c in consts])


def _delta_prompt(q, k, v, bgc, gct):
    nb, seq, _ = q.shape
    tok = lambda n: pl.BlockSpec((None, CHUNK, n), lambda b, t: (b, t, 0))
    return pl.pallas_call(
        _delta_chunk_kernel,
        grid=(nb, seq // CHUNK),
        in_specs=[tok(D_MODEL)] * 3 + [tok(LANES), pl.BlockSpec((None, N_HEADS, CHUNK), lambda b, t: (b, 0, t))],
        out_specs=[tok(D_MODEL),
                   pl.BlockSpec((None, N_HEADS, HEAD_DIM, HEAD_DIM), lambda b, t: (b, 0, 0, 0))],
        out_shape=[jax.ShapeDtypeStruct((nb, seq, D_MODEL), F32),
                   jax.ShapeDtypeStruct((nb, N_HEADS, HEAD_DIM, HEAD_DIM), F32)],
        compiler_params=_params(2),
        name="delta_prompt",
    )(q, k, v, bgc, gct)


def _mixer_out_ffn_prompt(o, z, sga, mb, x, pw, l, final):
    nb, seq, _ = x.shape
    tm = TILE_M
    tok = pl.BlockSpec((None, tm, D_MODEL), lambda b, t: (b, t, 0))
    consts = _mixer_out_consts(pw, l)
    return pl.pallas_call(
        functools.partial(_mixer_out_ffn_prompt_kernel, tm=tm, final=final),
        grid=(nb, seq // tm),
        in_specs=[tok] * 5 + [c[1] for c in consts],
        out_specs=[tok, pl.BlockSpec((None, FFN_CONV - 1, 2 * D_FF), lambda b, t: (b, 0, 0))],
        out_shape=[jax.ShapeDtypeStruct((nb, seq, D_MODEL), F32),
                   jax.ShapeDtypeStruct((nb, FFN_CONV - 1, 2 * D_FF), F32)],
        scratch_shapes=[pltpu.VMEM((FFN_HIST + tm, 2 * D_FF), F32), pltpu.VMEM((tm, D_FF), BF16)],
        compiler_params=_params(2),
        name="mixer_out_ffn_prompt",
    )(o, z, sga, mb, x, *[c[0] for c in consts])


def _mixer_in_sample(x, sqkv, sconf, pw, l):
    nb = x.shape[0]
    consts = _mixer_in_consts(pw, l, head_major=False)
    act = jax.ShapeDtypeStruct((nb, D_MODEL), F32)
    outs = [act] * 6 + [jax.ShapeDtypeStruct((nb, LANES), F32),
                        jax.ShapeDtypeStruct(sqkv.shape, F32), jax.ShapeDtypeStruct(sconf.shape, F32)]
    return pl.pallas_call(
        _mixer_in_sample_kernel,
        grid=(1,),
        in_specs=[_whole(a) for a in (x, sqkv, sconf)] + [c[1] for c in consts],
        out_specs=[_whole(a) for a in outs],
        out_shape=outs,
        compiler_params=_params(1),
        name="mixer_in_sample",
    )(x, sqkv, sconf, *[c[0] for c in consts])


def _delta_sample(beta, a, qt, kt, v, s):
    nb = s.shape[0]
    bt = 8
    smem = pl.BlockSpec(memory_space=pltpu.SMEM)
    return pl.pallas_call(
        functools.partial(_delta_step_kernel, bt=bt),
        grid=(nb // bt,),
        in_specs=[smem, smem,
                  pl.BlockSpec((bt, HEAD_DIM, N_HEADS), lambda i: (i, 0, 0)),
                  pl.BlockSpec((bt, HEAD_DIM, N_HEADS), lambda i: (i, 0, 0)),
                  pl.BlockSpec((bt, N_HEADS, HEAD_DIM), lambda i: (i, 0, 0)),
                  pl.BlockSpec((bt, N_HEADS, HEAD_DIM, HEAD_DIM), lambda i: (i, 0, 0, 0))],
        out_specs=[pl.BlockSpec((bt, N_HEADS, HEAD_DIM), lambda i: (i, 0, 0)),
                   pl.BlockSpec((bt, N_HEADS, HEAD_DIM, HEAD_DIM), lambda i: (i, 0, 0, 0))],
        out_shape=[jax.ShapeDtypeStruct((nb, N_HEADS, HEAD_DIM), F32),
                   jax.ShapeDtypeStruct(s.shape, F32)],
        compiler_params=_params(1),
        name="delta_sample",
    )(beta, a, qt, kt, v, s)


def _mixer_out_ffn_sample(o, z, sga, mb, x, sffn, pw, l, final):
    nb = x.shape[0]
    consts = _mixer_out_consts(pw, l)
    outs = [jax.ShapeDtypeStruct((nb, D_MODEL), F32), jax.ShapeDtypeStruct(sffn.shape, F32)]
    return pl.pallas_call(
        functools.partial(_mixer_out_ffn_sample_kernel, final=final),
        grid=(1,),
        in_specs=[_whole(a) for a in (o, z, sga, mb, x, sffn)] + [c[1] for c in consts],
        out_specs=[_whole(a) for a in outs],
        out_shape=outs,
        scratch_shapes=[pltpu.VMEM((nb, D_FF), BF16)],
        compiler_params=_params(1),
        name="mixer_out_ffn_sample",
    )(o, z, sga, mb, x, sffn, *[c[0] for c in consts])


def _prepare_weights(norm_mix_w, w_in, conv_qkv_w, a_log, dt_bias, delta_norm_w, w_o_delta,
                     conf_conv_w, conf_conv_b, conf_ln_w, conf_ln_b, w_o_conf, w_out,
                     norm_ffn_w, w_up, ffn_conv_w, ffn_conv_b, w_down, norm_final_w):
    depth = w_in.shape[0]
    w16 = w_in.astype(BF16)
    o_b = QKV_DIM + N_HEADS * HEAD_DIM
    o_a = o_b + N_HEADS
    o_glu = o_a + N_HEADS
    pad = LANES - 2 * N_HEADS
    rows = lambda p: p.reshape(depth, 1, -1)
    lane_pad = lambda p: jnp.pad(p.reshape(depth, 1, N_HEADS), ((0, 0), (0, 0), (N_HEADS, pad)))
    return {
        "norm_mix_w": rows(norm_mix_w),
        "w_in": w16,
        "w_tail": w16[:, :, o_glu:],
        "w_ba": jnp.pad(w16[:, :, o_b:o_glu], ((0, 0), (0, 0), (0, pad))),
        "w_at": jnp.swapaxes(w16[:, :, o_a:o_glu], 1, 2),
        "conv_qkv_w": conv_qkv_w,
        "alog_row": lane_pad(a_log), "dtb_row": lane_pad(dt_bias),
        "alog_col": a_log.reshape(depth, N_HEADS, 1), "dtb_col": dt_bias.reshape(depth, N_HEADS, 1),
        "conf_conv_w": conf_conv_w, "conf_conv_b": rows(conf_conv_b),
        "conf_ln_w": rows(conf_ln_w), "conf_ln_b": rows(conf_ln_b),
        "w_o_conf": w_o_conf.astype(BF16),
        "delta_norm_w": rows(delta_norm_w),
        "w_o_delta": w_o_delta.astype(BF16),
        "w_out": w_out.astype(BF16),
        "norm_ffn_w": rows(norm_ffn_w),
        "w_up": w_up.astype(BF16),
        "ffn_conv_w": ffn_conv_w, "ffn_conv_b": rows(ffn_conv_b),
        "w_down": w_down.astype(BF16),
        "norm_final_w": norm_final_w.reshape(1, 1, -1),
    }


def kernel(x_prompt, x_sample, state_delta, state_qkv_conv, state_conf_conv, state_ffn_conv, norm_mix_w, w_in, conv_qkv_w, a_log, dt_bias, delta_norm_w, w_o_delta, conf_conv_w, conf_conv_b, conf_ln_w, conf_ln_b, w_o_conf, w_out, norm_ffn_w, w_up, ffn_conv_w, ffn_conv_b, w_down, norm_final_w):
    depth = w_in.shape[0]
    nb_s = x_sample.shape[0]
    pw = _prepare_weights(norm_mix_w, w_in, conv_qkv_w, a_log, dt_bias, delta_norm_w, w_o_delta,
                          conf_conv_w, conf_conv_b, conf_ln_w, conf_ln_b, w_o_conf, w_out,
                          norm_ffn_w, w_up, ffn_conv_w, ffn_conv_b, w_down, norm_final_w)
    xp = x_prompt
    xs = x_sample.reshape(nb_s, D_MODEL)
    new_p = ([], [], [], [])
    new_s = ([], [], [], [])
    for l in range(depth):
        final = l == depth - 1
        q, k, v, z, sga, mb, bgc, gct, n_qkv, n_conf = _mixer_in_prompt(xp, pw, l)
        o, n_delta = _delta_prompt(q, k, v, bgc, gct)
        xp, n_ffn = _mixer_out_ffn_prompt(o, z, sga, mb, xp, pw, l, final)
        for lst, val in zip(new_p, (n_delta, n_qkv, n_conf, n_ffn)):
            lst.append(val)
        q, k, v, z, sga, mb, ba, n_qkv, n_conf = _mixer_in_sample(
            xs, state_qkv_conv[l].reshape(nb_s, -1), state_conf_conv[l].reshape(nb_s, -1), pw, l)
        heads = lambda a: a.reshape(nb_s, N_HEADS, HEAD_DIM)
        o, n_delta = _delta_sample(ba[:, :N_HEADS].reshape(-1), ba[:, N_HEADS:2 * N_HEADS].reshape(-1),
                                   heads(q).transpose(0, 2, 1), heads(k).transpose(0, 2, 1), heads(v),
                                   state_delta[l])
        xs, n_ffn = _mixer_out_ffn_sample(o.reshape(nb_s, D_MODEL), z, sga, mb, xs,
                                          state_ffn_conv[l].reshape(nb_s, -1), pw, l, final)
        for lst, val in zip(new_s, (n_delta, n_qkv.reshape(state_qkv_conv.shape[1:]),
                                    n_conf.reshape(state_conf_conv.shape[1:]),
                                    n_ffn.reshape(state_ffn_conv.shape[1:]))):
            lst.append(val)
    return (xp, xs.reshape(x_sample.shape),
            jnp.stack(new_p[0]), jnp.stack(new_p[1]), jnp.stack(new_p[2]), jnp.stack(new_p[3]),
            jnp.stack(new_s[0]), jnp.stack(new_s[1]), jnp.stack(new_s[2]), jnp.stack(new_s[3]))
```

```python
import functools

import jax
import jax.numpy as jnp
from jax import lax
from jax.experimental import pallas as pl
from jax.experimental.pallas import tpu as pltpu

F32 = jnp.float32
BF16 = jnp.bfloat16

D_MODEL = 1024
N_HEADS = 8
HEAD_DIM = 128
QKV_DIM = 3 * N_HEADS * HEAD_DIM
SHORT_CONV = 4
CONF_DIM = D_MODEL // 2
CONF_CONV = 31
D_FF = 2816
FFN_CONV = 3
NORM_EPS = 1e-6
Q_SCALE = HEAD_DIM ** -0.5

LANES = 128
SUBLANES = 8
MXU_COLS = 256
CHUNK = 128
SLABS = CHUNK // SUBLANES
TILE_M = 256
QKV_TAIL = SUBLANES * (SHORT_CONV - 1)
FFN_HIST = 8
VMEM_LIMIT = 60000 * 1024


def _mm(a, b):
    return jnp.dot(a.astype(BF16), b.astype(BF16), preferred_element_type=F32)


def _mm_nt(a, b):
    return lax.dot_general(a.astype(BF16), b.astype(BF16), (((1,), (1,)), ((), ())),
                           preferred_element_type=F32)


def _mm_tn(a, b):
    return lax.dot_general(a.astype(BF16), b.astype(BF16), (((0,), (0,)), ((), ())),
                           preferred_element_type=F32)


def _split3(x):
    hi = x.astype(BF16)
    r1 = x - hi.astype(F32)
    mid = r1.astype(BF16)
    lo = (r1 - mid.astype(F32)).astype(BF16)
    return hi, mid, lo


def _silu(x):
    return x * jax.nn.sigmoid(x)


def _softplus(x):
    return jnp.maximum(x, 0.0) + jnp.log1p(jnp.exp(-jnp.abs(x)))


def _rmsnorm(x, w):
    return x * lax.rsqrt(jnp.mean(x * x, axis=-1, keepdims=True) + NORM_EPS) * w


def _layernorm(x, w, b):
    mu = jnp.mean(x, axis=-1, keepdims=True)
    xc = x - mu
    return xc * lax.rsqrt(jnp.mean(xc * xc, axis=-1, keepdims=True) + NORM_EPS) * w + b


def _token_of_row(r):
    w = r % CHUNK
    return (r // CHUNK) * CHUNK + (w % SUBLANES) * SLABS + w // SUBLANES


def _row_of_token(tok):
    return (tok % SLABS) * SUBLANES + tok // SLABS


def _perm_matrix(n, inverse):
    i = lax.broadcasted_iota(jnp.int32, (n, n), 0)
    j = lax.broadcasted_iota(jnp.int32, (n, n), 1)
    hit = (i == _token_of_row(j)) if inverse else (j == _token_of_row(i))
    return jnp.where(hit, 1.0, 0.0).astype(BF16)


def _chunk_tri(n, upper):
    r = _token_of_row(lax.broadcasted_iota(jnp.int32, (n, n), 0))
    c = _token_of_row(lax.broadcasted_iota(jnp.int32, (n, n), 1))
    same = (r // CHUNK) == (c // CHUNK)
    tri = (r <= c) if upper else (r >= c)
    return jnp.where(same & tri, 1.0, 0.0).astype(BF16)


def _sublane_shifted(cur, prev, e):
    n = cur.shape[0]
    sub = lax.broadcasted_iota(jnp.int32, cur.shape, 0) % SUBLANES
    return jnp.where(sub >= e, pltpu.roll(cur, e, 0), pltpu.roll(prev, n - SUBLANES + e, 0))


def _delayed(zs, d):
    a, b = divmod(d, SLABS)
    if b == 0:
        return zs[a]
    cut = CHUNK - SUBLANES * b
    return jnp.concatenate([zs[a + 1][cut:], zs[a][:cut]], axis=0)


def _qkv_post(y, c):
    y = _silu(y)
    if c < 2 * N_HEADS:
        y = y * lax.rsqrt(jnp.sum(y * y, axis=-1, keepdims=True) + NORM_EPS)
        if c < N_HEADS:
            y = y * Q_SCALE
    return y


def _beta_decay(ba, alog, dtb):
    lane = lax.broadcasted_iota(jnp.int32, ba.shape, 1)
    beta = jnp.where(lane < N_HEADS, jax.nn.sigmoid(ba), 0.0)
    g = -jnp.exp(alog) * _softplus(ba + dtb)
    g = jnp.where((lane >= N_HEADS) & (lane < 2 * N_HEADS), g, 0.0)
    return beta, g


def _mixer_in_prompt_kernel(x_ref, nw_ref, wqkv_ref, wz_ref, wba_ref, wat_ref, wglu_ref, wga_ref, wgb_ref,
                            cw_ref, alog_ref, dtb_ref, alogc_ref, dtbc_ref,
                            ccw_ref, ccb_ref, lnw_ref, lnb_ref, woc_ref,
                            q_ref, k_ref, v_ref, z_ref, sga_ref, mb_ref, bgc_ref, gct_ref,
                            nqkv_ref, nconf_ref,
                            cc_buf, qkv_carry, u_carry, *, tm):
    t = pl.program_id(1)

    @pl.when(t == 0)
    def _():
        qkv_carry[...] = jnp.zeros(qkv_carry.shape, F32)
        u_carry[...] = jnp.zeros(u_carry.shape, F32)

    xh = _rmsnorm(x_ref[...], nw_ref[...]).astype(BF16)
    h = jnp.dot(_perm_matrix(tm, inverse=False), xh, preferred_element_type=F32).astype(BF16)
    last = tm - CHUNK
    slabs_per_group = MXU_COLS // LANES

    def state_rows(n_hist):
        return [last + _row_of_token(CHUNK - n_hist + i) for i in range(n_hist)]

    for g in range(CONF_DIM // MXU_COLS):
        gs = slice(g * MXU_COLS, (g + 1) * MXU_COLS)
        u = (jnp.dot(h, wglu_ref[:, gs], preferred_element_type=F32)
             * jax.nn.sigmoid(jnp.dot(h, wglu_ref[:, CONF_DIM + g * MXU_COLS:CONF_DIM + (g + 1) * MXU_COLS],
                                      preferred_element_type=F32)))
        for i, r in enumerate(state_rows(CONF_CONV - 1)):
            nconf_ref[i:i + 1, gs] = u[r:r + 1, :]
        for ci in range(tm // CHUNK):
            r0 = ci * CHUNK
            for j in range(slabs_per_group):
                cs = slice(g * MXU_COLS + j * LANES, g * MXU_COLS + (j + 1) * LANES)
                js = slice(j * LANES, (j + 1) * LANES)
                cur = u[r0:r0 + CHUNK, js]
                prev = u_carry[:, cs] if ci == 0 else u[r0 - CHUNK:r0, js]
                zs = (cur, _sublane_shifted(cur, prev, 1), _sublane_shifted(cur, prev, 2))
                acc = cur * ccw_ref[CONF_CONV - 1:CONF_CONV, cs]
                for d in range(1, CONF_CONV):
                    acc = acc + _delayed(zs, d) * ccw_ref[CONF_CONV - 1 - d:CONF_CONV - d, cs]
                cc_buf[r0:r0 + CHUNK, cs] = acc + ccb_ref[:, cs]
        u_carry[:, gs] = u[last:, :]
    cact = _silu(_layernorm(cc_buf[...], lnw_ref[...], lnb_ref[...])).astype(BF16)

    for g in range(QKV_DIM // MXU_COLS):
        gs = slice(g * MXU_COLS, (g + 1) * MXU_COLS)
        proj = jnp.dot(h, wqkv_ref[:, gs], preferred_element_type=F32)
        for i, r in enumerate(state_rows(SHORT_CONV - 1)):
            nqkv_ref[i:i + 1, gs] = proj[r:r + 1, :]
        for ci in range(tm // CHUNK):
            r0 = ci * CHUNK
            for j in range(slabs_per_group):
                c = g * slabs_per_group + j
                cs = slice(c * LANES, (c + 1) * LANES)
                js = slice(j * LANES, (j + 1) * LANES)
                cur = proj[r0:r0 + CHUNK, js]
                prev_tail = qkv_carry[:, cs] if ci == 0 else proj[r0 - QKV_TAIL:r0, js]
                z1 = jnp.concatenate([cur[:CHUNK - QKV_TAIL],
                                      _sublane_shifted(cur[CHUNK - QKV_TAIL:], prev_tail, 1)], axis=0)
                acc = cur * cw_ref[SHORT_CONV - 1:SHORT_CONV, cs]
                for d in range(1, SHORT_CONV):
                    acc = acc + _delayed((cur, z1), d) * cw_ref[SHORT_CONV - 1 - d:SHORT_CONV - d, cs]
                y = _qkv_post(acc, c)
                if c < N_HEADS:
                    q_ref[r0:r0 + CHUNK, cs] = y
                elif c < 2 * N_HEADS:
                    k_ref[r0:r0 + CHUNK, (c - N_HEADS) * LANES:(c - N_HEADS + 1) * LANES] = y
                else:
                    v_ref[r0:r0 + CHUNK, (c - 2 * N_HEADS) * LANES:(c - 2 * N_HEADS + 1) * LANES] = y
        qkv_carry[:, gs] = proj[tm - QKV_TAIL:, :]

    for g in range(D_MODEL // MXU_COLS):
        gs = slice(g * MXU_COLS, (g + 1) * MXU_COLS)
        z_ref[:, gs] = jnp.dot(h, wz_ref[:, gs], preferred_element_type=F32)

    ba = jnp.dot(h, wba_ref[...], preferred_element_type=F32)
    beta, g_tok = _beta_decay(ba, alog_ref[...], dtb_ref[...])
    tril = _chunk_tri(tm, upper=False)
    ghi, gmid, glo = _split3(g_tok)
    gc = (jnp.dot(tril, ghi, preferred_element_type=F32) + jnp.dot(tril, gmid, preferred_element_type=F32)
          + jnp.dot(tril, glo, preferred_element_type=F32))
    bgc_ref[...] = beta + gc
    at = _mm_nt(wat_ref[...], h)
    gt = -jnp.exp(alogc_ref[...]) * _softplus(at + dtbc_ref[...])
    triu = _chunk_tri(tm, upper=True)
    thi, tmid, tlo = _split3(gt)
    gct_ref[...] = (jnp.dot(thi, triu, preferred_element_type=F32) + jnp.dot(tmid, triu, preferred_element_type=F32)
                    + jnp.dot(tlo, triu, preferred_element_type=F32))

    for g in range(D_MODEL // MXU_COLS):
        gs = slice(g * MXU_COLS, (g + 1) * MXU_COLS)
        sga_ref[:, gs] = jax.nn.sigmoid(jnp.dot(h, wga_ref[:, gs], preferred_element_type=F32))
        mb_ref[:, gs] = jax.nn.sigmoid(jnp.dot(h, wgb_ref[:, gs], preferred_element_type=F32))

    mb_ref[...] = mb_ref[...] * jnp.dot(cact, woc_ref[...], preferred_element_type=F32)


def _delta_chunk_kernel(q_ref, k_ref, v_ref, bgc_ref, gct_ref, o_ref, s_ref):
    t = pl.program_id(1)

    @pl.when(t == 0)
    def _():
        s_ref[...] = jnp.zeros(s_ref.shape, F32)

    row = lax.broadcasted_iota(jnp.int32, (CHUNK, CHUNK), 0)
    col = lax.broadcasted_iota(jnp.int32, (CHUNK, CHUNK), 1)
    trow = _token_of_row(row)
    tcol = _token_of_row(col)
    causal = trow >= tcol
    strict = trow > tcol
    sub_diag = {}
    s_blk = 1
    while s_blk < CHUNK:
        sub_diag[s_blk] = (((trow // (2 * s_blk)) == (tcol // (2 * s_blk)))
                           & ((trow // s_blk) % 2 == 1) & ((tcol // s_blk) % 2 == 0))
        s_blk *= 2
    heads = range(N_HEADS)
    hsl = [slice(h * HEAD_DIM, (h + 1) * HEAD_DIM) for h in heads]
    beta = [bgc_ref[:, h:h + 1] for h in heads]
    gc = [bgc_ref[:, N_HEADS + h:N_HEADS + h + 1] for h in heads]
    r_last = _row_of_token(CHUNK - 1)
    glast = [gct_ref[h:h + 1, r_last:r_last + 1] for h in heads]
    eye = jnp.where(row == col, 1.0, 0.0)

    low, qk, rhs = [], [], []
    for h in heads:
        k = k_ref[:, hsl[h]]
        kb = k * beta[h]
        decay = jnp.where(causal, jnp.exp(jnp.where(causal, gc[h] - gct_ref[h:h + 1, :], 0.0)), 0.0)
        kq = _mm_nt(jnp.concatenate([kb, q_ref[:, hsl[h]]], axis=0), k)
        low.append(jnp.where(strict, kq[:CHUNK] * decay, 0.0))
        qk.append((kq[CHUNK:] * decay).astype(BF16))
        rhs.append(jnp.concatenate([v_ref[:, hsl[h]] * beta[h], kb * jnp.exp(gc[h])], axis=1).astype(BF16))

    tinv = [eye - jnp.where(sub_diag[1], low[h], 0.0) for h in heads]
    s_blk = 2
    while s_blk < CHUNK:
        t16 = [tinv[h].astype(BF16) for h in heads]
        m1 = [jnp.dot(jnp.where(sub_diag[s_blk], low[h], 0.0).astype(BF16), t16[h],
                      preferred_element_type=F32) for h in heads]
        tinv = [tinv[h] - jnp.dot(t16[h], m1[h].astype(BF16), preferred_element_type=F32) for h in heads]
        s_blk *= 2
    y = [jnp.dot(tinv[h].astype(BF16), rhs[h], preferred_element_type=F32) for h in heads]

    ws_qs = [_mm(jnp.concatenate([y[h][:, HEAD_DIM:], q_ref[:, hsl[h]] * jnp.exp(gc[h])], axis=0), s_ref[h])
             for h in heads]
    v_new = [(y[h][:, :HEAD_DIM] - ws_qs[h][:CHUNK]).astype(BF16) for h in heads]
    for h in heads:
        o_ref[:, hsl[h]] = ws_qs[h][CHUNK:] + jnp.dot(qk[h], v_new[h], preferred_element_type=F32)
    for h in heads:
        kg = k_ref[:, hsl[h]] * jnp.exp(glast[h] - gc[h])
        s_ref[h] = s_ref[h] * jnp.exp(glast[h]) + _mm_tn(kg, v_new[h])


def _delta_out_norm(o_ref, z_ref, dnw):
    parts = []
    for h in range(N_HEADS):
        hs = slice(h * HEAD_DIM, (h + 1) * HEAD_DIM)
        o = o_ref[:, hs]
        on = o * lax.rsqrt(jnp.mean(o * o, axis=-1, keepdims=True) + NORM_EPS) * dnw
        parts.append((on * _silu(z_ref[:, hs])).astype(BF16))
    return jnp.concatenate(parts, axis=1)


def _mixer_out_ffn_prompt_kernel(o_ref, z_ref, sga_ref, mb_ref, x_ref, dnw_ref, wod_ref, wout_ref,
                                 nfw_ref, wup_ref, fcw_ref, fcb_ref, wdown_ref, finw_ref,
                                 out_ref, nffn_ref, up_buf, act_buf, *, tm, final):
    t = pl.program_id(1)

    @pl.when(t == 0)
    def _():
        up_buf[0:FFN_HIST, :] = jnp.zeros((FFN_HIST, 2 * D_FF), F32)

    on = _delta_out_norm(o_ref, z_ref, dnw_ref[...])
    ya = jnp.dot(on, wod_ref[...], preferred_element_type=F32)
    merged = (sga_ref[...] * ya + mb_ref[...]).astype(BF16)
    merged = jnp.dot(_perm_matrix(tm, inverse=True), merged, preferred_element_type=F32).astype(BF16)
    x1 = x_ref[...] + jnp.dot(merged, wout_ref[...], preferred_element_type=F32)

    h2 = _rmsnorm(x1, nfw_ref[...]).astype(BF16)
    up_buf[FFN_HIST:FFN_HIST + tm, :] = jnp.dot(h2, wup_ref[...], preferred_element_type=F32)
    nffn_ref[...] = up_buf[FFN_HIST + tm - (FFN_CONV - 1):FFN_HIST + tm, :]
    base = FFN_HIST - (FFN_CONV - 1)
    for c in range(D_FF // LANES):
        acts = []
        for off in (0, D_FF):
            cs = slice(off + c * LANES, off + (c + 1) * LANES)
            acc = up_buf[base:base + tm, cs] * fcw_ref[0:1, cs]
            for j in range(1, FFN_CONV):
                acc = acc + up_buf[base + j:base + j + tm, cs] * fcw_ref[j:j + 1, cs]
            acts.append(acc + fcb_ref[:, cs])
        act_buf[:, c * LANES:(c + 1) * LANES] = (_silu(acts[0]) * acts[1]).astype(BF16)
    up_buf[0:FFN_HIST, :] = up_buf[tm:tm + FFN_HIST, :]
    x2 = x1 + jnp.dot(act_buf[...], wdown_ref[...], preferred_element_type=F32)
    out_ref[...] = _rmsnorm(x2, finw_ref[...]) if final else x2


def _mixer_in_sample_kernel(x_ref, sqkv_ref, sconf_ref, nw_ref, wqkv_ref, wz_ref, wba_ref, wglu_ref, wga_ref, wgb_ref,
                            cw_ref, alog_ref, dtb_ref, ccw_ref, ccb_ref, lnw_ref, lnb_ref, woc_ref,
                            q_ref, k_ref, v_ref, z_ref, sga_ref, mb_ref, ba_ref, nqkv_ref, nconf_ref):
    h = _rmsnorm(x_ref[...], nw_ref[...]).astype(BF16)
    hist = SHORT_CONV - 1
    for c in range(QKV_DIM // LANES):
        cs = slice(c * LANES, (c + 1) * LANES)
        new = jnp.dot(h, wqkv_ref[:, cs], preferred_element_type=F32)
        acc = new * cw_ref[hist:hist + 1, cs]
        for j in range(hist):
            row = sqkv_ref[:, j * QKV_DIM + c * LANES:j * QKV_DIM + (c + 1) * LANES]
            acc = acc + row * cw_ref[j:j + 1, cs]
            if j > 0:
                nqkv_ref[:, (j - 1) * QKV_DIM + c * LANES:(j - 1) * QKV_DIM + (c + 1) * LANES] = row
        nqkv_ref[:, (hist - 1) * QKV_DIM + c * LANES:(hist - 1) * QKV_DIM + (c + 1) * LANES] = new
        y = _qkv_post(acc, c)
        if c < N_HEADS:
            q_ref[:, cs] = y
        elif c < 2 * N_HEADS:
            k_ref[:, (c - N_HEADS) * LANES:(c - N_HEADS + 1) * LANES] = y
        else:
            v_ref[:, (c - 2 * N_HEADS) * LANES:(c - 2 * N_HEADS + 1) * LANES] = y

    z_ref[...] = jnp.dot(h, wz_ref[...], preferred_element_type=F32)
    ba = jnp.dot(h, wba_ref[...], preferred_element_type=F32)
    beta, g = _beta_decay(ba, alog_ref[...], dtb_ref[...])
    lane = lax.broadcasted_iota(jnp.int32, ba.shape, 1)
    ba_ref[...] = jnp.where(lane < N_HEADS, beta, jnp.exp(g))

    gl = jnp.dot(h, wglu_ref[...], preferred_element_type=F32)
    u = gl[:, :CONF_DIM] * jax.nn.sigmoid(gl[:, CONF_DIM:])
    chist = CONF_CONV - 1
    acc = u * ccw_ref[chist:chist + 1, :]
    for j in range(chist):
        row = sconf_ref[:, j * CONF_DIM:(j + 1) * CONF_DIM]
        acc = acc + row * ccw_ref[j:j + 1, :]
        if j > 0:
            nconf_ref[:, (j - 1) * CONF_DIM:j * CONF_DIM] = row
    nconf_ref[:, (chist - 1) * CONF_DIM:chist * CONF_DIM] = u
    cact = _silu(_layernorm(acc + ccb_ref[...], lnw_ref[...], lnb_ref[...]))
    yb = _mm(cact, woc_ref[...])
    sga_ref[...] = jax.nn.sigmoid(jnp.dot(h, wga_ref[...], preferred_element_type=F32))
    mb_ref[...] = jax.nn.sigmoid(jnp.dot(h, wgb_ref[...], preferred_element_type=F32)) * yb


def _delta_step_kernel(beta_ref, a_ref, qt_ref, kt_ref, v_ref, s_ref, *rest, bt):
    o_ref, snew_ref = rest[-2:]
    b0 = pl.program_id(0) * bt
    for i in range(bt):
        kt = kt_ref[i]
        qt = qt_ref[i]
        for h in range(N_HEADS):
            beta = beta_ref[(b0 + i) * N_HEADS + h]
            a = a_ref[(b0 + i) * N_HEADS + h]
            s = s_ref[i, h]
            kcol = kt[:, h:h + 1]
            r = jnp.sum(s * kcol, axis=0, keepdims=True)
            d = beta * (v_ref[i, h:h + 1, :] - a * r)
            sn = a * s + kcol * d
            snew_ref[i, h] = sn
            o_ref[i, h:h + 1, :] = jnp.sum(sn * qt[:, h:h + 1], axis=0, keepdims=True)


def _mixer_out_ffn_sample_kernel(o_ref, z_ref, sga_ref, mb_ref, x_ref, sffn_ref, dnw_ref, wod_ref, wout_ref,
                                 nfw_ref, wup_ref, fcw_ref, fcb_ref, wdown_ref, finw_ref,
                                 out_ref, nffn_ref, act_buf, *, final):
    on = _delta_out_norm(o_ref, z_ref, dnw_ref[...])
    ya = jnp.dot(on, wod_ref[...], preferred_element_type=F32)
    merged = sga_ref[...] * ya + mb_ref[...]
    x1 = x_ref[...] + _mm(merged, wout_ref[...])
    h2 = _rmsnorm(x1, nfw_ref[...]).astype(BF16)
    hist = FFN_CONV - 1
    for c in range(D_FF // LANES):
        acts = []
        for off in (0, D_FF):
            cs = slice(off + c * LANES, off + (c + 1) * LANES)
            new = jnp.dot(h2, wup_ref[:, cs], preferred_element_type=F32)
            acc = new * fcw_ref[hist:hist + 1, cs]
            for j in range(hist):
                row = sffn_ref[:, j * 2 * D_FF + off + c * LANES:j * 2 * D_FF + off + (c + 1) * LANES]
                acc = acc + row * fcw_ref[j:j + 1, cs]
                if j > 0:
                    nffn_ref[:, (j - 1) * 2 * D_FF + off + c * LANES:(j - 1) * 2 * D_FF + off + (c + 1) * LANES] = row
            nffn_ref[:, (hist - 1) * 2 * D_FF + off + c * LANES:(hist - 1) * 2 * D_FF + off + (c + 1) * LANES] = new
            acts.append(acc + fcb_ref[:, cs])
        act_buf[:, c * LANES:(c + 1) * LANES] = (_silu(acts[0]) * acts[1]).astype(BF16)
    x2 = x1 + jnp.dot(act_buf[...], wdown_ref[...], preferred_element_type=F32)
    out_ref[...] = _rmsnorm(x2, finw_ref[...]) if final else x2


def _params(n_grid):
    return pltpu.CompilerParams(dimension_semantics=("arbitrary",) * n_grid, vmem_limit_bytes=VMEM_LIMIT)


def _layer_block(arr, l, width=None, blk=0):
    rows, cols = arr.shape[1:]
    return pl.BlockSpec((None, rows, cols if width is None else width), lambda *_: (l, 0, blk),
                        pipeline_mode=pl.Buffered(1))


def _whole(arr):
    nd = len(arr.shape)
    return pl.BlockSpec(arr.shape, lambda *_: (0,) * nd, pipeline_mode=pl.Buffered(1))


def _mixer_in_consts(pw, l, head_major):
    names = ["norm_mix_w", ("w_in", QKV_DIM, 0), ("w_in", D_MODEL, QKV_DIM // D_MODEL), "w_ba"]
    names += ["w_at"] if head_major else []
    names += [("w_tail", D_MODEL, 0), ("w_tail", D_MODEL, 1), ("w_tail", D_MODEL, 2),
              "conv_qkv_w", "alog_row", "dtb_row"]
    names += ["alog_col", "dtb_col"] if head_major else []
    names += ["conf_conv_w", "conf_conv_b", "conf_ln_w", "conf_ln_b", "w_o_conf"]
    out = []
    for n in names:
        if isinstance(n, tuple):
            out.append((pw[n[0]], _layer_block(pw[n[0]], l, n[1], n[2])))
        else:
            out.append((pw[n], _layer_block(pw[n], l)))
    return out


def _mixer_out_consts(pw, l):
    names = ("delta_norm_w", "w_o_delta", "w_out", "norm_ffn_w", "w_up", "ffn_conv_w", "ffn_conv_b", "w_down")
    return [(pw[n], _layer_block(pw[n], l)) for n in names] + [(pw["norm_final_w"], _layer_block(pw["norm_final_w"], 0))]


def _mixer_in_prompt(x, pw, l):
    nb, seq, _ = x.shape
    tm = TILE_M
    tok = lambda n: pl.BlockSpec((None, tm, n), lambda b, t: (b, t, 0))
    consts = _mixer_in_consts(pw, l, head_major=True)
    act = jax.ShapeDtypeStruct((nb, seq, D_MODEL), F32)
    return pl.pallas_call(
        functools.partial(_mixer_in_prompt_kernel, tm=tm),
        grid=(nb, seq // tm),
        in_specs=[tok(D_MODEL)] + [c[1] for c in consts],
        out_specs=[tok(D_MODEL)] * 6 + [
            tok(LANES),
            pl.BlockSpec((None, N_HEADS, tm), lambda b, t: (b, 0, t)),
            pl.BlockSpec((None, SHORT_CONV - 1, QKV_DIM), lambda b, t: (b, 0, 0)),
            pl.BlockSpec((None, CONF_CONV - 1, CONF_DIM), lambda b, t: (b, 0, 0)),
        ],
        out_shape=[act] * 6 + [
            jax.ShapeDtypeStruct((nb, seq, LANES), F32),
            jax.ShapeDtypeStruct((nb, N_HEADS, seq), F32),
            jax.ShapeDtypeStruct((nb, SHORT_CONV - 1, QKV_DIM), F32),
            jax.ShapeDtypeStruct((nb, CONF_CONV - 1, CONF_DIM), F32),
        ],
        scratch_shapes=[pltpu.VMEM((tm, CONF_DIM), F32),
                        pltpu.VMEM((QKV_TAIL, QKV_DIM), F32),
                        pltpu.VMEM((CHUNK, CONF_DIM), F32)],
        compiler_params=_params(2),
        name="mixer_in_prompt",
    )(x, *[c[0] for c in consts])


def _delta_prompt(q, k, v, bgc, gct):
    nb, seq, _ = q.shape
    tok = lambda n: pl.BlockSpec((None, CHUNK, n), lambda b, t: (b, t, 0))
    return pl.pallas_call(
        _delta_chunk_kernel,
        grid=(nb, seq // CHUNK),
        in_specs=[tok(D_MODEL)] * 3 + [tok(LANES), pl.BlockSpec((None, N_HEADS, CHUNK), lambda b, t: (b, 0, t))],
        out_specs=[tok(D_MODEL),
                   pl.BlockSpec((None, N_HEADS, HEAD_DIM, HEAD_DIM), lambda b, t: (b, 0, 0, 0))],
        out_shape=[jax.ShapeDtypeStruct((nb, seq, D_MODEL), F32),
                   jax.ShapeDtypeStruct((nb, N_HEADS, HEAD_DIM, HEAD_DIM), F32)],
        compiler_params=_params(2),
        name="delta_prompt",
    )(q, k, v, bgc, gct)


def _mixer_out_ffn_prompt(o, z, sga, mb, x, pw, l, final):
    nb, seq, _ = x.shape
    tm = TILE_M
    tok = pl.BlockSpec((None, tm, D_MODEL), lambda b, t: (b, t, 0))
    consts = _mixer_out_consts(pw, l)
    return pl.pallas_call(
        functools.partial(_mixer_out_ffn_prompt_kernel, tm=tm, final=final),
        grid=(nb, seq // tm),
        in_specs=[tok] * 5 + [c[1] for c in consts],
        out_specs=[tok, pl.BlockSpec((None, FFN_CONV - 1, 2 * D_FF), lambda b, t: (b, 0, 0))],
        out_shape=[jax.ShapeDtypeStruct((nb, seq, D_MODEL), F32),
                   jax.ShapeDtypeStruct((nb, FFN_CONV - 1, 2 * D_FF), F32)],
        scratch_shapes=[pltpu.VMEM((FFN_HIST + tm, 2 * D_FF), F32), pltpu.VMEM((tm, D_FF), BF16)],
        compiler_params=_params(2),
        name="mixer_out_ffn_prompt",
    )(o, z, sga, mb, x, *[c[0] for c in consts])


def _mixer_in_sample(x, sqkv, sconf, pw, l):
    nb = x.shape[0]
    consts = _mixer_in_consts(pw, l, head_major=False)
    act = jax.ShapeDtypeStruct((nb, D_MODEL), F32)
    outs = [act] * 6 + [jax.ShapeDtypeStruct((nb, LANES), F32),
                        jax.ShapeDtypeStruct(sqkv.shape, F32), jax.ShapeDtypeStruct(sconf.shape, F32)]
    return pl.pallas_call(
        _mixer_in_sample_kernel,
        grid=(1,),
        in_specs=[_whole(a) for a in (x, sqkv, sconf)] + [c[1] for c in consts],
        out_specs=[_whole(a) for a in outs],
        out_shape=outs,
        compiler_params=_params(1),
        name="mixer_in_sample",
    )(x, sqkv, sconf, *[c[0] for c in consts])


def _delta_sample(beta, a, qt, kt, v, states, l, new_states):
    nb = states.shape[1]
    bt = 8
    smem = pl.BlockSpec(memory_space=pltpu.SMEM)
    state_blk = pl.BlockSpec((None, bt, N_HEADS, HEAD_DIM, HEAD_DIM), lambda i: (l, i, 0, 0, 0))
    args = [beta, a, qt, kt, v, states] + ([] if new_states is None else [new_states])
    return pl.pallas_call(
        functools.partial(_delta_step_kernel, bt=bt),
        grid=(nb // bt,),
        in_specs=[smem, smem,
                  pl.BlockSpec((bt, HEAD_DIM, N_HEADS), lambda i: (i, 0, 0)),
                  pl.BlockSpec((bt, HEAD_DIM, N_HEADS), lambda i: (i, 0, 0)),
                  pl.BlockSpec((bt, N_HEADS, HEAD_DIM), lambda i: (i, 0, 0)),
                  state_blk] + ([] if new_states is None else [pl.BlockSpec(memory_space=pl.ANY)]),
        out_specs=[pl.BlockSpec((bt, N_HEADS, HEAD_DIM), lambda i: (i, 0, 0)), state_blk],
        out_shape=[jax.ShapeDtypeStruct((nb, N_HEADS, HEAD_DIM), F32),
                   jax.ShapeDtypeStruct(states.shape, F32)],
        input_output_aliases={} if new_states is None else {len(args) - 1: 1},
        compiler_params=_params(1),
        name="delta_sample",
    )(*args)


def _mixer_out_ffn_sample(o, z, sga, mb, x, sffn, pw, l, final):
    nb = x.shape[0]
    consts = _mixer_out_consts(pw, l)
    outs = [jax.ShapeDtypeStruct((nb, D_MODEL), F32), jax.ShapeDtypeStruct(sffn.shape, F32)]
    return pl.pallas_call(
        functools.partial(_mixer_out_ffn_sample_kernel, final=final),
        grid=(1,),
        in_specs=[_whole(a) for a in (o, z, sga, mb, x, sffn)] + [c[1] for c in consts],
        out_specs=[_whole(a) for a in outs],
        out_shape=outs,
        scratch_shapes=[pltpu.VMEM((nb, D_FF), BF16)],
        compiler_params=_params(1),
        name="mixer_out_ffn_sample",
    )(o, z, sga, mb, x, sffn, *[c[0] for c in consts])


def _prepare_weights(norm_mix_w, w_in, conv_qkv_w, a_log, dt_bias, delta_norm_w, w_o_delta,
                     conf_conv_w, conf_conv_b, conf_ln_w, conf_ln_b, w_o_conf, w_out,
                     norm_ffn_w, w_up, ffn_conv_w, ffn_conv_b, w_down, norm_final_w):
    depth = w_in.shape[0]
    w16 = w_in.astype(BF16)
    o_b = QKV_DIM + N_HEADS * HEAD_DIM
    o_a = o_b + N_HEADS
    o_glu = o_a + N_HEADS
    pad = LANES - 2 * N_HEADS
    rows = lambda p: p.reshape(depth, 1, -1)
    lane_pad = lambda p: jnp.pad(p.reshape(depth, 1, N_HEADS), ((0, 0), (0, 0), (N_HEADS, pad)))
    return {
        "norm_mix_w": rows(norm_mix_w),
        "w_in": w16,
        "w_tail": w16[:, :, o_glu:],
        "w_ba": jnp.pad(w16[:, :, o_b:o_glu], ((0, 0), (0, 0), (0, pad))),
        "w_at": jnp.swapaxes(w16[:, :, o_a:o_glu], 1, 2),
        "conv_qkv_w": conv_qkv_w,
        "alog_row": lane_pad(a_log), "dtb_row": lane_pad(dt_bias),
        "alog_col": a_log.reshape(depth, N_HEADS, 1), "dtb_col": dt_bias.reshape(depth, N_HEADS, 1),
        "conf_conv_w": conf_conv_w, "conf_conv_b": rows(conf_conv_b),
        "conf_ln_w": rows(conf_ln_w), "conf_ln_b": rows(conf_ln_b),
        "w_o_conf": w_o_conf.astype(BF16),
        "delta_norm_w": rows(delta_norm_w),
        "w_o_delta": w_o_delta.astype(BF16),
        "w_out": w_out.astype(BF16),
        "norm_ffn_w": rows(norm_ffn_w),
        "w_up": w_up.astype(BF16),
        "ffn_conv_w": ffn_conv_w, "ffn_conv_b": rows(ffn_conv_b),
        "w_down": w_down.astype(BF16),
        "norm_final_w": norm_final_w.reshape(1, 1, -1),
    }


def kernel(x_prompt, x_sample, state_delta, state_qkv_conv, state_conf_conv, state_ffn_conv, norm_mix_w, w_in, conv_qkv_w, a_log, dt_bias, delta_norm_w, w_o_delta, conf_conv_w, conf_conv_b, conf_ln_w, conf_ln_b, w_o_conf, w_out, norm_ffn_w, w_up, ffn_conv_w, ffn_conv_b, w_down, norm_final_w):
    depth = w_in.shape[0]
    nb_s = x_sample.shape[0]
    pw = _prepare_weights(norm_mix_w, w_in, conv_qkv_w, a_log, dt_bias, delta_norm_w, w_o_delta,
                          conf_conv_w, conf_conv_b, conf_ln_w, conf_ln_b, w_o_conf, w_out,
                          norm_ffn_w, w_up, ffn_conv_w, ffn_conv_b, w_down, norm_final_w)
    xp = x_prompt
    xs = x_sample.reshape(nb_s, D_MODEL)
    new_p = ([], [], [], [])
    new_s = ([], [], [])
    delta_s = None
    for l in range(depth):
        final = l == depth - 1
        q, k, v, z, sga, mb, bgc, gct, n_qkv, n_conf = _mixer_in_prompt(xp, pw, l)
        o, n_delta = _delta_prompt(q, k, v, bgc, gct)
        xp, n_ffn = _mixer_out_ffn_prompt(o, z, sga, mb, xp, pw, l, final)
        for lst, val in zip(new_p, (n_delta, n_qkv, n_conf, n_ffn)):
            lst.append(val)
        q, k, v, z, sga, mb, ba, n_qkv, n_conf = _mixer_in_sample(
            xs, state_qkv_conv[l].reshape(nb_s, -1), state_conf_conv[l].reshape(nb_s, -1), pw, l)
        heads = lambda a: a.reshape(nb_s, N_HEADS, HEAD_DIM)
        o, delta_s = _delta_sample(ba[:, :N_HEADS].reshape(-1), ba[:, N_HEADS:2 * N_HEADS].reshape(-1),
                                   heads(q).transpose(0, 2, 1), heads(k).transpose(0, 2, 1), heads(v),
                                   state_delta, l, delta_s)
        xs, n_ffn = _mixer_out_ffn_sample(o.reshape(nb_s, D_MODEL), z, sga, mb, xs,
                                          state_ffn_conv[l].reshape(nb_s, -1), pw, l, final)
        for lst, val in zip(new_s, (n_qkv.reshape(state_qkv_conv.shape[1:]),
                                    n_conf.reshape(state_conf_conv.shape[1:]),
                                    n_ffn.reshape(state_ffn_conv.shape[1:]))):
            lst.append(val)
    return (xp, xs.reshape(x_sample.shape),
            jnp.stack(new_p[0]), jnp.stack(new_p[1]), jnp.stack(new_p[2]), jnp.stack(new_p[3]),
            delta_s, jnp.stack(new_s[0]), jnp.stack(new_s[1]), jnp.stack(new_s[2]))
```

```python
import functools

import jax
import jax.numpy as jnp
from jax import lax
from jax.experimental import pallas as pl
from jax.experimental.pallas import tpu as pltpu

F32 = jnp.float32
BF16 = jnp.bfloat16

D_MODEL = 1024
N_HEADS = 8
HEAD_DIM = 128
QKV_DIM = 3 * N_HEADS * HEAD_DIM
SHORT_CONV = 4
CONF_DIM = D_MODEL // 2
CONF_CONV = 31
D_FF = 2816
FFN_CONV = 3
NORM_EPS = 1e-6
Q_SCALE = HEAD_DIM ** -0.5

LANES = 128
SUBLANES = 8
MXU_COLS = 256
CHUNK = 128
SLABS = CHUNK // SUBLANES
TILE_M_IN = 512
TILE_M = 256
QKV_TAIL = SUBLANES * (SHORT_CONV - 1)
FFN_HIST = 8
VMEM_LIMIT = 60000 * 1024


def _mm(a, b):
    return jnp.dot(a.astype(BF16), b.astype(BF16), preferred_element_type=F32)


def _mm_nt(a, b):
    return lax.dot_general(a.astype(BF16), b.astype(BF16), (((1,), (1,)), ((), ())),
                           preferred_element_type=F32)


def _mm_tn(a, b):
    return lax.dot_general(a.astype(BF16), b.astype(BF16), (((0,), (0,)), ((), ())),
                           preferred_element_type=F32)


def _split3(x):
    hi = x.astype(BF16)
    r1 = x - hi.astype(F32)
    mid = r1.astype(BF16)
    lo = (r1 - mid.astype(F32)).astype(BF16)
    return hi, mid, lo


def _silu(x):
    return x * jax.nn.sigmoid(x)


def _softplus(x):
    return jnp.maximum(x, 0.0) + jnp.log1p(jnp.exp(-jnp.abs(x)))


def _rmsnorm(x, w):
    return x * lax.rsqrt(jnp.mean(x * x, axis=-1, keepdims=True) + NORM_EPS) * w


def _layernorm(x, w, b):
    mu = jnp.mean(x, axis=-1, keepdims=True)
    xc = x - mu
    return xc * lax.rsqrt(jnp.mean(xc * xc, axis=-1, keepdims=True) + NORM_EPS) * w + b


def _token_of_row(r):
    w = r % CHUNK
    return (r // CHUNK) * CHUNK + (w % SUBLANES) * SLABS + w // SUBLANES


def _row_of_token(tok):
    return (tok % SLABS) * SUBLANES + tok // SLABS


def _perm_matrix(n, inverse):
    i = lax.broadcasted_iota(jnp.int32, (n, n), 0)
    j = lax.broadcasted_iota(jnp.int32, (n, n), 1)
    hit = (i == _token_of_row(j)) if inverse else (j == _token_of_row(i))
    return jnp.where(hit, 1.0, 0.0).astype(BF16)


def _chunk_tri(n, upper):
    r = _token_of_row(lax.broadcasted_iota(jnp.int32, (n, n), 0))
    c = _token_of_row(lax.broadcasted_iota(jnp.int32, (n, n), 1))
    same = (r // CHUNK) == (c // CHUNK)
    tri = (r <= c) if upper else (r >= c)
    return jnp.where(same & tri, 1.0, 0.0).astype(BF16)


def _sublane_shifted(cur, prev, e):
    n = cur.shape[0]
    sub = lax.broadcasted_iota(jnp.int32, cur.shape, 0) % SUBLANES
    return jnp.where(sub >= e, pltpu.roll(cur, e, 0), pltpu.roll(prev, n - SUBLANES + e, 0))


def _delayed(zs, d):
    a, b = divmod(d, SLABS)
    if b == 0:
        return zs[a]
    cut = CHUNK - SUBLANES * b
    return jnp.concatenate([zs[a + 1][cut:], zs[a][:cut]], axis=0)


def _qkv_post(y, c):
    y = _silu(y)
    if c < 2 * N_HEADS:
        y = y * lax.rsqrt(jnp.sum(y * y, axis=-1, keepdims=True) + NORM_EPS)
        if c < N_HEADS:
            y = y * Q_SCALE
    return y


def _beta_decay(ba, alog, dtb):
    lane = lax.broadcasted_iota(jnp.int32, ba.shape, 1)
    beta = jnp.where(lane < N_HEADS, jax.nn.sigmoid(ba), 0.0)
    g = -jnp.exp(alog) * _softplus(ba + dtb)
    g = jnp.where((lane >= N_HEADS) & (lane < 2 * N_HEADS), g, 0.0)
    return beta, g


def _mixer_in_prompt_kernel(x_ref, perm_ref, tril_ref, triu_ref, nw_ref, wqkv_ref, wz_ref, wba_ref, wat_ref, wglu_ref, wga_ref, wgb_ref,
                            cw_ref, alog_ref, dtb_ref, alogc_ref, dtbc_ref,
                            ccw_ref, ccb_ref, lnw_ref, lnb_ref, woc_ref,
                            q_ref, k_ref, v_ref, z_ref, sga_ref, mb_ref, bgc_ref, gct_ref,
                            nqkv_ref, nconf_ref,
                            cc_buf, sgb_buf, qkv_carry, u_carry, *, tm):
    t = pl.program_id(1)

    @pl.when(t == 0)
    def _():
        qkv_carry[...] = jnp.zeros(qkv_carry.shape, F32)
        u_carry[...] = jnp.zeros(u_carry.shape, F32)

    xh = _rmsnorm(x_ref[...], nw_ref[...]).astype(BF16)
    chunks = [slice(r0, r0 + CHUNK) for r0 in range(0, tm, CHUNK)]
    h = jnp.concatenate([jnp.dot(perm_ref[...], xh[ch], preferred_element_type=F32) for ch in chunks],
                        axis=0).astype(BF16)
    last = tm - CHUNK
    slabs_per_group = MXU_COLS // LANES

    def state_rows(n_hist):
        return [last + _row_of_token(CHUNK - n_hist + i) for i in range(n_hist)]

    for g in range(CONF_DIM // MXU_COLS):
        gs = slice(g * MXU_COLS, (g + 1) * MXU_COLS)
        u = (jnp.dot(h, wglu_ref[:, gs], preferred_element_type=F32)
             * jax.nn.sigmoid(jnp.dot(h, wglu_ref[:, CONF_DIM + g * MXU_COLS:CONF_DIM + (g + 1) * MXU_COLS],
                                      preferred_element_type=F32)))
        for i, r in enumerate(state_rows(CONF_CONV - 1)):
            nconf_ref[i:i + 1, gs] = u[r:r + 1, :]
        for ci in range(tm // CHUNK):
            r0 = ci * CHUNK
            for j in range(slabs_per_group):
                cs = slice(g * MXU_COLS + j * LANES, g * MXU_COLS + (j + 1) * LANES)
                js = slice(j * LANES, (j + 1) * LANES)
                cur = u[r0:r0 + CHUNK, js]
                prev = u_carry[:, cs] if ci == 0 else u[r0 - CHUNK:r0, js]
                zs = (cur, _sublane_shifted(cur, prev, 1), _sublane_shifted(cur, prev, 2))
                acc = cur * ccw_ref[CONF_CONV - 1:CONF_CONV, cs]
                for d in range(1, CONF_CONV):
                    acc = acc + _delayed(zs, d) * ccw_ref[CONF_CONV - 1 - d:CONF_CONV - d, cs]
                cc_buf[r0:r0 + CHUNK, cs] = acc + ccb_ref[:, cs]
        u_carry[:, gs] = u[last:, :]
    cact = _silu(_layernorm(cc_buf[...], lnw_ref[...], lnb_ref[...])).astype(BF16)

    for g in range(QKV_DIM // MXU_COLS):
        gs = slice(g * MXU_COLS, (g + 1) * MXU_COLS)
        proj = jnp.dot(h, wqkv_ref[:, gs], preferred_element_type=F32)
        for i, r in enumerate(state_rows(SHORT_CONV - 1)):
            nqkv_ref[i:i + 1, gs] = proj[r:r + 1, :]
        for ci in range(tm // CHUNK):
            r0 = ci * CHUNK
            for j in range(slabs_per_group):
                c = g * slabs_per_group + j
                cs = slice(c * LANES, (c + 1) * LANES)
                js = slice(j * LANES, (j + 1) * LANES)
                cur = proj[r0:r0 + CHUNK, js]
                prev_tail = qkv_carry[:, cs] if ci == 0 else proj[r0 - QKV_TAIL:r0, js]
                z1 = jnp.concatenate([cur[:CHUNK - QKV_TAIL],
                                      _sublane_shifted(cur[CHUNK - QKV_TAIL:], prev_tail, 1)], axis=0)
                acc = cur * cw_ref[SHORT_CONV - 1:SHORT_CONV, cs]
                for d in range(1, SHORT_CONV):
                    acc = acc + _delayed((cur, z1), d) * cw_ref[SHORT_CONV - 1 - d:SHORT_CONV - d, cs]
                y = _qkv_post(acc, c)
                y = y.astype(BF16)
                if c < N_HEADS:
                    q_ref[r0:r0 + CHUNK, cs] = y
                elif c < 2 * N_HEADS:
                    k_ref[r0:r0 + CHUNK, (c - N_HEADS) * LANES:(c - N_HEADS + 1) * LANES] = y
                else:
                    v_ref[r0:r0 + CHUNK, (c - 2 * N_HEADS) * LANES:(c - 2 * N_HEADS + 1) * LANES] = y
        qkv_carry[:, gs] = proj[tm - QKV_TAIL:, :]

    for g in range(D_MODEL // MXU_COLS):
        gs = slice(g * MXU_COLS, (g + 1) * MXU_COLS)
        z_ref[:, gs] = jnp.dot(h, wz_ref[:, gs], preferred_element_type=F32).astype(BF16)

    ba = jnp.dot(h, wba_ref[...], preferred_element_type=F32)
    beta, g_tok = _beta_decay(ba, alog_ref[...], dtb_ref[...])
    at = _mm_nt(wat_ref[...], h)
    g_head = -jnp.exp(alogc_ref[...]) * _softplus(at + dtbc_ref[...])
    for ch in chunks:
        gc = sum(jnp.dot(tril_ref[...], part, preferred_element_type=F32) for part in _split3(g_tok[ch]))
        bgc_ref[ch, :] = beta[ch] + gc
        gct_ref[:, ch] = sum(jnp.dot(part, triu_ref[...], preferred_element_type=F32) for part in _split3(g_head[:, ch]))

    for g in range(D_MODEL // MXU_COLS):
        gs = slice(g * MXU_COLS, (g + 1) * MXU_COLS)
        sga_ref[:, gs] = jax.nn.sigmoid(jnp.dot(h, wga_ref[:, gs], preferred_element_type=F32)).astype(BF16)
        sgb_buf[:, gs] = jax.nn.sigmoid(jnp.dot(h, wgb_ref[:, gs], preferred_element_type=F32))

    mb_ref[...] = (sgb_buf[...] * jnp.dot(cact, woc_ref[...], preferred_element_type=F32)).astype(BF16)


def _delta_masks():
    row = lax.broadcasted_iota(jnp.int32, (CHUNK, CHUNK), 0)
    col = lax.broadcasted_iota(jnp.int32, (CHUNK, CHUNK), 1)
    trow = _token_of_row(row)
    tcol = _token_of_row(col)
    masks = [trow >= tcol, trow > tcol, row == col]
    s_blk = 1
    while s_blk < CHUNK:
        masks.append(((trow // (2 * s_blk)) == (tcol // (2 * s_blk)))
                     & ((trow // s_blk) % 2 == 1) & ((tcol // s_blk) % 2 == 0))
        s_blk *= 2
    return jnp.stack(masks).astype(F32)


def _delta_chunk_kernel(q_ref, k_ref, v_ref, bgc_ref, gct_ref, mask_ref, o_ref, s_ref):
    t = pl.program_id(1)

    @pl.when(t == 0)
    def _():
        s_ref[...] = jnp.zeros(s_ref.shape, F32)

    causal, strict, eye = 0, 1, 2
    sub_diag = {1 << i: 3 + i for i in range(CHUNK.bit_length() - 1)}
    heads = range(N_HEADS)
    hsl = [slice(h * HEAD_DIM, (h + 1) * HEAD_DIM) for h in heads]
    beta = [bgc_ref[:, h:h + 1] for h in heads]
    gc = [bgc_ref[:, N_HEADS + h:N_HEADS + h + 1] for h in heads]
    r_last = _row_of_token(CHUNK - 1)
    glast = [gct_ref[h:h + 1, r_last:r_last + 1] for h in heads]

    low, qk, rhs = [], [], []
    for h in heads:
        k = k_ref[:, hsl[h]].astype(F32)
        kb = k * beta[h]
        decay = jnp.exp((gc[h] - gct_ref[h:h + 1, :]) * mask_ref[causal]) * mask_ref[causal]
        kq = _mm_nt(jnp.concatenate([kb, q_ref[:, hsl[h]].astype(F32)], axis=0), k)
        low.append(kq[:CHUNK] * decay * mask_ref[strict])
        qk.append((kq[CHUNK:] * decay).astype(BF16))
        rhs.append(jnp.concatenate([v_ref[:, hsl[h]].astype(F32) * beta[h], kb * jnp.exp(gc[h])],
                                   axis=1).astype(BF16))

    tinv = [mask_ref[eye] - low[h] * mask_ref[sub_diag[1]] for h in heads]
    s_blk = 2
    while s_blk < CHUNK:
        t16 = [tinv[h].astype(BF16) for h in heads]
        m1 = [jnp.dot((low[h] * mask_ref[sub_diag[s_blk]]).astype(BF16), t16[h],
                      preferred_element_type=F32) for h in heads]
        tinv = [tinv[h] - jnp.dot(t16[h], m1[h].astype(BF16), preferred_element_type=F32) for h in heads]
        s_blk *= 2
    y = [jnp.dot(tinv[h].astype(BF16), rhs[h], preferred_element_type=F32) for h in heads]

    ws_qs = [_mm(jnp.concatenate([y[h][:, HEAD_DIM:], q_ref[:, hsl[h]].astype(F32) * jnp.exp(gc[h])], axis=0),
                 s_ref[h])
             for h in heads]
    v_new = [(y[h][:, :HEAD_DIM] - ws_qs[h][:CHUNK]).astype(BF16) for h in heads]
    for h in heads:
        o_ref[:, hsl[h]] = (ws_qs[h][CHUNK:]
                            + jnp.dot(qk[h], v_new[h], preferred_element_type=F32)).astype(o_ref.dtype)
    for h in heads:
        kg = k_ref[:, hsl[h]].astype(F32) * jnp.exp(glast[h] - gc[h])
        s_ref[h] = s_ref[h] * jnp.exp(glast[h]) + _mm_tn(kg, v_new[h])


def _delta_out_norm(o_ref, z_ref, dnw):
    parts = []
    for h in range(N_HEADS):
        hs = slice(h * HEAD_DIM, (h + 1) * HEAD_DIM)
        o = o_ref[:, hs].astype(F32)
        on = o * lax.rsqrt(jnp.mean(o * o, axis=-1, keepdims=True) + NORM_EPS) * dnw
        parts.append((on * _silu(z_ref[:, hs].astype(F32))).astype(BF16))
    return jnp.concatenate(parts, axis=1)


def _mixer_out_ffn_prompt_kernel(o_ref, z_ref, sga_ref, mb_ref, x_ref, unperm_ref, dnw_ref, wod_ref, wout_ref,
                                 nfw_ref, wup_ref, fcw_ref, fcb_ref, wdown_ref, finw_ref,
                                 out_ref, nffn_ref, up_buf, act_buf, *, tm, final):
    t = pl.program_id(1)

    @pl.when(t == 0)
    def _():
        up_buf[0:FFN_HIST, :] = jnp.zeros((FFN_HIST, 2 * D_FF), F32)

    on = _delta_out_norm(o_ref, z_ref, dnw_ref[...])
    ya = jnp.dot(on, wod_ref[...], preferred_element_type=F32)
    merged = (sga_ref[...].astype(F32) * ya + mb_ref[...].astype(F32)).astype(BF16)
    merged = jnp.concatenate([jnp.dot(unperm_ref[...], merged[r0:r0 + CHUNK], preferred_element_type=F32)
                              for r0 in range(0, tm, CHUNK)], axis=0).astype(BF16)
    x1 = x_ref[...] + jnp.dot(merged, wout_ref[...], preferred_element_type=F32)

    h2 = _rmsnorm(x1, nfw_ref[...]).astype(BF16)
    up_buf[FFN_HIST:FFN_HIST + tm, :] = jnp.dot(h2, wup_ref[...], preferred_element_type=F32)
    nffn_ref[...] = up_buf[FFN_HIST + tm - (FFN_CONV - 1):FFN_HIST + tm, :]
    base = FFN_HIST - (FFN_CONV - 1)
    for c in range(D_FF // LANES):
        acts = []
        for off in (0, D_FF):
            cs = slice(off + c * LANES, off + (c + 1) * LANES)
            acc = up_buf[base:base + tm, cs] * fcw_ref[0:1, cs]
            for j in range(1, FFN_CONV):
                acc = acc + up_buf[base + j:base + j + tm, cs] * fcw_ref[j:j + 1, cs]
            acts.append(acc + fcb_ref[:, cs])
        act_buf[:, c * LANES:(c + 1) * LANES] = (_silu(acts[0]) * acts[1]).astype(BF16)
    up_buf[0:FFN_HIST, :] = up_buf[tm:tm + FFN_HIST, :]
    x2 = x1 + jnp.dot(act_buf[...], wdown_ref[...], preferred_element_type=F32)
    out_ref[...] = _rmsnorm(x2, finw_ref[...]) if final else x2


def _mixer_in_sample_kernel(x_ref, sqkv_ref, sconf_ref, nw_ref, wqkv_ref, wz_ref, wba_ref, wglu_ref, wga_ref, wgb_ref,
                            cw_ref, alog_ref, dtb_ref, ccw_ref, ccb_ref, lnw_ref, lnb_ref, woc_ref,
                            q_ref, k_ref, v_ref, z_ref, sga_ref, mb_ref, ba_ref, nqkv_ref, nconf_ref):
    h = _rmsnorm(x_ref[...], nw_ref[...]).astype(BF16)
    hist = SHORT_CONV - 1
    for c in range(QKV_DIM // LANES):
        cs = slice(c * LANES, (c + 1) * LANES)
        new = jnp.dot(h, wqkv_ref[:, cs], preferred_element_type=F32)
        acc = new * cw_ref[hist:hist + 1, cs]
        for j in range(hist):
            row = sqkv_ref[:, j * QKV_DIM + c * LANES:j * QKV_DIM + (c + 1) * LANES]
            acc = acc + row * cw_ref[j:j + 1, cs]
            if j > 0:
                nqkv_ref[:, (j - 1) * QKV_DIM + c * LANES:(j - 1) * QKV_DIM + (c + 1) * LANES] = row
        nqkv_ref[:, (hist - 1) * QKV_DIM + c * LANES:(hist - 1) * QKV_DIM + (c + 1) * LANES] = new
        y = _qkv_post(acc, c)
        if c < N_HEADS:
            q_ref[:, cs] = y
        elif c < 2 * N_HEADS:
            k_ref[:, (c - N_HEADS) * LANES:(c - N_HEADS + 1) * LANES] = y
        else:
            v_ref[:, (c - 2 * N_HEADS) * LANES:(c - 2 * N_HEADS + 1) * LANES] = y

    z_ref[...] = jnp.dot(h, wz_ref[...], preferred_element_type=F32)
    ba = jnp.dot(h, wba_ref[...], preferred_element_type=F32)
    beta, g = _beta_decay(ba, alog_ref[...], dtb_ref[...])
    lane = lax.broadcasted_iota(jnp.int32, ba.shape, 1)
    ba_ref[...] = jnp.where(lane < N_HEADS, beta, jnp.exp(g))

    gl = jnp.dot(h, wglu_ref[...], preferred_element_type=F32)
    u = gl[:, :CONF_DIM] * jax.nn.sigmoid(gl[:, CONF_DIM:])
    chist = CONF_CONV - 1
    acc = u * ccw_ref[chist:chist + 1, :]
    for j in range(chist):
        row = sconf_ref[:, j * CONF_DIM:(j + 1) * CONF_DIM]
        acc = acc + row * ccw_ref[j:j + 1, :]
        if j > 0:
            nconf_ref[:, (j - 1) * CONF_DIM:j * CONF_DIM] = row
    nconf_ref[:, (chist - 1) * CONF_DIM:chist * CONF_DIM] = u
    cact = _silu(_layernorm(acc + ccb_ref[...], lnw_ref[...], lnb_ref[...]))
    yb = _mm(cact, woc_ref[...])
    sga_ref[...] = jax.nn.sigmoid(jnp.dot(h, wga_ref[...], preferred_element_type=F32))
    mb_ref[...] = jax.nn.sigmoid(jnp.dot(h, wgb_ref[...], preferred_element_type=F32)) * yb


def _delta_step_kernel(beta_ref, a_ref, qt_ref, kt_ref, v_ref, s_ref, *rest, bt):
    o_ref, snew_ref = rest[-2:]
    b0 = pl.program_id(0) * bt
    for i in range(bt):
        kt = kt_ref[i]
        qt = qt_ref[i]
        for h in range(N_HEADS):
            beta = beta_ref[(b0 + i) * N_HEADS + h]
            a = a_ref[(b0 + i) * N_HEADS + h]
            s = s_ref[i, h]
            kcol = kt[:, h:h + 1]
            r = jnp.sum(s * kcol, axis=0, keepdims=True)
            d = beta * (v_ref[i, h:h + 1, :] - a * r)
            sn = a * s + kcol * d
            snew_ref[i, h] = sn
            o_ref[i, h:h + 1, :] = jnp.sum(sn * qt[:, h:h + 1], axis=0, keepdims=True)


def _mixer_out_ffn_sample_kernel(o_ref, z_ref, sga_ref, mb_ref, x_ref, sffn_ref, dnw_ref, wod_ref, wout_ref,
                                 nfw_ref, wup_ref, fcw_ref, fcb_ref, wdown_ref, finw_ref,
                                 out_ref, nffn_ref, act_buf, *, final):
    on = _delta_out_norm(o_ref, z_ref, dnw_ref[...])
    ya = jnp.dot(on, wod_ref[...], preferred_element_type=F32)
    merged = sga_ref[...] * ya + mb_ref[...]
    x1 = x_ref[...] + _mm(merged, wout_ref[...])
    h2 = _rmsnorm(x1, nfw_ref[...]).astype(BF16)
    hist = FFN_CONV - 1
    for c in range(D_FF // LANES):
        acts = []
        for off in (0, D_FF):
            cs = slice(off + c * LANES, off + (c + 1) * LANES)
            new = jnp.dot(h2, wup_ref[:, cs], preferred_element_type=F32)
            acc = new * fcw_ref[hist:hist + 1, cs]
            for j in range(hist):
                row = sffn_ref[:, j * 2 * D_FF + off + c * LANES:j * 2 * D_FF + off + (c + 1) * LANES]
                acc = acc + row * fcw_ref[j:j + 1, cs]
                if j > 0:
                    nffn_ref[:, (j - 1) * 2 * D_FF + off + c * LANES:(j - 1) * 2 * D_FF + off + (c + 1) * LANES] = row
            nffn_ref[:, (hist - 1) * 2 * D_FF + off + c * LANES:(hist - 1) * 2 * D_FF + off + (c + 1) * LANES] = new
            acts.append(acc + fcb_ref[:, cs])
        act_buf[:, c * LANES:(c + 1) * LANES] = (_silu(acts[0]) * acts[1]).astype(BF16)
    x2 = x1 + jnp.dot(act_buf[...], wdown_ref[...], preferred_element_type=F32)
    out_ref[...] = _rmsnorm(x2, finw_ref[...]) if final else x2


def _params(n_grid):
    return pltpu.CompilerParams(dimension_semantics=("arbitrary",) * n_grid, vmem_limit_bytes=VMEM_LIMIT)


def _layer_block(arr, l, width=None, blk=0):
    rows, cols = arr.shape[1:]
    return pl.BlockSpec((None, rows, cols if width is None else width), lambda *_: (l, 0, blk),
                        pipeline_mode=pl.Buffered(1))


def _whole(arr):
    nd = len(arr.shape)
    return pl.BlockSpec(arr.shape, lambda *_: (0,) * nd, pipeline_mode=pl.Buffered(1))


def _mixer_in_consts(pw, l, head_major):
    names = ["norm_mix_w", ("w_in", QKV_DIM, 0), ("w_in", D_MODEL, QKV_DIM // D_MODEL), "w_ba"]
    names += ["w_at"] if head_major else []
    names += [("w_tail", D_MODEL, 0), ("w_tail", D_MODEL, 1), ("w_tail", D_MODEL, 2),
              "conv_qkv_w", "alog_row", "dtb_row"]
    names += ["alog_col", "dtb_col"] if head_major else []
    names += ["conf_conv_w", "conf_conv_b", "conf_ln_w", "conf_ln_b", "w_o_conf"]
    out = []
    for n in names:
        if isinstance(n, tuple):
            out.append((pw[n[0]], _layer_block(pw[n[0]], l, n[1], n[2])))
        else:
            out.append((pw[n], _layer_block(pw[n], l)))
    return out


def _mixer_out_consts(pw, l):
    names = ("delta_norm_w", "w_o_delta", "w_out", "norm_ffn_w", "w_up", "ffn_conv_w", "ffn_conv_b", "w_down")
    return [(pw[n], _layer_block(pw[n], l)) for n in names] + [(pw["norm_final_w"], _layer_block(pw["norm_final_w"], 0))]


def _mixer_in_prompt(x, pw, l):
    nb, seq, _ = x.shape
    tm = TILE_M_IN
    tok = lambda n: pl.BlockSpec((None, tm, n), lambda b, t: (b, t, 0))
    consts = _mixer_in_consts(pw, l, head_major=True)
    mats = [_perm_matrix(CHUNK, inverse=False), _chunk_tri(CHUNK, upper=False), _chunk_tri(CHUNK, upper=True)]
    act = jax.ShapeDtypeStruct((nb, seq, D_MODEL), BF16)
    return pl.pallas_call(
        functools.partial(_mixer_in_prompt_kernel, tm=tm),
        grid=(nb, seq // tm),
        in_specs=[tok(D_MODEL)] + [_whole(m) for m in mats] + [c[1] for c in consts],
        out_specs=[tok(D_MODEL)] * 6 + [
            tok(LANES),
            pl.BlockSpec((None, N_HEADS, tm), lambda b, t: (b, 0, t)),
            pl.BlockSpec((None, SHORT_CONV - 1, QKV_DIM), lambda b, t: (b, 0, 0)),
            pl.BlockSpec((None, CONF_CONV - 1, CONF_DIM), lambda b, t: (b, 0, 0)),
        ],
        out_shape=[act] * 6 + [
            jax.ShapeDtypeStruct((nb, seq, LANES), F32),
            jax.ShapeDtypeStruct((nb, N_HEADS, seq), F32),
            jax.ShapeDtypeStruct((nb, SHORT_CONV - 1, QKV_DIM), F32),
            jax.ShapeDtypeStruct((nb, CONF_CONV - 1, CONF_DIM), F32),
        ],
        scratch_shapes=[pltpu.VMEM((tm, CONF_DIM), F32),
                        pltpu.VMEM((tm, D_MODEL), F32),
                        pltpu.VMEM((QKV_TAIL, QKV_DIM), F32),
                        pltpu.VMEM((CHUNK, CONF_DIM), F32)],
        compiler_params=_params(2),
        name="mixer_in_prompt",
    )(x, *mats, *[c[0] for c in consts])


def _delta_prompt(q, k, v, bgc, gct):
    nb, seq, _ = q.shape
    tok = lambda n: pl.BlockSpec((None, CHUNK, n), lambda b, t: (b, t, 0))
    masks = _delta_masks()
    return pl.pallas_call(
        _delta_chunk_kernel,
        grid=(nb, seq // CHUNK),
        in_specs=[tok(D_MODEL)] * 3 + [tok(LANES), pl.BlockSpec((None, N_HEADS, CHUNK), lambda b, t: (b, 0, t)),
                  _whole(masks)],
        out_specs=[tok(D_MODEL),
                   pl.BlockSpec((None, N_HEADS, HEAD_DIM, HEAD_DIM), lambda b, t: (b, 0, 0, 0))],
        out_shape=[jax.ShapeDtypeStruct((nb, seq, D_MODEL), BF16),
                   jax.ShapeDtypeStruct((nb, N_HEADS, HEAD_DIM, HEAD_DIM), F32)],
        compiler_params=_params(2),
        name="delta_prompt",
    )(q, k, v, bgc, gct, masks)


def _mixer_out_ffn_prompt(o, z, sga, mb, x, pw, l, final):
    nb, seq, _ = x.shape
    tm = TILE_M
    tok = pl.BlockSpec((None, tm, D_MODEL), lambda b, t: (b, t, 0))
    consts = _mixer_out_consts(pw, l)
    unperm = _perm_matrix(CHUNK, inverse=True)
    return pl.pallas_call(
        functools.partial(_mixer_out_ffn_prompt_kernel, tm=tm, final=final),
        grid=(nb, seq // tm),
        in_specs=[tok] * 5 + [_whole(unperm)] + [c[1] for c in consts],
        out_specs=[tok, pl.BlockSpec((None, FFN_CONV - 1, 2 * D_FF), lambda b, t: (b, 0, 0))],
        out_shape=[jax.ShapeDtypeStruct((nb, seq, D_MODEL), F32),
                   jax.ShapeDtypeStruct((nb, FFN_CONV - 1, 2 * D_FF), F32)],
        scratch_shapes=[pltpu.VMEM((FFN_HIST + tm, 2 * D_FF), F32), pltpu.VMEM((tm, D_FF), BF16)],
        compiler_params=_params(2),
        name="mixer_out_ffn_prompt",
    )(o, z, sga, mb, x, unperm, *[c[0] for c in consts])


def _mixer_in_sample(x, sqkv, sconf, pw, l):
    nb = x.shape[0]
    consts = _mixer_in_consts(pw, l, head_major=False)
    act = jax.ShapeDtypeStruct((nb, D_MODEL), F32)
    outs = [act] * 6 + [jax.ShapeDtypeStruct((nb, LANES), F32),
                        jax.ShapeDtypeStruct(sqkv.shape, F32), jax.ShapeDtypeStruct(sconf.shape, F32)]
    return pl.pallas_call(
        _mixer_in_sample_kernel,
        grid=(1,),
        in_specs=[_whole(a) for a in (x, sqkv, sconf)] + [c[1] for c in consts],
        out_specs=[_whole(a) for a in outs],
        out_shape=outs,
        compiler_params=_params(1),
        name="mixer_in_sample",
    )(x, sqkv, sconf, *[c[0] for c in consts])


def _delta_sample(beta, a, qt, kt, v, states, l, new_states):
    nb = states.shape[1]
    bt = 8
    smem = pl.BlockSpec(memory_space=pltpu.SMEM)
    state_blk = pl.BlockSpec((None, bt, N_HEADS, HEAD_DIM, HEAD_DIM), lambda i: (l, i, 0, 0, 0))
    args = [beta, a, qt, kt, v, states] + ([] if new_states is None else [new_states])
    return pl.pallas_call(
        functools.partial(_delta_step_kernel, bt=bt),
        grid=(nb // bt,),
        in_specs=[smem, smem,
                  pl.BlockSpec((bt, HEAD_DIM, N_HEADS), lambda i: (i, 0, 0)),
                  pl.BlockSpec((bt, HEAD_DIM, N_HEADS), lambda i: (i, 0, 0)),
                  pl.BlockSpec((bt, N_HEADS, HEAD_DIM), lambda i: (i, 0, 0)),
                  state_blk] + ([] if new_states is None else [pl.BlockSpec(memory_space=pl.ANY)]),
        out_specs=[pl.BlockSpec((bt, N_HEADS, HEAD_DIM), lambda i: (i, 0, 0)), state_blk],
        out_shape=[jax.ShapeDtypeStruct((nb, N_HEADS, HEAD_DIM), F32),
                   jax.ShapeDtypeStruct(states.shape, F32)],
        input_output_aliases={} if new_states is None else {len(args) - 1: 1},
        compiler_params=_params(1),
        name="delta_sample",
    )(*args)


def _mixer_out_ffn_sample(o, z, sga, mb, x, sffn, pw, l, final):
    nb = x.shape[0]
    consts = _mixer_out_consts(pw, l)
    outs = [jax.ShapeDtypeStruct((nb, D_MODEL), F32), jax.ShapeDtypeStruct(sffn.shape, F32)]
    return pl.pallas_call(
        functools.partial(_mixer_out_ffn_sample_kernel, final=final),
        grid=(1,),
        in_specs=[_whole(a) for a in (o, z, sga, mb, x, sffn)] + [c[1] for c in consts],
        out_specs=[_whole(a) for a in outs],
        out_shape=outs,
        scratch_shapes=[pltpu.VMEM((nb, D_FF), BF16)],
        compiler_params=_params(1),
        name="mixer_out_ffn_sample",
    )(o, z, sga, mb, x, sffn, *[c[0] for c in consts])


def _prepare_weights(norm_mix_w, w_in, conv_qkv_w, a_log, dt_bias, delta_norm_w, w_o_delta,
                     conf_conv_w, conf_conv_b, conf_ln_w, conf_ln_b, w_o_conf, w_out,
                     norm_ffn_w, w_up, ffn_conv_w, ffn_conv_b, w_down, norm_final_w):
    depth = w_in.shape[0]
    o_b = QKV_DIM + N_HEADS * HEAD_DIM
    o_a = o_b + N_HEADS
    o_glu = o_a + N_HEADS
    pad = LANES - 2 * N_HEADS
    rows = lambda p: p.reshape(depth, 1, -1)
    lane_pad = lambda p: jnp.pad(p.reshape(depth, 1, N_HEADS), ((0, 0), (0, 0), (N_HEADS, pad)))
    return {
        "norm_mix_w": rows(norm_mix_w),
        "w_in": w_in[:, :, :o_b].astype(BF16),
        "w_tail": w_in[:, :, o_glu:].astype(BF16),
        "w_ba": jnp.pad(w_in[:, :, o_b:o_glu], ((0, 0), (0, 0), (0, pad))).astype(BF16),
        "w_at": jnp.swapaxes(w_in[:, :, o_a:o_glu], 1, 2).astype(BF16),
        "conv_qkv_w": conv_qkv_w,
        "alog_row": lane_pad(a_log), "dtb_row": lane_pad(dt_bias),
        "alog_col": a_log.reshape(depth, N_HEADS, 1), "dtb_col": dt_bias.reshape(depth, N_HEADS, 1),
        "conf_conv_w": conf_conv_w, "conf_conv_b": rows(conf_conv_b),
        "conf_ln_w": rows(conf_ln_w), "conf_ln_b": rows(conf_ln_b),
        "w_o_conf": w_o_conf.astype(BF16),
        "delta_norm_w": rows(delta_norm_w),
        "w_o_delta": w_o_delta.astype(BF16),
        "w_out": w_out.astype(BF16),
        "norm_ffn_w": rows(norm_ffn_w),
        "w_up": w_up.astype(BF16),
        "ffn_conv_w": ffn_conv_w, "ffn_conv_b": rows(ffn_conv_b),
        "w_down": w_down.astype(BF16),
        "norm_final_w": norm_final_w.reshape(1, 1, -1),
    }


def kernel(x_prompt, x_sample, state_delta, state_qkv_conv, state_conf_conv, state_ffn_conv, norm_mix_w, w_in, conv_qkv_w, a_log, dt_bias, delta_norm_w, w_o_delta, conf_conv_w, conf_conv_b, conf_ln_w, conf_ln_b, w_o_conf, w_out, norm_ffn_w, w_up, ffn_conv_w, ffn_conv_b, w_down, norm_final_w):
    depth = w_in.shape[0]
    nb_s = x_sample.shape[0]
    pw = _prepare_weights(norm_mix_w, w_in, conv_qkv_w, a_log, dt_bias, delta_norm_w, w_o_delta,
                          conf_conv_w, conf_conv_b, conf_ln_w, conf_ln_b, w_o_conf, w_out,
                          norm_ffn_w, w_up, ffn_conv_w, ffn_conv_b, w_down, norm_final_w)
    xp = x_prompt
    xs = x_sample.reshape(nb_s, D_MODEL)
    new_p = ([], [], [], [])
    new_s = ([], [], [])
    delta_s = None
    for l in range(depth):
        final = l == depth - 1
        q, k, v, z, sga, mb, bgc, gct, n_qkv, n_conf = _mixer_in_prompt(xp, pw, l)
        o, n_delta = _delta_prompt(q, k, v, bgc, gct)
        xp, n_ffn = _mixer_out_ffn_prompt(o, z, sga, mb, xp, pw, l, final)
        for lst, val in zip(new_p, (n_delta, n_qkv, n_conf, n_ffn)):
            lst.append(val)
        q, k, v, z, sga, mb, ba, n_qkv, n_conf = _mixer_in_sample(
            xs, state_qkv_conv[l].reshape(nb_s, -1), state_conf_conv[l].reshape(nb_s, -1), pw, l)
        heads = lambda a: a.reshape(nb_s, N_HEADS, HEAD_DIM)
        o, delta_s = _delta_sample(ba[:, :N_HEADS].reshape(-1), ba[:, N_HEADS:2 * N_HEADS].reshape(-1),
                                   heads(q).transpose(0, 2, 1), heads(k).transpose(0, 2, 1), heads(v),
                                   state_delta, l, delta_s)
        xs, n_ffn = _mixer_out_ffn_sample(o.reshape(nb_s, D_MODEL), z, sga, mb, xs,
                                          state_ffn_conv[l].reshape(nb_s, -1), pw, l, final)
        for lst, val in zip(new_s, (n_qkv.reshape(state_qkv_conv.shape[1:]),
                                    n_conf.reshape(state_conf_conv.shape[1:]),
                                    n_ffn.reshape(state_ffn_conv.shape[1:]))):
            lst.append(val)
    return (xp, xs.reshape(x_sample.shape),
            jnp.stack(new_p[0]), jnp.stack(new_p[1]), jnp.stack(new_p[2]), jnp.stack(new_p[3]),
            delta_s, jnp.stack(new_s[0]), jnp.stack(new_s[1]), jnp.stack(new_s[2]))
```

```python
import functools

import jax
import jax.numpy as jnp
from jax import lax
from jax.experimental import pallas as pl
from jax.experimental.pallas import tpu as pltpu

F32 = jnp.float32
BF16 = jnp.bfloat16

D_MODEL = 1024
N_HEADS = 8
HEAD_DIM = 128
QKV_DIM = 3 * N_HEADS * HEAD_DIM
SHORT_CONV = 4
CONF_DIM = D_MODEL // 2
CONF_CONV = 31
D_FF = 2816
FFN_CONV = 3
NORM_EPS = 1e-6
Q_SCALE = HEAD_DIM ** -0.5

LANES = 128
SUBLANES = 8
MXU_COLS = 256
CHUNK = 128
SLABS = CHUNK // SUBLANES
DELTA_CHUNKS = 2
TILE_M_IN = 512
TILE_M = 256
QKV_TAIL = SUBLANES * (SHORT_CONV - 1)
FFN_HIST = 8
VMEM_LIMIT = 60000 * 1024


def _mm(a, b):
    return jnp.dot(a.astype(BF16), b.astype(BF16), preferred_element_type=F32)


def _mm_nt(a, b):
    return lax.dot_general(a.astype(BF16), b.astype(BF16), (((1,), (1,)), ((), ())),
                           preferred_element_type=F32)


def _mm_tn(a, b):
    return lax.dot_general(a.astype(BF16), b.astype(BF16), (((0,), (0,)), ((), ())),
                           preferred_element_type=F32)


def _split3(x):
    hi = x.astype(BF16)
    r1 = x - hi.astype(F32)
    mid = r1.astype(BF16)
    lo = (r1 - mid.astype(F32)).astype(BF16)
    return hi, mid, lo


def _silu(x):
    return x * jax.nn.sigmoid(x)


def _softplus(x):
    return jnp.maximum(x, 0.0) + jnp.log1p(jnp.exp(-jnp.abs(x)))


def _rmsnorm(x, w):
    return x * lax.rsqrt(jnp.mean(x * x, axis=-1, keepdims=True) + NORM_EPS) * w


def _layernorm(x, w, b):
    mu = jnp.mean(x, axis=-1, keepdims=True)
    xc = x - mu
    return xc * lax.rsqrt(jnp.mean(xc * xc, axis=-1, keepdims=True) + NORM_EPS) * w + b


def _token_of_row(r):
    w = r % CHUNK
    return (r // CHUNK) * CHUNK + (w % SUBLANES) * SLABS + w // SUBLANES


def _row_of_token(tok):
    return (tok % SLABS) * SUBLANES + tok // SLABS


def _perm_matrix(n, inverse):
    i = lax.broadcasted_iota(jnp.int32, (n, n), 0)
    j = lax.broadcasted_iota(jnp.int32, (n, n), 1)
    hit = (i == _token_of_row(j)) if inverse else (j == _token_of_row(i))
    return jnp.where(hit, 1.0, 0.0).astype(BF16)


def _chunk_tri(n, upper):
    r = _token_of_row(lax.broadcasted_iota(jnp.int32, (n, n), 0))
    c = _token_of_row(lax.broadcasted_iota(jnp.int32, (n, n), 1))
    same = (r // CHUNK) == (c // CHUNK)
    tri = (r <= c) if upper else (r >= c)
    return jnp.where(same & tri, 1.0, 0.0).astype(BF16)


def _sublane_shifted(cur, prev, e):
    n = cur.shape[0]
    sub = lax.broadcasted_iota(jnp.int32, cur.shape, 0) % SUBLANES
    return jnp.where(sub >= e, pltpu.roll(cur, e, 0), pltpu.roll(prev, n - SUBLANES + e, 0))


def _delayed(zs, d):
    a, b = divmod(d, SLABS)
    if b == 0:
        return zs[a]
    cut = CHUNK - SUBLANES * b
    return jnp.concatenate([zs[a + 1][cut:], zs[a][:cut]], axis=0)


def _qkv_post(y, c):
    y = _silu(y)
    if c < 2 * N_HEADS:
        y = y * lax.rsqrt(jnp.sum(y * y, axis=-1, keepdims=True) + NORM_EPS)
        if c < N_HEADS:
            y = y * Q_SCALE
    return y


def _beta_decay(ba, alog, dtb):
    lane = lax.broadcasted_iota(jnp.int32, ba.shape, 1)
    beta = jnp.where(lane < N_HEADS, jax.nn.sigmoid(ba), 0.0)
    g = -jnp.exp(alog) * _softplus(ba + dtb)
    g = jnp.where((lane >= N_HEADS) & (lane < 2 * N_HEADS), g, 0.0)
    return beta, g


def _mixer_in_prompt_kernel(x_ref, perm_ref, tril_ref, triu_ref, nw_ref, wqkv_ref, wz_ref, wba_ref, wat_ref, wglu_ref, wga_ref, wgb_ref,
                            cw_ref, alog_ref, dtb_ref, alogc_ref, dtbc_ref,
                            ccw_ref, ccb_ref, lnw_ref, lnb_ref, woc_ref,
                            q_ref, k_ref, v_ref, z_ref, sga_ref, mb_ref, bgc_ref, gct_ref,
                            nqkv_ref, nconf_ref,
                            cc_buf, sgb_buf, qkv_carry, u_carry, *, tm):
    t = pl.program_id(1)

    @pl.when(t == 0)
    def _():
        qkv_carry[...] = jnp.zeros(qkv_carry.shape, F32)
        u_carry[...] = jnp.zeros(u_carry.shape, F32)

    xh = _rmsnorm(x_ref[...], nw_ref[...]).astype(BF16)
    chunks = [slice(r0, r0 + CHUNK) for r0 in range(0, tm, CHUNK)]
    h = jnp.concatenate([jnp.dot(perm_ref[...], xh[ch], preferred_element_type=F32) for ch in chunks],
                        axis=0).astype(BF16)
    last = tm - CHUNK
    slabs_per_group = MXU_COLS // LANES

    def state_rows(n_hist):
        return [last + _row_of_token(CHUNK - n_hist + i) for i in range(n_hist)]

    for g in range(CONF_DIM // MXU_COLS):
        gs = slice(g * MXU_COLS, (g + 1) * MXU_COLS)
        u = (jnp.dot(h, wglu_ref[:, gs], preferred_element_type=F32)
             * jax.nn.sigmoid(jnp.dot(h, wglu_ref[:, CONF_DIM + g * MXU_COLS:CONF_DIM + (g + 1) * MXU_COLS],
                                      preferred_element_type=F32)))
        for i, r in enumerate(state_rows(CONF_CONV - 1)):
            nconf_ref[i:i + 1, gs] = u[r:r + 1, :]
        for ci in range(tm // CHUNK):
            r0 = ci * CHUNK
            for j in range(slabs_per_group):
                cs = slice(g * MXU_COLS + j * LANES, g * MXU_COLS + (j + 1) * LANES)
                js = slice(j * LANES, (j + 1) * LANES)
                cur = u[r0:r0 + CHUNK, js]
                prev = u_carry[:, cs] if ci == 0 else u[r0 - CHUNK:r0, js]
                zs = (cur, _sublane_shifted(cur, prev, 1), _sublane_shifted(cur, prev, 2))
                acc = cur * ccw_ref[CONF_CONV - 1:CONF_CONV, cs]
                for d in range(1, CONF_CONV):
                    acc = acc + _delayed(zs, d) * ccw_ref[CONF_CONV - 1 - d:CONF_CONV - d, cs]
                cc_buf[r0:r0 + CHUNK, cs] = acc + ccb_ref[:, cs]
        u_carry[:, gs] = u[last:, :]
    cact = _silu(_layernorm(cc_buf[...], lnw_ref[...], lnb_ref[...])).astype(BF16)

    for g in range(QKV_DIM // MXU_COLS):
        gs = slice(g * MXU_COLS, (g + 1) * MXU_COLS)
        proj = jnp.dot(h, wqkv_ref[:, gs], preferred_element_type=F32)
        for i, r in enumerate(state_rows(SHORT_CONV - 1)):
            nqkv_ref[i:i + 1, gs] = proj[r:r + 1, :]
        for ci in range(tm // CHUNK):
            r0 = ci * CHUNK
            for j in range(slabs_per_group):
                c = g * slabs_per_group + j
                cs = slice(c * LANES, (c + 1) * LANES)
                js = slice(j * LANES, (j + 1) * LANES)
                cur = proj[r0:r0 + CHUNK, js]
                prev_tail = qkv_carry[:, cs] if ci == 0 else proj[r0 - QKV_TAIL:r0, js]
                z1 = jnp.concatenate([cur[:CHUNK - QKV_TAIL],
                                      _sublane_shifted(cur[CHUNK - QKV_TAIL:], prev_tail, 1)], axis=0)
                acc = cur * cw_ref[SHORT_CONV - 1:SHORT_CONV, cs]
                for d in range(1, SHORT_CONV):
                    acc = acc + _delayed((cur, z1), d) * cw_ref[SHORT_CONV - 1 - d:SHORT_CONV - d, cs]
                y = _qkv_post(acc, c)
                y = y.astype(BF16)
                if c < N_HEADS:
                    q_ref[r0:r0 + CHUNK, cs] = y
                elif c < 2 * N_HEADS:
                    k_ref[r0:r0 + CHUNK, (c - N_HEADS) * LANES:(c - N_HEADS + 1) * LANES] = y
                else:
                    v_ref[r0:r0 + CHUNK, (c - 2 * N_HEADS) * LANES:(c - 2 * N_HEADS + 1) * LANES] = y
        qkv_carry[:, gs] = proj[tm - QKV_TAIL:, :]

    for g in range(D_MODEL // MXU_COLS):
        gs = slice(g * MXU_COLS, (g + 1) * MXU_COLS)
        z_ref[:, gs] = jnp.dot(h, wz_ref[:, gs], preferred_element_type=F32).astype(BF16)

    ba = jnp.dot(h, wba_ref[...], preferred_element_type=F32)
    beta, g_tok = _beta_decay(ba, alog_ref[...], dtb_ref[...])
    at = _mm_nt(wat_ref[...], h)
    g_head = -jnp.exp(alogc_ref[...]) * _softplus(at + dtbc_ref[...])
    for ch in chunks:
        gc = sum(jnp.dot(tril_ref[...], part, preferred_element_type=F32) for part in _split3(g_tok[ch]))
        bgc_ref[ch, :] = beta[ch] + gc
        gct_ref[:, ch] = sum(jnp.dot(part, triu_ref[...], preferred_element_type=F32) for part in _split3(g_head[:, ch]))

    for g in range(D_MODEL // MXU_COLS):
        gs = slice(g * MXU_COLS, (g + 1) * MXU_COLS)
        sga_ref[:, gs] = jax.nn.sigmoid(jnp.dot(h, wga_ref[:, gs], preferred_element_type=F32)).astype(BF16)
        sgb_buf[:, gs] = jax.nn.sigmoid(jnp.dot(h, wgb_ref[:, gs], preferred_element_type=F32))

    mb_ref[...] = (sgb_buf[...] * jnp.dot(cact, woc_ref[...], preferred_element_type=F32)).astype(BF16)


def _delta_masks():
    row = lax.broadcasted_iota(jnp.int32, (CHUNK, CHUNK), 0)
    col = lax.broadcasted_iota(jnp.int32, (CHUNK, CHUNK), 1)
    trow = _token_of_row(row)
    tcol = _token_of_row(col)
    masks = [trow >= tcol, trow > tcol, row == col]
    s_blk = 1
    while s_blk < CHUNK:
        masks.append(((trow // (2 * s_blk)) == (tcol // (2 * s_blk)))
                     & ((trow // s_blk) % 2 == 1) & ((tcol // s_blk) % 2 == 0))
        s_blk *= 2
    return jnp.stack(masks).astype(F32)


def _delta_chunk_kernel(q_ref, k_ref, v_ref, bgc_ref, gct_ref, mask_ref, o_ref, s_ref, *, nc):
    t = pl.program_id(1)

    @pl.when(t == 0)
    def _():
        s_ref[...] = jnp.zeros(s_ref.shape, F32)

    causal, strict, eye = 0, 1, 2
    sub_diag = {1 << i: 3 + i for i in range(CHUNK.bit_length() - 1)}
    heads = range(N_HEADS)
    units = [(ci, h) for ci in range(nc) for h in heads]

    def rows(ci):
        return slice(ci * CHUNK, (ci + 1) * CHUNK)

    def cols(h):
        return slice(h * HEAD_DIM, (h + 1) * HEAD_DIM)

    r_last = _row_of_token(CHUNK - 1)
    beta = {(ci, h): bgc_ref[rows(ci), h:h + 1] for ci, h in units}
    gc = {(ci, h): bgc_ref[rows(ci), N_HEADS + h:N_HEADS + h + 1] for ci, h in units}
    glast = {(ci, h): gct_ref[h:h + 1, ci * CHUNK + r_last:ci * CHUNK + r_last + 1]
             for ci, h in units}

    low, qk, rhs = {}, {}, {}
    for u in units:
        ci, h = u
        k = k_ref[rows(ci), cols(h)].astype(F32)
        kb = k * beta[u]
        decay = jnp.exp((gc[u] - gct_ref[h:h + 1, rows(ci)]) * mask_ref[causal]) * mask_ref[causal]
        kq = _mm_nt(jnp.concatenate([kb, q_ref[rows(ci), cols(h)].astype(F32)], axis=0), k)
        low[u] = kq[:CHUNK] * decay * mask_ref[strict]
        qk[u] = (kq[CHUNK:] * decay).astype(BF16)
        rhs[u] = jnp.concatenate([v_ref[rows(ci), cols(h)].astype(F32) * beta[u], kb * jnp.exp(gc[u])],
                                 axis=1).astype(BF16)

    tinv = {u: mask_ref[eye] - low[u] * mask_ref[sub_diag[1]] for u in units}
    s_blk = 2
    while s_blk < CHUNK:
        t16 = {u: tinv[u].astype(BF16) for u in units}
        m1 = {u: jnp.dot((low[u] * mask_ref[sub_diag[s_blk]]).astype(BF16), t16[u],
                         preferred_element_type=F32) for u in units}
        tinv = {u: tinv[u] - jnp.dot(t16[u], m1[u].astype(BF16), preferred_element_type=F32) for u in units}
        s_blk *= 2
    y = {u: jnp.dot(tinv[u].astype(BF16), rhs[u], preferred_element_type=F32) for u in units}

    state = [s_ref[h] for h in heads]
    for ci in range(nc):
        ws_qs = [_mm(jnp.concatenate([y[ci, h][:, HEAD_DIM:],
                                      q_ref[rows(ci), cols(h)].astype(F32) * jnp.exp(gc[ci, h])], axis=0), state[h])
                 for h in heads]
        v_new = [(y[ci, h][:, :HEAD_DIM] - ws_qs[h][:CHUNK]).astype(BF16) for h in heads]
        for h in heads:
            o_ref[rows(ci), cols(h)] = (ws_qs[h][CHUNK:] + jnp.dot(qk[ci, h], v_new[h], preferred_element_type=F32)
                                        ).astype(o_ref.dtype)
        state = [state[h] * jnp.exp(glast[ci, h])
                 + _mm_tn(k_ref[rows(ci), cols(h)].astype(F32) * jnp.exp(glast[ci, h] - gc[ci, h]), v_new[h])
                 for h in heads]
    for h in heads:
        s_ref[h] = state[h]


def _delta_out_norm(o_ref, z_ref, dnw):
    parts = []
    for h in range(N_HEADS):
        hs = slice(h * HEAD_DIM, (h + 1) * HEAD_DIM)
        o = o_ref[:, hs].astype(F32)
        on = o * lax.rsqrt(jnp.mean(o * o, axis=-1, keepdims=True) + NORM_EPS) * dnw
        parts.append((on * _silu(z_ref[:, hs].astype(F32))).astype(BF16))
    return jnp.concatenate(parts, axis=1)


def _mixer_out_ffn_prompt_kernel(o_ref, z_ref, sga_ref, mb_ref, x_ref, unperm_ref, dnw_ref, wod_ref, wout_ref,
                                 nfw_ref, wup_ref, fcw_ref, fcb_ref, wdown_ref, finw_ref,
                                 out_ref, nffn_ref, up_buf, act_buf, *, tm, final):
    t = pl.program_id(1)

    @pl.when(t == 0)
    def _():
        up_buf[0:FFN_HIST, :] = jnp.zeros((FFN_HIST, 2 * D_FF), F32)

    on = _delta_out_norm(o_ref, z_ref, dnw_ref[...])
    ya = jnp.dot(on, wod_ref[...], preferred_element_type=F32)
    merged = (sga_ref[...].astype(F32) * ya + mb_ref[...].astype(F32)).astype(BF16)
    merged = jnp.concatenate([jnp.dot(unperm_ref[...], merged[r0:r0 + CHUNK], preferred_element_type=F32)
                              for r0 in range(0, tm, CHUNK)], axis=0).astype(BF16)
    x1 = x_ref[...] + jnp.dot(merged, wout_ref[...], preferred_element_type=F32)

    h2 = _rmsnorm(x1, nfw_ref[...]).astype(BF16)
    up_buf[FFN_HIST:FFN_HIST + tm, :] = jnp.dot(h2, wup_ref[...], preferred_element_type=F32)
    nffn_ref[...] = up_buf[FFN_HIST + tm - (FFN_CONV - 1):FFN_HIST + tm, :]
    base = FFN_HIST - (FFN_CONV - 1)
    for c in range(D_FF // LANES):
        acts = []
        for off in (0, D_FF):
            cs = slice(off + c * LANES, off + (c + 1) * LANES)
            acc = up_buf[base:base + tm, cs] * fcw_ref[0:1, cs]
            for j in range(1, FFN_CONV):
                acc = acc + up_buf[base + j:base + j + tm, cs] * fcw_ref[j:j + 1, cs]
            acts.append(acc + fcb_ref[:, cs])
        act_buf[:, c * LANES:(c + 1) * LANES] = (_silu(acts[0]) * acts[1]).astype(BF16)
    up_buf[0:FFN_HIST, :] = up_buf[tm:tm + FFN_HIST, :]
    x2 = x1 + jnp.dot(act_buf[...], wdown_ref[...], preferred_element_type=F32)
    out_ref[...] = _rmsnorm(x2, finw_ref[...]) if final else x2


def _mixer_in_sample_kernel(x_ref, sqkv_ref, sconf_ref, nw_ref, wqkv_ref, wz_ref, wba_ref, wglu_ref, wga_ref, wgb_ref,
                            cw_ref, alog_ref, dtb_ref, ccw_ref, ccb_ref, lnw_ref, lnb_ref, woc_ref,
                            q_ref, k_ref, v_ref, z_ref, sga_ref, mb_ref, ba_ref, nqkv_ref, nconf_ref):
    h = _rmsnorm(x_ref[...], nw_ref[...]).astype(BF16)
    hist = SHORT_CONV - 1
    for c in range(QKV_DIM // LANES):
        cs = slice(c * LANES, (c + 1) * LANES)
        new = jnp.dot(h, wqkv_ref[:, cs], preferred_element_type=F32)
        acc = new * cw_ref[hist:hist + 1, cs]
        for j in range(hist):
            row = sqkv_ref[:, j * QKV_DIM + c * LANES:j * QKV_DIM + (c + 1) * LANES]
            acc = acc + row * cw_ref[j:j + 1, cs]
            if j > 0:
                nqkv_ref[:, (j - 1) * QKV_DIM + c * LANES:(j - 1) * QKV_DIM + (c + 1) * LANES] = row
        nqkv_ref[:, (hist - 1) * QKV_DIM + c * LANES:(hist - 1) * QKV_DIM + (c + 1) * LANES] = new
        y = _qkv_post(acc, c)
        if c < N_HEADS:
            q_ref[:, cs] = y
        elif c < 2 * N_HEADS:
            k_ref[:, (c - N_HEADS) * LANES:(c - N_HEADS + 1) * LANES] = y
        else:
            v_ref[:, (c - 2 * N_HEADS) * LANES:(c - 2 * N_HEADS + 1) * LANES] = y

    z_ref[...] = jnp.dot(h, wz_ref[...], preferred_element_type=F32)
    ba = jnp.dot(h, wba_ref[...], preferred_element_type=F32)
    beta, g = _beta_decay(ba, alog_ref[...], dtb_ref[...])
    lane = lax.broadcasted_iota(jnp.int32, ba.shape, 1)
    ba_ref[...] = jnp.where(lane < N_HEADS, beta, jnp.exp(g))

    gl = jnp.dot(h, wglu_ref[...], preferred_element_type=F32)
    u = gl[:, :CONF_DIM] * jax.nn.sigmoid(gl[:, CONF_DIM:])
    chist = CONF_CONV - 1
    acc = u * ccw_ref[chist:chist + 1, :]
    for j in range(chist):
        row = sconf_ref[:, j * CONF_DIM:(j + 1) * CONF_DIM]
        acc = acc + row * ccw_ref[j:j + 1, :]
        if j > 0:
            nconf_ref[:, (j - 1) * CONF_DIM:j * CONF_DIM] = row
    nconf_ref[:, (chist - 1) * CONF_DIM:chist * CONF_DIM] = u
    cact = _silu(_layernorm(acc + ccb_ref[...], lnw_ref[...], lnb_ref[...]))
    yb = _mm(cact, woc_ref[...])
    sga_ref[...] = jax.nn.sigmoid(jnp.dot(h, wga_ref[...], preferred_element_type=F32))
    mb_ref[...] = jax.nn.sigmoid(jnp.dot(h, wgb_ref[...], preferred_element_type=F32)) * yb


def _delta_step_kernel(beta_ref, a_ref, qt_ref, kt_ref, v_ref, s_ref, *rest, bt):
    o_ref, snew_ref = rest[-2:]
    b0 = pl.program_id(0) * bt
    for i in range(bt):
        kt = kt_ref[i]
        qt = qt_ref[i]
        for h in range(N_HEADS):
            beta = beta_ref[(b0 + i) * N_HEADS + h]
            a = a_ref[(b0 + i) * N_HEADS + h]
            s = s_ref[i, h]
            kcol = kt[:, h:h + 1]
            r = jnp.sum(s * kcol, axis=0, keepdims=True)
            d = beta * (v_ref[i, h:h + 1, :] - a * r)
            sn = a * s + kcol * d
            snew_ref[i, h] = sn
            o_ref[i, h:h + 1, :] = jnp.sum(sn * qt[:, h:h + 1], axis=0, keepdims=True)


def _mixer_out_ffn_sample_kernel(o_ref, z_ref, sga_ref, mb_ref, x_ref, sffn_ref, dnw_ref, wod_ref, wout_ref,
                                 nfw_ref, wup_ref, fcw_ref, fcb_ref, wdown_ref, finw_ref,
                                 out_ref, nffn_ref, act_buf, *, final):
    on = _delta_out_norm(o_ref, z_ref, dnw_ref[...])
    ya = jnp.dot(on, wod_ref[...], preferred_element_type=F32)
    merged = sga_ref[...] * ya + mb_ref[...]
    x1 = x_ref[...] + _mm(merged, wout_ref[...])
    h2 = _rmsnorm(x1, nfw_ref[...]).astype(BF16)
    hist = FFN_CONV - 1
    for c in range(D_FF // LANES):
        acts = []
        for off in (0, D_FF):
            cs = slice(off + c * LANES, off + (c + 1) * LANES)
            new = jnp.dot(h2, wup_ref[:, cs], preferred_element_type=F32)
            acc = new * fcw_ref[hist:hist + 1, cs]
            for j in range(hist):
                row = sffn_ref[:, j * 2 * D_FF + off + c * LANES:j * 2 * D_FF + off + (c + 1) * LANES]
                acc = acc + row * fcw_ref[j:j + 1, cs]
                if j > 0:
                    nffn_ref[:, (j - 1) * 2 * D_FF + off + c * LANES:(j - 1) * 2 * D_FF + off + (c + 1) * LANES] = row
            nffn_ref[:, (hist - 1) * 2 * D_FF + off + c * LANES:(hist - 1) * 2 * D_FF + off + (c + 1) * LANES] = new
            acts.append(acc + fcb_ref[:, cs])
        act_buf[:, c * LANES:(c + 1) * LANES] = (_silu(acts[0]) * acts[1]).astype(BF16)
    x2 = x1 + jnp.dot(act_buf[...], wdown_ref[...], preferred_element_type=F32)
    out_ref[...] = _rmsnorm(x2, finw_ref[...]) if final else x2


def _params(n_grid):
    return pltpu.CompilerParams(dimension_semantics=("arbitrary",) * n_grid, vmem_limit_bytes=VMEM_LIMIT)


def _layer_block(arr, l, width=None, blk=0):
    rows, cols = arr.shape[1:]
    return pl.BlockSpec((None, rows, cols if width is None else width), lambda *_: (l, 0, blk),
                        pipeline_mode=pl.Buffered(1))


def _whole(arr):
    nd = len(arr.shape)
    return pl.BlockSpec(arr.shape, lambda *_: (0,) * nd, pipeline_mode=pl.Buffered(1))


def _mixer_in_consts(pw, l, head_major):
    names = ["norm_mix_w", ("w_in", QKV_DIM, 0), ("w_in", D_MODEL, QKV_DIM // D_MODEL), "w_ba"]
    names += ["w_at"] if head_major else []
    names += [("w_tail", D_MODEL, 0), ("w_tail", D_MODEL, 1), ("w_tail", D_MODEL, 2),
              "conv_qkv_w", "alog_row", "dtb_row"]
    names += ["alog_col", "dtb_col"] if head_major else []
    names += ["conf_conv_w", "conf_conv_b", "conf_ln_w", "conf_ln_b", "w_o_conf"]
    out = []
    for n in names:
        if isinstance(n, tuple):
            out.append((pw[n[0]], _layer_block(pw[n[0]], l, n[1], n[2])))
        else:
            out.append((pw[n], _layer_block(pw[n], l)))
    return out


def _mixer_out_consts(pw, l):
    names = ("delta_norm_w", "w_o_delta", "w_out", "norm_ffn_w", "w_up", "ffn_conv_w", "ffn_conv_b", "w_down")
    return [(pw[n], _layer_block(pw[n], l)) for n in names] + [(pw["norm_final_w"], _layer_block(pw["norm_final_w"], 0))]


def _mixer_in_prompt(x, pw, l):
    nb, seq, _ = x.shape
    tm = TILE_M_IN
    tok = lambda n: pl.BlockSpec((None, tm, n), lambda b, t: (b, t, 0))
    consts = _mixer_in_consts(pw, l, head_major=True)
    mats = [_perm_matrix(CHUNK, inverse=False), _chunk_tri(CHUNK, upper=False), _chunk_tri(CHUNK, upper=True)]
    act = jax.ShapeDtypeStruct((nb, seq, D_MODEL), BF16)
    return pl.pallas_call(
        functools.partial(_mixer_in_prompt_kernel, tm=tm),
        grid=(nb, seq // tm),
        in_specs=[tok(D_MODEL)] + [_whole(m) for m in mats] + [c[1] for c in consts],
        out_specs=[tok(D_MODEL)] * 6 + [
            tok(LANES),
            pl.BlockSpec((None, N_HEADS, tm), lambda b, t: (b, 0, t)),
            pl.BlockSpec((None, SHORT_CONV - 1, QKV_DIM), lambda b, t: (b, 0, 0)),
            pl.BlockSpec((None, CONF_CONV - 1, CONF_DIM), lambda b, t: (b, 0, 0)),
        ],
        out_shape=[act] * 6 + [
            jax.ShapeDtypeStruct((nb, seq, LANES), F32),
            jax.ShapeDtypeStruct((nb, N_HEADS, seq), F32),
            jax.ShapeDtypeStruct((nb, SHORT_CONV - 1, QKV_DIM), F32),
            jax.ShapeDtypeStruct((nb, CONF_CONV - 1, CONF_DIM), F32),
        ],
        scratch_shapes=[pltpu.VMEM((tm, CONF_DIM), F32),
                        pltpu.VMEM((tm, D_MODEL), F32),
                        pltpu.VMEM((QKV_TAIL, QKV_DIM), F32),
                        pltpu.VMEM((CHUNK, CONF_DIM), F32)],
        compiler_params=_params(2),
        name="mixer_in_prompt",
    )(x, *mats, *[c[0] for c in consts])


def _delta_prompt(q, k, v, bgc, gct):
    nb, seq, _ = q.shape
    tm = DELTA_CHUNKS * CHUNK
    tok = lambda n: pl.BlockSpec((None, tm, n), lambda b, t: (b, t, 0))
    masks = _delta_masks()
    return pl.pallas_call(
        functools.partial(_delta_chunk_kernel, nc=DELTA_CHUNKS),
        grid=(nb, seq // tm),
        in_specs=[tok(D_MODEL)] * 3 + [tok(LANES), pl.BlockSpec((None, N_HEADS, tm), lambda b, t: (b, 0, t)),
                  _whole(masks)],
        out_specs=[tok(D_MODEL),
                   pl.BlockSpec((None, N_HEADS, HEAD_DIM, HEAD_DIM), lambda b, t: (b, 0, 0, 0))],
        out_shape=[jax.ShapeDtypeStruct((nb, seq, D_MODEL), BF16),
                   jax.ShapeDtypeStruct((nb, N_HEADS, HEAD_DIM, HEAD_DIM), F32)],
        compiler_params=_params(2),
        name="delta_prompt",
    )(q, k, v, bgc, gct, masks)


def _mixer_out_ffn_prompt(o, z, sga, mb, x, pw, l, final):
    nb, seq, _ = x.shape
    tm = TILE_M
    tok = pl.BlockSpec((None, tm, D_MODEL), lambda b, t: (b, t, 0))
    consts = _mixer_out_consts(pw, l)
    unperm = _perm_matrix(CHUNK, inverse=True)
    return pl.pallas_call(
        functools.partial(_mixer_out_ffn_prompt_kernel, tm=tm, final=final),
        grid=(nb, seq // tm),
        in_specs=[tok] * 5 + [_whole(unperm)] + [c[1] for c in consts],
        out_specs=[tok, pl.BlockSpec((None, FFN_CONV - 1, 2 * D_FF), lambda b, t: (b, 0, 0))],
        out_shape=[jax.ShapeDtypeStruct((nb, seq, D_MODEL), F32),
                   jax.ShapeDtypeStruct((nb, FFN_CONV - 1, 2 * D_FF), F32)],
        scratch_shapes=[pltpu.VMEM((FFN_HIST + tm, 2 * D_FF), F32), pltpu.VMEM((tm, D_FF), BF16)],
        compiler_params=_params(2),
        name="mixer_out_ffn_prompt",
    )(o, z, sga, mb, x, unperm, *[c[0] for c in consts])


def _mixer_in_sample(x, sqkv, sconf, pw, l):
    nb = x.shape[0]
    consts = _mixer_in_consts(pw, l, head_major=False)
    act = jax.ShapeDtypeStruct((nb, D_MODEL), F32)
    outs = [act] * 6 + [jax.ShapeDtypeStruct((nb, LANES), F32),
                        jax.ShapeDtypeStruct(sqkv.shape, F32), jax.ShapeDtypeStruct(sconf.shape, F32)]
    return pl.pallas_call(
        _mixer_in_sample_kernel,
        grid=(1,),
        in_specs=[_whole(a) for a in (x, sqkv, sconf)] + [c[1] for c in consts],
        out_specs=[_whole(a) for a in outs],
        out_shape=outs,
        compiler_params=_params(1),
        name="mixer_in_sample",
    )(x, sqkv, sconf, *[c[0] for c in consts])


def _delta_sample(beta, a, qt, kt, v, states, l, new_states):
    nb = states.shape[1]
    bt = 8
    smem = pl.BlockSpec(memory_space=pltpu.SMEM)
    state_blk = pl.BlockSpec((None, bt, N_HEADS, HEAD_DIM, HEAD_DIM), lambda i: (l, i, 0, 0, 0))
    args = [beta, a, qt, kt, v, states] + ([] if new_states is None else [new_states])
    return pl.pallas_call(
        functools.partial(_delta_step_kernel, bt=bt),
        grid=(nb // bt,),
        in_specs=[smem, smem,
                  pl.BlockSpec((bt, HEAD_DIM, N_HEADS), lambda i: (i, 0, 0)),
                  pl.BlockSpec((bt, HEAD_DIM, N_HEADS), lambda i: (i, 0, 0)),
                  pl.BlockSpec((bt, N_HEADS, HEAD_DIM), lambda i: (i, 0, 0)),
                  state_blk] + ([] if new_states is None else [pl.BlockSpec(memory_space=pl.ANY)]),
        out_specs=[pl.BlockSpec((bt, N_HEADS, HEAD_DIM), lambda i: (i, 0, 0)), state_blk],
        out_shape=[jax.ShapeDtypeStruct((nb, N_HEADS, HEAD_DIM), F32),
                   jax.ShapeDtypeStruct(states.shape, F32)],
        input_output_aliases={} if new_states is None else {len(args) - 1: 1},
        compiler_params=_params(1),
        name="delta_sample",
    )(*args)


def _mixer_out_ffn_sample(o, z, sga, mb, x, sffn, pw, l, final):
    nb = x.shape[0]
    consts = _mixer_out_consts(pw, l)
    outs = [jax.ShapeDtypeStruct((nb, D_MODEL), F32), jax.ShapeDtypeStruct(sffn.shape, F32)]
    return pl.pallas_call(
        functools.partial(_mixer_out_ffn_sample_kernel, final=final),
        grid=(1,),
        in_specs=[_whole(a) for a in (o, z, sga, mb, x, sffn)] + [c[1] for c in consts],
        out_specs=[_whole(a) for a in outs],
        out_shape=outs,
        scratch_shapes=[pltpu.VMEM((nb, D_FF), BF16)],
        compiler_params=_params(1),
        name="mixer_out_ffn_sample",
    )(o, z, sga, mb, x, sffn, *[c[0] for c in consts])


def _prepare_weights(norm_mix_w, w_in, conv_qkv_w, a_log, dt_bias, delta_norm_w, w_o_delta,
                     conf_conv_w, conf_conv_b, conf_ln_w, conf_ln_b, w_o_conf, w_out,
                     norm_ffn_w, w_up, ffn_conv_w, ffn_conv_b, w_down, norm_final_w):
    depth = w_in.shape[0]
    o_b = QKV_DIM + N_HEADS * HEAD_DIM
    o_a = o_b + N_HEADS
    o_glu = o_a + N_HEADS
    pad = LANES - 2 * N_HEADS
    rows = lambda p: p.reshape(depth, 1, -1)
    lane_pad = lambda p: jnp.pad(p.reshape(depth, 1, N_HEADS), ((0, 0), (0, 0), (N_HEADS, pad)))
    return {
        "norm_mix_w": rows(norm_mix_w),
        "w_in": w_in[:, :, :o_b].astype(BF16),
        "w_tail": w_in[:, :, o_glu:].astype(BF16),
        "w_ba": jnp.pad(w_in[:, :, o_b:o_glu], ((0, 0), (0, 0), (0, pad))).astype(BF16),
        "w_at": jnp.swapaxes(w_in[:, :, o_a:o_glu], 1, 2).astype(BF16),
        "conv_qkv_w": conv_qkv_w,
        "alog_row": lane_pad(a_log), "dtb_row": lane_pad(dt_bias),
        "alog_col": a_log.reshape(depth, N_HEADS, 1), "dtb_col": dt_bias.reshape(depth, N_HEADS, 1),
        "conf_conv_w": conf_conv_w, "conf_conv_b": rows(conf_conv_b),
        "conf_ln_w": rows(conf_ln_w), "conf_ln_b": rows(conf_ln_b),
        "w_o_conf": w_o_conf.astype(BF16),
        "delta_norm_w": rows(delta_norm_w),
        "w_o_delta": w_o_delta.astype(BF16),
        "w_out": w_out.astype(BF16),
        "norm_ffn_w": rows(norm_ffn_w),
        "w_up": w_up.astype(BF16),
        "ffn_conv_w": ffn_conv_w, "ffn_conv_b": rows(ffn_conv_b),
        "w_down": w_down.astype(BF16),
        "norm_final_w": norm_final_w.reshape(1, 1, -1),
    }


def kernel(x_prompt, x_sample, state_delta, state_qkv_conv, state_conf_conv, state_ffn_conv, norm_mix_w, w_in, conv_qkv_w, a_log, dt_bias, delta_norm_w, w_o_delta, conf_conv_w, conf_conv_b, conf_ln_w, conf_ln_b, w_o_conf, w_out, norm_ffn_w, w_up, ffn_conv_w, ffn_conv_b, w_down, norm_final_w):
    depth = w_in.shape[0]
    nb_s = x_sample.shape[0]
    pw = _prepare_weights(norm_mix_w, w_in, conv_qkv_w, a_log, dt_bias, delta_norm_w, w_o_delta,
                          conf_conv_w, conf_conv_b, conf_ln_w, conf_ln_b, w_o_conf, w_out,
                          norm_ffn_w, w_up, ffn_conv_w, ffn_conv_b, w_down, norm_final_w)
    xp = x_prompt
    xs = x_sample.reshape(nb_s, D_MODEL)
    new_p = ([], [], [], [])
    new_s = ([], [], [])
    delta_s = None
    for l in range(depth):
        final = l == depth - 1
        q, k, v, z, sga, mb, bgc, gct, n_qkv, n_conf = _mixer_in_prompt(xp, pw, l)
        o, n_delta = _delta_prompt(q, k, v, bgc, gct)
        xp, n_ffn = _mixer_out_ffn_prompt(o, z, sga, mb, xp, pw, l, final)
        for lst, val in zip(new_p, (n_delta, n_qkv, n_conf, n_ffn)):
            lst.append(val)
        q, k, v, z, sga, mb, ba, n_qkv, n_conf = _mixer_in_sample(
            xs, state_qkv_conv[l].reshape(nb_s, -1), state_conf_conv[l].reshape(nb_s, -1), pw, l)
        heads = lambda a: a.reshape(nb_s, N_HEADS, HEAD_DIM)
        o, delta_s = _delta_sample(ba[:, :N_HEADS].reshape(-1), ba[:, N_HEADS:2 * N_HEADS].reshape(-1),
                                   heads(q).transpose(0, 2, 1), heads(k).transpose(0, 2, 1), heads(v),
                                   state_delta, l, delta_s)
        xs, n_ffn = _mixer_out_ffn_sample(o.reshape(nb_s, D_MODEL), z, sga, mb, xs,
                                          state_ffn_conv[l].reshape(nb_s, -1), pw, l, final)
        for lst, val in zip(new_s, (n_qkv.reshape(state_qkv_conv.shape[1:]),
                                    n_conf.reshape(state_conf_conv.shape[1:]),
                                    n_ffn.reshape(state_ffn_conv.shape[1:]))):
            lst.append(val)
    return (xp, xs.reshape(x_sample.shape),
            jnp.stack(new_p[0]), jnp.stack(new_p[1]), jnp.stack(new_p[2]), jnp.stack(new_p[3]),
            delta_s, jnp.stack(new_s[0]), jnp.stack(new_s[1]), jnp.stack(new_s[2]))
```

```python
import functools

import jax
import jax.numpy as jnp
from jax import lax
from jax.experimental import pallas as pl
from jax.experimental.pallas import tpu as pltpu

F32 = jnp.float32
BF16 = jnp.bfloat16

D_MODEL = 1024
N_HEADS = 8
HEAD_DIM = 128
QKV_DIM = 3 * N_HEADS * HEAD_DIM
SHORT_CONV = 4
CONF_DIM = D_MODEL // 2
CONF_CONV = 31
D_FF = 2816
FFN_CONV = 3
NORM_EPS = 1e-6
Q_SCALE = HEAD_DIM ** -0.5

LANES = 128
SUBLANES = 8
MXU_COLS = 256
CHUNK = 128
SLABS = CHUNK // SUBLANES
DELTA_CHUNKS = 2
TILE_M_IN = 512
TILE_M = 256
QKV_TAIL = SUBLANES * (SHORT_CONV - 1)
FFN_HIST = 8
VMEM_LIMIT = 60000 * 1024


def _mm(a, b):
    return jnp.dot(a.astype(BF16), b.astype(BF16), preferred_element_type=F32)


def _mm_nt(a, b):
    return lax.dot_general(a.astype(BF16), b.astype(BF16), (((1,), (1,)), ((), ())),
                           preferred_element_type=F32)


def _mm_tn(a, b):
    return lax.dot_general(a.astype(BF16), b.astype(BF16), (((0,), (0,)), ((), ())),
                           preferred_element_type=F32)


def _split3(x):
    hi = x.astype(BF16)
    r1 = x - hi.astype(F32)
    mid = r1.astype(BF16)
    lo = (r1 - mid.astype(F32)).astype(BF16)
    return hi, mid, lo


def _silu(x):
    return x * jax.nn.sigmoid(x)


def _softplus(x):
    return jnp.maximum(x, 0.0) + jnp.log1p(jnp.exp(-jnp.abs(x)))


def _rmsnorm(x, w):
    return x * lax.rsqrt(jnp.mean(x * x, axis=-1, keepdims=True) + NORM_EPS) * w


def _layernorm(x, w, b):
    mu = jnp.mean(x, axis=-1, keepdims=True)
    xc = x - mu
    return xc * lax.rsqrt(jnp.mean(xc * xc, axis=-1, keepdims=True) + NORM_EPS) * w + b


def _token_of_row(r):
    w = r % CHUNK
    return (r // CHUNK) * CHUNK + (w % SUBLANES) * SLABS + w // SUBLANES


def _row_of_token(tok):
    return (tok % SLABS) * SUBLANES + tok // SLABS


def _perm_matrix(n, inverse):
    i = lax.broadcasted_iota(jnp.int32, (n, n), 0)
    j = lax.broadcasted_iota(jnp.int32, (n, n), 1)
    hit = (i == _token_of_row(j)) if inverse else (j == _token_of_row(i))
    return jnp.where(hit, 1.0, 0.0).astype(BF16)


def _chunk_tri(n, upper):
    r = _token_of_row(lax.broadcasted_iota(jnp.int32, (n, n), 0))
    c = _token_of_row(lax.broadcasted_iota(jnp.int32, (n, n), 1))
    same = (r // CHUNK) == (c // CHUNK)
    tri = (r <= c) if upper else (r >= c)
    return jnp.where(same & tri, 1.0, 0.0).astype(BF16)


def _sublane_shifted(cur, prev, e):
    n = cur.shape[0]
    sub = lax.broadcasted_iota(jnp.int32, cur.shape, 0) % SUBLANES
    return jnp.where(sub >= e, pltpu.roll(cur, e, 0), pltpu.roll(prev, n - SUBLANES + e, 0))


def _delayed(zs, d):
    a, b = divmod(d, SLABS)
    if b == 0:
        return zs[a]
    cut = CHUNK - SUBLANES * b
    return jnp.concatenate([zs[a + 1][cut:], zs[a][:cut]], axis=0)


def _qkv_post(y, c):
    y = _silu(y)
    if c < 2 * N_HEADS:
        y = y * lax.rsqrt(jnp.sum(y * y, axis=-1, keepdims=True) + NORM_EPS)
        if c < N_HEADS:
            y = y * Q_SCALE
    return y


def _beta_decay(ba, alog, dtb):
    lane = lax.broadcasted_iota(jnp.int32, ba.shape, 1)
    beta = jnp.where(lane < N_HEADS, jax.nn.sigmoid(ba), 0.0)
    g = -jnp.exp(alog) * _softplus(ba + dtb)
    g = jnp.where((lane >= N_HEADS) & (lane < 2 * N_HEADS), g, 0.0)
    return beta, g


def _mixer_in_prompt_kernel(x_ref, perm_ref, tril_ref, triu_ref, nw_ref, wqkv_ref, wz_ref, wba_ref, wat_ref, wglu_ref, wga_ref, wgb_ref,
                            cw_ref, alog_ref, dtb_ref, alogc_ref, dtbc_ref,
                            ccw_ref, ccb_ref, lnw_ref, lnb_ref, woc_ref,
                            q_ref, k_ref, v_ref, z_ref, sga_ref, mb_ref, bgc_ref, gct_ref,
                            nqkv_ref, nconf_ref,
                            cc_buf, sgb_buf, qkv_carry, u_carry, *, tm):
    t = pl.program_id(1)

    @pl.when(t == 0)
    def _():
        qkv_carry[...] = jnp.zeros(qkv_carry.shape, F32)
        u_carry[...] = jnp.zeros(u_carry.shape, F32)

    xh = _rmsnorm(x_ref[...], nw_ref[...]).astype(BF16)
    chunks = [slice(r0, r0 + CHUNK) for r0 in range(0, tm, CHUNK)]
    h = jnp.concatenate([jnp.dot(perm_ref[...], xh[ch], preferred_element_type=F32) for ch in chunks],
                        axis=0).astype(BF16)
    last = tm - CHUNK
    slabs_per_group = MXU_COLS // LANES

    def state_rows(n_hist):
        return [last + _row_of_token(CHUNK - n_hist + i) for i in range(n_hist)]

    for g in range(CONF_DIM // MXU_COLS):
        gs = slice(g * MXU_COLS, (g + 1) * MXU_COLS)
        u = (jnp.dot(h, wglu_ref[:, gs], preferred_element_type=F32)
             * jax.nn.sigmoid(jnp.dot(h, wglu_ref[:, CONF_DIM + g * MXU_COLS:CONF_DIM + (g + 1) * MXU_COLS],
                                      preferred_element_type=F32)))
        for i, r in enumerate(state_rows(CONF_CONV - 1)):
            nconf_ref[i:i + 1, gs] = u[r:r + 1, :]
        for ci in range(tm // CHUNK):
            r0 = ci * CHUNK
            for j in range(slabs_per_group):
                cs = slice(g * MXU_COLS + j * LANES, g * MXU_COLS + (j + 1) * LANES)
                js = slice(j * LANES, (j + 1) * LANES)
                cur = u[r0:r0 + CHUNK, js]
                prev = u_carry[:, cs] if ci == 0 else u[r0 - CHUNK:r0, js]
                zs = (cur, _sublane_shifted(cur, prev, 1), _sublane_shifted(cur, prev, 2))
                acc = cur * ccw_ref[CONF_CONV - 1:CONF_CONV, cs]
                for d in range(1, CONF_CONV):
                    acc = acc + _delayed(zs, d) * ccw_ref[CONF_CONV - 1 - d:CONF_CONV - d, cs]
                cc_buf[r0:r0 + CHUNK, cs] = acc + ccb_ref[:, cs]
        u_carry[:, gs] = u[last:, :]
    cact = _silu(_layernorm(cc_buf[...], lnw_ref[...], lnb_ref[...])).astype(BF16)

    for g in range(QKV_DIM // MXU_COLS):
        gs = slice(g * MXU_COLS, (g + 1) * MXU_COLS)
        proj = jnp.dot(h, wqkv_ref[:, gs], preferred_element_type=F32)
        for i, r in enumerate(state_rows(SHORT_CONV - 1)):
            nqkv_ref[i:i + 1, gs] = proj[r:r + 1, :]
        for ci in range(tm // CHUNK):
            r0 = ci * CHUNK
            for j in range(slabs_per_group):
                c = g * slabs_per_group + j
                cs = slice(c * LANES, (c + 1) * LANES)
                js = slice(j * LANES, (j + 1) * LANES)
                cur = proj[r0:r0 + CHUNK, js]
                prev_tail = qkv_carry[:, cs] if ci == 0 else proj[r0 - QKV_TAIL:r0, js]
                z1 = jnp.concatenate([cur[:CHUNK - QKV_TAIL],
                                      _sublane_shifted(cur[CHUNK - QKV_TAIL:], prev_tail, 1)], axis=0)
                acc = cur * cw_ref[SHORT_CONV - 1:SHORT_CONV, cs]
                for d in range(1, SHORT_CONV):
                    acc = acc + _delayed((cur, z1), d) * cw_ref[SHORT_CONV - 1 - d:SHORT_CONV - d, cs]
                y = _qkv_post(acc, c)
                y = y.astype(BF16)
                if c < N_HEADS:
                    q_ref[r0:r0 + CHUNK, cs] = y
                elif c < 2 * N_HEADS:
                    k_ref[r0:r0 + CHUNK, (c - N_HEADS) * LANES:(c - N_HEADS + 1) * LANES] = y
                else:
                    v_ref[r0:r0 + CHUNK, (c - 2 * N_HEADS) * LANES:(c - 2 * N_HEADS + 1) * LANES] = y
        qkv_carry[:, gs] = proj[tm - QKV_TAIL:, :]

    for g in range(D_MODEL // MXU_COLS):
        gs = slice(g * MXU_COLS, (g + 1) * MXU_COLS)
        z_ref[:, gs] = jnp.dot(h, wz_ref[:, gs], preferred_element_type=F32).astype(BF16)

    ba = jnp.dot(h, wba_ref[...], preferred_element_type=F32)
    beta, g_tok = _beta_decay(ba, alog_ref[...], dtb_ref[...])
    at = _mm_nt(wat_ref[...], h)
    g_head = -jnp.exp(alogc_ref[...]) * _softplus(at + dtbc_ref[...])
    for ch in chunks:
        gc = sum(jnp.dot(tril_ref[...], part, preferred_element_type=F32) for part in _split3(g_tok[ch]))
        bgc_ref[ch, :] = beta[ch] + gc
        gct_ref[:, ch] = sum(jnp.dot(part, triu_ref[...], preferred_element_type=F32) for part in _split3(g_head[:, ch]))

    for g in range(D_MODEL // MXU_COLS):
        gs = slice(g * MXU_COLS, (g + 1) * MXU_COLS)
        sga_ref[:, gs] = jax.nn.sigmoid(jnp.dot(h, wga_ref[:, gs], preferred_element_type=F32)).astype(BF16)
        sgb_buf[:, gs] = jax.nn.sigmoid(jnp.dot(h, wgb_ref[:, gs], preferred_element_type=F32))

    mb_ref[...] = (sgb_buf[...] * jnp.dot(cact, woc_ref[...], preferred_element_type=F32)).astype(BF16)


def _delta_masks():
    row = lax.broadcasted_iota(jnp.int32, (CHUNK, CHUNK), 0)
    col = lax.broadcasted_iota(jnp.int32, (CHUNK, CHUNK), 1)
    trow = _token_of_row(row)
    tcol = _token_of_row(col)
    masks = [trow >= tcol, trow > tcol, row == col]
    s_blk = 1
    while s_blk < CHUNK:
        masks.append(((trow // (2 * s_blk)) == (tcol // (2 * s_blk)))
                     & ((trow // s_blk) % 2 == 1) & ((tcol // s_blk) % 2 == 0))
        s_blk *= 2
    return jnp.stack(masks).astype(F32)


def _delta_chunk_kernel(q_ref, k_ref, v_ref, bgc_ref, gct_ref, mask_ref, o_ref, s_ref, *, nc):
    t = pl.program_id(1)

    @pl.when(t == 0)
    def _():
        s_ref[...] = jnp.zeros(s_ref.shape, F32)

    causal, strict, eye = 0, 1, 2
    sub_diag = {1 << i: 3 + i for i in range(CHUNK.bit_length() - 1)}
    heads = range(N_HEADS)
    units = [(ci, h) for ci in range(nc) for h in heads]

    def rows(ci):
        return slice(ci * CHUNK, (ci + 1) * CHUNK)

    def cols(h):
        return slice(h * HEAD_DIM, (h + 1) * HEAD_DIM)

    r_last = _row_of_token(CHUNK - 1)
    beta = {(ci, h): bgc_ref[rows(ci), h:h + 1] for ci, h in units}
    gc = {(ci, h): bgc_ref[rows(ci), N_HEADS + h:N_HEADS + h + 1] for ci, h in units}
    glast = {(ci, h): gct_ref[h:h + 1, ci * CHUNK + r_last:ci * CHUNK + r_last + 1]
             for ci, h in units}

    low, qk, rhs = {}, {}, {}
    for u in units:
        ci, h = u
        k = k_ref[rows(ci), cols(h)].astype(F32)
        kb = k * beta[u]
        decay = jnp.exp((gc[u] - gct_ref[h:h + 1, rows(ci)]) * mask_ref[causal]) * mask_ref[causal]
        kq = _mm_nt(jnp.concatenate([kb, q_ref[rows(ci), cols(h)].astype(F32)], axis=0), k)
        low[u] = kq[:CHUNK] * decay * mask_ref[strict]
        qk[u] = (kq[CHUNK:] * decay).astype(BF16)
        rhs[u] = jnp.concatenate([v_ref[rows(ci), cols(h)].astype(F32) * beta[u], kb * jnp.exp(gc[u])],
                                 axis=1).astype(BF16)

    tinv = {u: mask_ref[eye] - low[u] * mask_ref[sub_diag[1]] for u in units}
    s_blk = 2
    while s_blk < CHUNK:
        t16 = {u: tinv[u].astype(BF16) for u in units}
        m1 = {u: jnp.dot((low[u] * mask_ref[sub_diag[s_blk]]).astype(BF16), t16[u],
                         preferred_element_type=F32) for u in units}
        tinv = {u: tinv[u] - jnp.dot(t16[u], m1[u].astype(BF16), preferred_element_type=F32) for u in units}
        s_blk *= 2
    y = {u: jnp.dot(tinv[u].astype(BF16), rhs[u], preferred_element_type=F32) for u in units}

    state = [s_ref[h] for h in heads]
    for ci in range(nc):
        ws_qs = [_mm(jnp.concatenate([y[ci, h][:, HEAD_DIM:],
                                      q_ref[rows(ci), cols(h)].astype(F32) * jnp.exp(gc[ci, h])], axis=0), state[h])
                 for h in heads]
        v_new = [(y[ci, h][:, :HEAD_DIM] - ws_qs[h][:CHUNK]).astype(BF16) for h in heads]
        for h in heads:
            o_ref[rows(ci), cols(h)] = (ws_qs[h][CHUNK:] + jnp.dot(qk[ci, h], v_new[h], preferred_element_type=F32)
                                        ).astype(o_ref.dtype)
        state = [state[h] * jnp.exp(glast[ci, h])
                 + _mm_tn(k_ref[rows(ci), cols(h)].astype(F32) * jnp.exp(glast[ci, h] - gc[ci, h]), v_new[h])
                 for h in heads]
    for h in heads:
        s_ref[h] = state[h]


def _delta_out_norm(o_ref, z_ref, dnw):
    parts = []
    for h in range(N_HEADS):
        hs = slice(h * HEAD_DIM, (h + 1) * HEAD_DIM)
        o = o_ref[:, hs].astype(F32)
        on = o * lax.rsqrt(jnp.mean(o * o, axis=-1, keepdims=True) + NORM_EPS) * dnw
        parts.append((on * _silu(z_ref[:, hs].astype(F32))).astype(BF16))
    return jnp.concatenate(parts, axis=1)


def _mixer_out_ffn_prompt_kernel(o_ref, z_ref, sga_ref, mb_ref, x_ref, unperm_ref, dnw_ref, wod_ref, wout_ref,
                                 nfw_ref, wup_ref, fcw_ref, fcb_ref, wdown_ref, finw_ref,
                                 out_ref, nffn_ref, up_buf, act_buf, *, tm, final):
    t = pl.program_id(1)

    @pl.when(t == 0)
    def _():
        up_buf[0:FFN_HIST, :] = jnp.zeros((FFN_HIST, 2 * D_FF), F32)

    on = _delta_out_norm(o_ref, z_ref, dnw_ref[...])
    ya = jnp.dot(on, wod_ref[...], preferred_element_type=F32)
    merged = (sga_ref[...].astype(F32) * ya + mb_ref[...].astype(F32)).astype(BF16)
    merged = jnp.concatenate([jnp.dot(unperm_ref[...], merged[r0:r0 + CHUNK], preferred_element_type=F32)
                              for r0 in range(0, tm, CHUNK)], axis=0).astype(BF16)
    x1 = x_ref[...] + jnp.dot(merged, wout_ref[...], preferred_element_type=F32)

    h2 = _rmsnorm(x1, nfw_ref[...]).astype(BF16)
    up_buf[FFN_HIST:FFN_HIST + tm, :] = jnp.dot(h2, wup_ref[...], preferred_element_type=F32)
    nffn_ref[...] = up_buf[FFN_HIST + tm - (FFN_CONV - 1):FFN_HIST + tm, :]
    base = FFN_HIST - (FFN_CONV - 1)
    for c in range(D_FF // LANES):
        acts = []
        for off in (0, D_FF):
            cs = slice(off + c * LANES, off + (c + 1) * LANES)
            acc = up_buf[base:base + tm, cs] * fcw_ref[0:1, cs]
            for j in range(1, FFN_CONV):
                acc = acc + up_buf[base + j:base + j + tm, cs] * fcw_ref[j:j + 1, cs]
            acts.append(acc + fcb_ref[:, cs])
        act_buf[:, c * LANES:(c + 1) * LANES] = (_silu(acts[0]) * acts[1]).astype(BF16)
    up_buf[0:FFN_HIST, :] = up_buf[tm:tm + FFN_HIST, :]
    x2 = x1 + jnp.dot(act_buf[...], wdown_ref[...], preferred_element_type=F32)
    out_ref[...] = _rmsnorm(x2, finw_ref[...]) if final else x2


def _mixer_in_sample_kernel(x_ref, sqkv_ref, sconf_ref, nw_ref, wqkv_ref, wz_ref, wba_ref, wglu_ref, wga_ref, wgb_ref,
                            cw_ref, alog_ref, dtb_ref, ccw_ref, ccb_ref, lnw_ref, lnb_ref, woc_ref,
                            q_ref, k_ref, v_ref, z_ref, sga_ref, mb_ref, ba_ref, nqkv_ref, nconf_ref):
    h = _rmsnorm(x_ref[...], nw_ref[...]).astype(BF16)
    hist = SHORT_CONV - 1
    for c in range(QKV_DIM // LANES):
        cs = slice(c * LANES, (c + 1) * LANES)
        new = jnp.dot(h, wqkv_ref[:, cs], preferred_element_type=F32)
        acc = new * cw_ref[hist:hist + 1, cs]
        for j in range(hist):
            row = sqkv_ref[:, j * QKV_DIM + c * LANES:j * QKV_DIM + (c + 1) * LANES]
            acc = acc + row * cw_ref[j:j + 1, cs]
            if j > 0:
                nqkv_ref[:, (j - 1) * QKV_DIM + c * LANES:(j - 1) * QKV_DIM + (c + 1) * LANES] = row
        nqkv_ref[:, (hist - 1) * QKV_DIM + c * LANES:(hist - 1) * QKV_DIM + (c + 1) * LANES] = new
        y = _qkv_post(acc, c)
        if c < N_HEADS:
            q_ref[:, cs] = y
        elif c < 2 * N_HEADS:
            k_ref[:, (c - N_HEADS) * LANES:(c - N_HEADS + 1) * LANES] = y
        else:
            v_ref[:, (c - 2 * N_HEADS) * LANES:(c - 2 * N_HEADS + 1) * LANES] = y

    z_ref[...] = jnp.dot(h, wz_ref[...], preferred_element_type=F32)
    ba = jnp.dot(h, wba_ref[...], preferred_element_type=F32)
    beta, g = _beta_decay(ba, alog_ref[...], dtb_ref[...])
    lane = lax.broadcasted_iota(jnp.int32, ba.shape, 1)
    ba_ref[...] = jnp.where(lane < N_HEADS, beta, jnp.exp(g))

    gl = jnp.dot(h, wglu_ref[...], preferred_element_type=F32)
    u = gl[:, :CONF_DIM] * jax.nn.sigmoid(gl[:, CONF_DIM:])
    chist = CONF_CONV - 1
    acc = u * ccw_ref[chist:chist + 1, :]
    for j in range(chist):
        row = sconf_ref[:, j * CONF_DIM:(j + 1) * CONF_DIM]
        acc = acc + row * ccw_ref[j:j + 1, :]
        if j > 0:
            nconf_ref[:, (j - 1) * CONF_DIM:j * CONF_DIM] = row
    nconf_ref[:, (chist - 1) * CONF_DIM:chist * CONF_DIM] = u
    cact = _silu(_layernorm(acc + ccb_ref[...], lnw_ref[...], lnb_ref[...]))
    yb = _mm(cact, woc_ref[...])
    sga_ref[...] = jax.nn.sigmoid(jnp.dot(h, wga_ref[...], preferred_element_type=F32))
    mb_ref[...] = jax.nn.sigmoid(jnp.dot(h, wgb_ref[...], preferred_element_type=F32)) * yb


def _delta_step_kernel(beta_ref, a_ref, qt_ref, kt_ref, v_ref, s_ref, *rest, bt):
    o_ref, snew_ref = rest[-2:]
    b0 = pl.program_id(1) * bt

    @pl.when(pl.program_id(0) > 0)
    def _():
        snew_ref[...] = jnp.zeros(snew_ref.shape, F32)

    @pl.when(pl.program_id(0) == 0)
    def _():
        for i in range(bt):
            kt = kt_ref[i]
            qt = qt_ref[i]
            for h in range(N_HEADS):
                beta = beta_ref[(b0 + i) * N_HEADS + h]
                a = a_ref[(b0 + i) * N_HEADS + h]
                s = s_ref[i, h]
                kcol = kt[:, h:h + 1]
                r = jnp.sum(s * kcol, axis=0, keepdims=True)
                d = beta * (v_ref[i, h:h + 1, :] - a * r)
                sn = a * s + kcol * d
                snew_ref[i, h] = sn
                o_ref[i, h:h + 1, :] = jnp.sum(sn * qt[:, h:h + 1], axis=0, keepdims=True)


def _mixer_out_ffn_sample_kernel(o_ref, z_ref, sga_ref, mb_ref, x_ref, sffn_ref, dnw_ref, wod_ref, wout_ref,
                                 nfw_ref, wup_ref, fcw_ref, fcb_ref, wdown_ref, finw_ref,
                                 out_ref, nffn_ref, act_buf, *, final):
    on = _delta_out_norm(o_ref, z_ref, dnw_ref[...])
    ya = jnp.dot(on, wod_ref[...], preferred_element_type=F32)
    merged = sga_ref[...] * ya + mb_ref[...]
    x1 = x_ref[...] + _mm(merged, wout_ref[...])
    h2 = _rmsnorm(x1, nfw_ref[...]).astype(BF16)
    hist = FFN_CONV - 1
    for c in range(D_FF // LANES):
        acts = []
        for off in (0, D_FF):
            cs = slice(off + c * LANES, off + (c + 1) * LANES)
            new = jnp.dot(h2, wup_ref[:, cs], preferred_element_type=F32)
            acc = new * fcw_ref[hist:hist + 1, cs]
            for j in range(hist):
                row = sffn_ref[:, j * 2 * D_FF + off + c * LANES:j * 2 * D_FF + off + (c + 1) * LANES]
                acc = acc + row * fcw_ref[j:j + 1, cs]
                if j > 0:
                    nffn_ref[:, (j - 1) * 2 * D_FF + off + c * LANES:(j - 1) * 2 * D_FF + off + (c + 1) * LANES] = row
            nffn_ref[:, (hist - 1) * 2 * D_FF + off + c * LANES:(hist - 1) * 2 * D_FF + off + (c + 1) * LANES] = new
            acts.append(acc + fcb_ref[:, cs])
        act_buf[:, c * LANES:(c + 1) * LANES] = (_silu(acts[0]) * acts[1]).astype(BF16)
    x2 = x1 + jnp.dot(act_buf[...], wdown_ref[...], preferred_element_type=F32)
    out_ref[...] = _rmsnorm(x2, finw_ref[...]) if final else x2


def _params(n_grid):
    return pltpu.CompilerParams(dimension_semantics=("arbitrary",) * n_grid, vmem_limit_bytes=VMEM_LIMIT)


def _layer_block(arr, l, width=None, blk=0):
    rows, cols = arr.shape[1:]
    return pl.BlockSpec((None, rows, cols if width is None else width), lambda *_: (l, 0, blk),
                        pipeline_mode=pl.Buffered(1))


def _whole(arr):
    nd = len(arr.shape)
    return pl.BlockSpec(arr.shape, lambda *_: (0,) * nd, pipeline_mode=pl.Buffered(1))


def _mixer_in_consts(pw, l, head_major):
    names = ["norm_mix_w", ("w_in", QKV_DIM, 0), ("w_in", D_MODEL, QKV_DIM // D_MODEL), "w_ba"]
    names += ["w_at"] if head_major else []
    names += [("w_tail", D_MODEL, 0), ("w_tail", D_MODEL, 1), ("w_tail", D_MODEL, 2),
              "conv_qkv_w", "alog_row", "dtb_row"]
    names += ["alog_col", "dtb_col"] if head_major else []
    names += ["conf_conv_w", "conf_conv_b", "conf_ln_w", "conf_ln_b", "w_o_conf"]
    out = []
    for n in names:
        if isinstance(n, tuple):
            out.append((pw[n[0]], _layer_block(pw[n[0]], l, n[1], n[2])))
        else:
            out.append((pw[n], _layer_block(pw[n], l)))
    return out


def _mixer_out_consts(pw, l):
    names = ("delta_norm_w", "w_o_delta", "w_out", "norm_ffn_w", "w_up", "ffn_conv_w", "ffn_conv_b", "w_down")
    return [(pw[n], _layer_block(pw[n], l)) for n in names] + [(pw["norm_final_w"], _layer_block(pw["norm_final_w"], 0))]


def _mixer_in_prompt(x, pw, l):
    nb, seq, _ = x.shape
    tm = TILE_M_IN
    tok = lambda n: pl.BlockSpec((None, tm, n), lambda b, t: (b, t, 0))
    consts = _mixer_in_consts(pw, l, head_major=True)
    mats = [_perm_matrix(CHUNK, inverse=False), _chunk_tri(CHUNK, upper=False), _chunk_tri(CHUNK, upper=True)]
    act = jax.ShapeDtypeStruct((nb, seq, D_MODEL), BF16)
    return pl.pallas_call(
        functools.partial(_mixer_in_prompt_kernel, tm=tm),
        grid=(nb, seq // tm),
        in_specs=[tok(D_MODEL)] + [_whole(m) for m in mats] + [c[1] for c in consts],
        out_specs=[tok(D_MODEL)] * 6 + [
            tok(LANES),
            pl.BlockSpec((None, N_HEADS, tm), lambda b, t: (b, 0, t)),
            pl.BlockSpec((None, SHORT_CONV - 1, QKV_DIM), lambda b, t: (b, 0, 0)),
            pl.BlockSpec((None, CONF_CONV - 1, CONF_DIM), lambda b, t: (b, 0, 0)),
        ],
        out_shape=[act] * 6 + [
            jax.ShapeDtypeStruct((nb, seq, LANES), F32),
            jax.ShapeDtypeStruct((nb, N_HEADS, seq), F32),
            jax.ShapeDtypeStruct((nb, SHORT_CONV - 1, QKV_DIM), F32),
            jax.ShapeDtypeStruct((nb, CONF_CONV - 1, CONF_DIM), F32),
        ],
        scratch_shapes=[pltpu.VMEM((tm, CONF_DIM), F32),
                        pltpu.VMEM((tm, D_MODEL), F32),
                        pltpu.VMEM((QKV_TAIL, QKV_DIM), F32),
                        pltpu.VMEM((CHUNK, CONF_DIM), F32)],
        compiler_params=_params(2),
        name="mixer_in_prompt",
    )(x, *mats, *[c[0] for c in consts])


def _delta_prompt(q, k, v, bgc, gct):
    nb, seq, _ = q.shape
    tm = DELTA_CHUNKS * CHUNK
    tok = lambda n: pl.BlockSpec((None, tm, n), lambda b, t: (b, t, 0))
    masks = _delta_masks()
    return pl.pallas_call(
        functools.partial(_delta_chunk_kernel, nc=DELTA_CHUNKS),
        grid=(nb, seq // tm),
        in_specs=[tok(D_MODEL)] * 3 + [tok(LANES), pl.BlockSpec((None, N_HEADS, tm), lambda b, t: (b, 0, t)),
                  _whole(masks)],
        out_specs=[tok(D_MODEL),
                   pl.BlockSpec((None, N_HEADS, HEAD_DIM, HEAD_DIM), lambda b, t: (b, 0, 0, 0))],
        out_shape=[jax.ShapeDtypeStruct((nb, seq, D_MODEL), BF16),
                   jax.ShapeDtypeStruct((nb, N_HEADS, HEAD_DIM, HEAD_DIM), F32)],
        compiler_params=_params(2),
        name="delta_prompt",
    )(q, k, v, bgc, gct, masks)


def _mixer_out_ffn_prompt(o, z, sga, mb, x, pw, l, final):
    nb, seq, _ = x.shape
    tm = TILE_M
    tok = pl.BlockSpec((None, tm, D_MODEL), lambda b, t: (b, t, 0))
    consts = _mixer_out_consts(pw, l)
    unperm = _perm_matrix(CHUNK, inverse=True)
    return pl.pallas_call(
        functools.partial(_mixer_out_ffn_prompt_kernel, tm=tm, final=final),
        grid=(nb, seq // tm),
        in_specs=[tok] * 5 + [_whole(unperm)] + [c[1] for c in consts],
        out_specs=[tok, pl.BlockSpec((None, FFN_CONV - 1, 2 * D_FF), lambda b, t: (b, 0, 0))],
        out_shape=[jax.ShapeDtypeStruct((nb, seq, D_MODEL), F32),
                   jax.ShapeDtypeStruct((nb, FFN_CONV - 1, 2 * D_FF), F32)],
        scratch_shapes=[pltpu.VMEM((FFN_HIST + tm, 2 * D_FF), F32), pltpu.VMEM((tm, D_FF), BF16)],
        compiler_params=_params(2),
        name="mixer_out_ffn_prompt",
    )(o, z, sga, mb, x, unperm, *[c[0] for c in consts])


def _mixer_in_sample(x, sqkv, sconf, pw, l):
    nb = x.shape[0]
    consts = _mixer_in_consts(pw, l, head_major=False)
    act = jax.ShapeDtypeStruct((nb, D_MODEL), F32)
    outs = [act] * 6 + [jax.ShapeDtypeStruct((nb, LANES), F32),
                        jax.ShapeDtypeStruct(sqkv.shape, F32), jax.ShapeDtypeStruct(sconf.shape, F32)]
    return pl.pallas_call(
        _mixer_in_sample_kernel,
        grid=(1,),
        in_specs=[_whole(a) for a in (x, sqkv, sconf)] + [c[1] for c in consts],
        out_specs=[_whole(a) for a in outs],
        out_shape=outs,
        compiler_params=_params(1),
        name="mixer_in_sample",
    )(x, sqkv, sconf, *[c[0] for c in consts])


def _delta_sample(beta, a, qt, kt, v, states, l, new_states):
    depth, nb = states.shape[:2]
    bt = 8
    n_bt = nb // bt
    first = new_states is None
    assert first == (l == 0)
    smem = pl.BlockSpec(memory_space=pltpu.SMEM)

    def tile(j, i):
        return jnp.where(j == 0, i, n_bt - 1)

    args = [beta, a, qt, kt, v, states] + ([] if first else [new_states])
    return pl.pallas_call(
        functools.partial(_delta_step_kernel, bt=bt),
        grid=(depth if first else 1, n_bt),
        in_specs=[smem, smem,
                  pl.BlockSpec((bt, HEAD_DIM, N_HEADS), lambda j, i: (tile(j, i), 0, 0)),
                  pl.BlockSpec((bt, HEAD_DIM, N_HEADS), lambda j, i: (tile(j, i), 0, 0)),
                  pl.BlockSpec((bt, N_HEADS, HEAD_DIM), lambda j, i: (tile(j, i), 0, 0)),
                  pl.BlockSpec((None, bt, N_HEADS, HEAD_DIM, HEAD_DIM), lambda j, i: (l, tile(j, i), 0, 0, 0))]
                 + ([] if first else [pl.BlockSpec(memory_space=pl.ANY)]),
        out_specs=[pl.BlockSpec((bt, N_HEADS, HEAD_DIM), lambda j, i: (tile(j, i), 0, 0)),
                   pl.BlockSpec((None, bt, N_HEADS, HEAD_DIM, HEAD_DIM), lambda j, i: (l + j, i, 0, 0, 0))],
        out_shape=[jax.ShapeDtypeStruct((nb, N_HEADS, HEAD_DIM), F32),
                   jax.ShapeDtypeStruct(states.shape, F32)],
        input_output_aliases={} if first else {len(args) - 1: 1},
        compiler_params=_params(2),
        name="delta_sample",
    )(*args)


def _mixer_out_ffn_sample(o, z, sga, mb, x, sffn, pw, l, final):
    nb = x.shape[0]
    consts = _mixer_out_consts(pw, l)
    outs = [jax.ShapeDtypeStruct((nb, D_MODEL), F32), jax.ShapeDtypeStruct(sffn.shape, F32)]
    return pl.pallas_call(
        functools.partial(_mixer_out_ffn_sample_kernel, final=final),
        grid=(1,),
        in_specs=[_whole(a) for a in (o, z, sga, mb, x, sffn)] + [c[1] for c in consts],
        out_specs=[_whole(a) for a in outs],
        out_shape=outs,
        scratch_shapes=[pltpu.VMEM((nb, D_FF), BF16)],
        compiler_params=_params(1),
        name="mixer_out_ffn_sample",
    )(o, z, sga, mb, x, sffn, *[c[0] for c in consts])


def _prepare_weights(norm_mix_w, w_in, conv_qkv_w, a_log, dt_bias, delta_norm_w, w_o_delta,
                     conf_conv_w, conf_conv_b, conf_ln_w, conf_ln_b, w_o_conf, w_out,
                     norm_ffn_w, w_up, ffn_conv_w, ffn_conv_b, w_down, norm_final_w):
    depth = w_in.shape[0]
    o_b = QKV_DIM + N_HEADS * HEAD_DIM
    o_a = o_b + N_HEADS
    o_glu = o_a + N_HEADS
    pad = LANES - 2 * N_HEADS
    rows = lambda p: p.reshape(depth, 1, -1)
    lane_pad = lambda p: jnp.pad(p.reshape(depth, 1, N_HEADS), ((0, 0), (0, 0), (N_HEADS, pad)))
    return {
        "norm_mix_w": rows(norm_mix_w),
        "w_in": w_in[:, :, :o_b].astype(BF16),
        "w_tail": w_in[:, :, o_glu:].astype(BF16),
        "w_ba": jnp.pad(w_in[:, :, o_b:o_glu], ((0, 0), (0, 0), (0, pad))).astype(BF16),
        "w_at": jnp.swapaxes(w_in[:, :, o_a:o_glu], 1, 2).astype(BF16),
        "conv_qkv_w": conv_qkv_w,
        "alog_row": lane_pad(a_log), "dtb_row": lane_pad(dt_bias),
        "alog_col": a_log.reshape(depth, N_HEADS, 1), "dtb_col": dt_bias.reshape(depth, N_HEADS, 1),
        "conf_conv_w": conf_conv_w, "conf_conv_b": rows(conf_conv_b),
        "conf_ln_w": rows(conf_ln_w), "conf_ln_b": rows(conf_ln_b),
        "w_o_conf": w_o_conf.astype(BF16),
        "delta_norm_w": rows(delta_norm_w),
        "w_o_delta": w_o_delta.astype(BF16),
        "w_out": w_out.astype(BF16),
        "norm_ffn_w": rows(norm_ffn_w),
        "w_up": w_up.astype(BF16),
        "ffn_conv_w": ffn_conv_w, "ffn_conv_b": rows(ffn_conv_b),
        "w_down": w_down.astype(BF16),
        "norm_final_w": norm_final_w.reshape(1, 1, -1),
    }


def kernel(x_prompt, x_sample, state_delta, state_qkv_conv, state_conf_conv, state_ffn_conv, norm_mix_w, w_in, conv_qkv_w, a_log, dt_bias, delta_norm_w, w_o_delta, conf_conv_w, conf_conv_b, conf_ln_w, conf_ln_b, w_o_conf, w_out, norm_ffn_w, w_up, ffn_conv_w, ffn_conv_b, w_down, norm_final_w):
    depth = w_in.shape[0]
    nb_s = x_sample.shape[0]
    pw = _prepare_weights(norm_mix_w, w_in, conv_qkv_w, a_log, dt_bias, delta_norm_w, w_o_delta,
                          conf_conv_w, conf_conv_b, conf_ln_w, conf_ln_b, w_o_conf, w_out,
                          norm_ffn_w, w_up, ffn_conv_w, ffn_conv_b, w_down, norm_final_w)
    xp = x_prompt
    xs = x_sample.reshape(nb_s, D_MODEL)
    new_p = ([], [], [], [])
    new_s = ([], [], [])
    delta_s = None
    for l in range(depth):
        final = l == depth - 1
        q, k, v, z, sga, mb, bgc, gct, n_qkv, n_conf = _mixer_in_prompt(xp, pw, l)
        o, n_delta = _delta_prompt(q, k, v, bgc, gct)
        xp, n_ffn = _mixer_out_ffn_prompt(o, z, sga, mb, xp, pw, l, final)
        for lst, val in zip(new_p, (n_delta, n_qkv, n_conf, n_ffn)):
            lst.append(val)
        q, k, v, z, sga, mb, ba, n_qkv, n_conf = _mixer_in_sample(
            xs, state_qkv_conv[l].reshape(nb_s, -1), state_conf_conv[l].reshape(nb_s, -1), pw, l)
        heads = lambda a: a.reshape(nb_s, N_HEADS, HEAD_DIM)
        o, delta_s = _delta_sample(ba[:, :N_HEADS].reshape(-1), ba[:, N_HEADS:2 * N_HEADS].reshape(-1),
                                   heads(q).transpose(0, 2, 1), heads(k).transpose(0, 2, 1), heads(v),
                                   state_delta, l, delta_s)
        xs, n_ffn = _mixer_out_ffn_sample(o.reshape(nb_s, D_MODEL), z, sga, mb, xs,
                                          state_ffn_conv[l].reshape(nb_s, -1), pw, l, final)
        for lst, val in zip(new_s, (n_qkv.reshape(state_qkv_conv.shape[1:]),
                                    n_conf.reshape(state_conf_conv.shape[1:]),
                                    n_ffn.reshape(state_ffn_conv.shape[1:]))):
            lst.append(val)
    return (xp, xs.reshape(x_sample.shape),
            jnp.stack(new_p[0]), jnp.stack(new_p[1]), jnp.stack(new_p[2]), jnp.stack(new_p[3]),
            delta_s, jnp.stack(new_s[0]), jnp.stack(new_s[1]), jnp.stack(new_s[2]))
```

```python
import functools

import jax
import jax.numpy as jnp
from jax import lax
from jax.experimental import pallas as pl
from jax.experimental.pallas import tpu as pltpu

F32 = jnp.float32
BF16 = jnp.bfloat16

D_MODEL = 1024
N_HEADS = 8
HEAD_DIM = 128
QKV_DIM = 3 * N_HEADS * HEAD_DIM
SHORT_CONV = 4
CONF_DIM = D_MODEL // 2
CONF_CONV = 31
D_FF = 2816
FFN_CONV = 3
NORM_EPS = 1e-6
Q_SCALE = HEAD_DIM ** -0.5

LANES = 128
SUBLANES = 8
MXU_COLS = 256
CHUNK = 128
SLABS = CHUNK // SUBLANES
DELTA_CHUNKS = 2
TILE_M_IN = 512
TILE_M = 256
QKV_TAIL = SUBLANES * (SHORT_CONV - 1)
FFN_HIST = SUBLANES
SAMPLE_BATCH_TILE = 16
VMEM_LIMIT = 60000 * 1024


def _mm(a, b):
    return jnp.dot(a.astype(BF16), b.astype(BF16), preferred_element_type=F32)


def _mm_nt(a, b):
    return lax.dot_general(a.astype(BF16), b.astype(BF16), (((1,), (1,)), ((), ())),
                           preferred_element_type=F32)


def _mm_tn(a, b):
    return lax.dot_general(a.astype(BF16), b.astype(BF16), (((0,), (0,)), ((), ())),
                           preferred_element_type=F32)


def _split3(x):
    hi = x.astype(BF16)
    r1 = x - hi.astype(F32)
    mid = r1.astype(BF16)
    lo = (r1 - mid.astype(F32)).astype(BF16)
    return hi, mid, lo


def _silu(x):
    return x * jax.nn.sigmoid(x)


def _softplus(x):
    return jnp.maximum(x, 0.0) + jnp.log1p(jnp.exp(-jnp.abs(x)))


def _rmsnorm(x, w):
    return x * lax.rsqrt(jnp.mean(x * x, axis=-1, keepdims=True) + NORM_EPS) * w


def _layernorm(x, w, b):
    mu = jnp.mean(x, axis=-1, keepdims=True)
    xc = x - mu
    return xc * lax.rsqrt(jnp.mean(xc * xc, axis=-1, keepdims=True) + NORM_EPS) * w + b


def _token_of_row(r):
    w = r % CHUNK
    return (r // CHUNK) * CHUNK + (w % SUBLANES) * SLABS + w // SUBLANES


def _row_of_token(tok):
    return (tok % SLABS) * SUBLANES + tok // SLABS


def _perm_matrix(n, inverse):
    i = lax.broadcasted_iota(jnp.int32, (n, n), 0)
    j = lax.broadcasted_iota(jnp.int32, (n, n), 1)
    hit = (i == _token_of_row(j)) if inverse else (j == _token_of_row(i))
    return jnp.where(hit, 1.0, 0.0).astype(BF16)


def _chunk_tri(n, upper):
    r = _token_of_row(lax.broadcasted_iota(jnp.int32, (n, n), 0))
    c = _token_of_row(lax.broadcasted_iota(jnp.int32, (n, n), 1))
    same = (r // CHUNK) == (c // CHUNK)
    tri = (r <= c) if upper else (r >= c)
    return jnp.where(same & tri, 1.0, 0.0).astype(BF16)


def _sublane_shifted(cur, prev, e):
    n = cur.shape[0]
    sub = lax.broadcasted_iota(jnp.int32, cur.shape, 0) % SUBLANES
    return jnp.where(sub >= e, pltpu.roll(cur, e, 0), pltpu.roll(prev, n - SUBLANES + e, 0))


def _delayed(zs, d):
    a, b = divmod(d, SLABS)
    if b == 0:
        return zs[a]
    cut = CHUNK - SUBLANES * b
    return jnp.concatenate([zs[a + 1][cut:], zs[a][:cut]], axis=0)


def _qkv_post(y, c):
    y = _silu(y)
    if c < 2 * N_HEADS:
        y = y * lax.rsqrt(jnp.sum(y * y, axis=-1, keepdims=True) + NORM_EPS)
        if c < N_HEADS:
            y = y * Q_SCALE
    return y


def _beta_decay(ba, alog, dtb):
    lane = lax.broadcasted_iota(jnp.int32, ba.shape, 1)
    beta = jnp.where(lane < N_HEADS, jax.nn.sigmoid(ba), 0.0)
    g = -jnp.exp(alog) * _softplus(ba + dtb)
    g = jnp.where((lane >= N_HEADS) & (lane < 2 * N_HEADS), g, 0.0)
    return beta, g


def _mixer_in_prompt_kernel(x_ref, perm_ref, tril_ref, triu_ref, nw_ref, wqkv_ref, wz_ref, wba_ref, wat_ref, wglu_ref, wga_ref, wgb_ref,
                            cw_ref, alog_ref, dtb_ref, alogc_ref, dtbc_ref,
                            ccw_ref, ccb_ref, lnw_ref, lnb_ref, woc_ref,
                            q_ref, k_ref, v_ref, z_ref, sga_ref, mb_ref, bgc_ref, gct_ref,
                            nqkv_ref, nconf_ref,
                            cc_buf, sgb_buf, qkv_carry, u_carry, *, tm):
    t = pl.program_id(1)

    @pl.when(t == 0)
    def _():
        qkv_carry[...] = jnp.zeros(qkv_carry.shape, F32)
        u_carry[...] = jnp.zeros(u_carry.shape, F32)

    xh = _rmsnorm(x_ref[...], nw_ref[...]).astype(BF16)
    chunks = [slice(r0, r0 + CHUNK) for r0 in range(0, tm, CHUNK)]
    h = jnp.concatenate([jnp.dot(perm_ref[...], xh[ch], preferred_element_type=F32) for ch in chunks],
                        axis=0).astype(BF16)
    last = tm - CHUNK
    slabs_per_group = MXU_COLS // LANES

    def state_rows(n_hist):
        return [last + _row_of_token(CHUNK - n_hist + i) for i in range(n_hist)]

    for g in range(CONF_DIM // MXU_COLS):
        gs = slice(g * MXU_COLS, (g + 1) * MXU_COLS)
        u = (jnp.dot(h, wglu_ref[:, gs], preferred_element_type=F32)
             * jax.nn.sigmoid(jnp.dot(h, wglu_ref[:, CONF_DIM + g * MXU_COLS:CONF_DIM + (g + 1) * MXU_COLS],
                                      preferred_element_type=F32)))
        for i, r in enumerate(state_rows(CONF_CONV - 1)):
            nconf_ref[i:i + 1, gs] = u[r:r + 1, :]
        for ci in range(tm // CHUNK):
            r0 = ci * CHUNK
            for j in range(slabs_per_group):
                cs = slice(g * MXU_COLS + j * LANES, g * MXU_COLS + (j + 1) * LANES)
                js = slice(j * LANES, (j + 1) * LANES)
                cur = u[r0:r0 + CHUNK, js]
                prev = u_carry[:, cs] if ci == 0 else u[r0 - CHUNK:r0, js]
                zs = (cur, _sublane_shifted(cur, prev, 1), _sublane_shifted(cur, prev, 2))
                acc = cur * ccw_ref[CONF_CONV - 1:CONF_CONV, cs]
                for d in range(1, CONF_CONV):
                    acc = acc + _delayed(zs, d) * ccw_ref[CONF_CONV - 1 - d:CONF_CONV - d, cs]
                cc_buf[r0:r0 + CHUNK, cs] = acc + ccb_ref[:, cs]
        u_carry[:, gs] = u[last:, :]
    cact = _silu(_layernorm(cc_buf[...], lnw_ref[...], lnb_ref[...])).astype(BF16)

    for g in range(QKV_DIM // MXU_COLS):
        gs = slice(g * MXU_COLS, (g + 1) * MXU_COLS)
        proj = jnp.dot(h, wqkv_ref[:, gs], preferred_element_type=F32)
        for i, r in enumerate(state_rows(SHORT_CONV - 1)):
            nqkv_ref[i:i + 1, gs] = proj[r:r + 1, :]
        for ci in range(tm // CHUNK):
            r0 = ci * CHUNK
            for j in range(slabs_per_group):
                c = g * slabs_per_group + j
                cs = slice(c * LANES, (c + 1) * LANES)
                js = slice(j * LANES, (j + 1) * LANES)
                cur = proj[r0:r0 + CHUNK, js]
                prev_tail = qkv_carry[:, cs] if ci == 0 else proj[r0 - QKV_TAIL:r0, js]
                z1 = jnp.concatenate([cur[:CHUNK - QKV_TAIL],
                                      _sublane_shifted(cur[CHUNK - QKV_TAIL:], prev_tail, 1)], axis=0)
                acc = cur * cw_ref[SHORT_CONV - 1:SHORT_CONV, cs]
                for d in range(1, SHORT_CONV):
                    acc = acc + _delayed((cur, z1), d) * cw_ref[SHORT_CONV - 1 - d:SHORT_CONV - d, cs]
                y = _qkv_post(acc, c)
                y = y.astype(BF16)
                if c < N_HEADS:
                    q_ref[r0:r0 + CHUNK, cs] = y
                elif c < 2 * N_HEADS:
                    k_ref[r0:r0 + CHUNK, (c - N_HEADS) * LANES:(c - N_HEADS + 1) * LANES] = y
                else:
                    v_ref[r0:r0 + CHUNK, (c - 2 * N_HEADS) * LANES:(c - 2 * N_HEADS + 1) * LANES] = y
        qkv_carry[:, gs] = proj[tm - QKV_TAIL:, :]

    for g in range(D_MODEL // MXU_COLS):
        gs = slice(g * MXU_COLS, (g + 1) * MXU_COLS)
        z_ref[:, gs] = jnp.dot(h, wz_ref[:, gs], preferred_element_type=F32).astype(BF16)

    ba = jnp.dot(h, wba_ref[...], preferred_element_type=F32)
    beta, g_tok = _beta_decay(ba, alog_ref[...], dtb_ref[...])
    at = _mm_nt(wat_ref[...], h)
    g_head = -jnp.exp(alogc_ref[...]) * _softplus(at + dtbc_ref[...])
    for ch in chunks:
        gc = sum(jnp.dot(tril_ref[...], part, preferred_element_type=F32) for part in _split3(g_tok[ch]))
        bgc_ref[ch, :] = beta[ch] + gc
        gct_ref[:, ch] = sum(jnp.dot(part, triu_ref[...], preferred_element_type=F32) for part in _split3(g_head[:, ch]))

    for g in range(D_MODEL // MXU_COLS):
        gs = slice(g * MXU_COLS, (g + 1) * MXU_COLS)
        sga_ref[:, gs] = jax.nn.sigmoid(jnp.dot(h, wga_ref[:, gs], preferred_element_type=F32)).astype(BF16)
        sgb_buf[:, gs] = jax.nn.sigmoid(jnp.dot(h, wgb_ref[:, gs], preferred_element_type=F32))

    mb_ref[...] = (sgb_buf[...] * jnp.dot(cact, woc_ref[...], preferred_element_type=F32)).astype(BF16)


def _delta_masks():
    row = lax.broadcasted_iota(jnp.int32, (CHUNK, CHUNK), 0)
    col = lax.broadcasted_iota(jnp.int32, (CHUNK, CHUNK), 1)
    trow = _token_of_row(row)
    tcol = _token_of_row(col)
    masks = [trow >= tcol, trow > tcol, row == col]
    s_blk = 1
    while s_blk < CHUNK:
        masks.append(((trow // (2 * s_blk)) == (tcol // (2 * s_blk)))
                     & ((trow // s_blk) % 2 == 1) & ((tcol // s_blk) % 2 == 0))
        s_blk *= 2
    return jnp.stack(masks).astype(F32)


def _delta_chunk_kernel(q_ref, k_ref, v_ref, bgc_ref, gct_ref, mask_ref, o_ref, s_ref, *, nc):
    t = pl.program_id(1)

    @pl.when(t == 0)
    def _():
        s_ref[...] = jnp.zeros(s_ref.shape, F32)

    causal, strict, eye = 0, 1, 2
    sub_diag = {1 << i: 3 + i for i in range(CHUNK.bit_length() - 1)}
    heads = range(N_HEADS)
    units = [(ci, h) for ci in range(nc) for h in heads]

    def rows(ci):
        return slice(ci * CHUNK, (ci + 1) * CHUNK)

    def cols(h):
        return slice(h * HEAD_DIM, (h + 1) * HEAD_DIM)

    r_last = _row_of_token(CHUNK - 1)
    beta = {(ci, h): bgc_ref[rows(ci), h:h + 1] for ci, h in units}
    gc = {(ci, h): bgc_ref[rows(ci), N_HEADS + h:N_HEADS + h + 1] for ci, h in units}
    glast = {(ci, h): gct_ref[h:h + 1, ci * CHUNK + r_last:ci * CHUNK + r_last + 1]
             for ci, h in units}

    low, qk, rhs = {}, {}, {}
    for u in units:
        ci, h = u
        k = k_ref[rows(ci), cols(h)].astype(F32)
        kb = k * beta[u]
        decay = jnp.exp((gc[u] - gct_ref[h:h + 1, rows(ci)]) * mask_ref[causal]) * mask_ref[causal]
        kq = _mm_nt(jnp.concatenate([kb, q_ref[rows(ci), cols(h)].astype(F32)], axis=0), k)
        low[u] = kq[:CHUNK] * decay * mask_ref[strict]
        qk[u] = (kq[CHUNK:] * decay).astype(BF16)
        rhs[u] = jnp.concatenate([v_ref[rows(ci), cols(h)].astype(F32) * beta[u], kb * jnp.exp(gc[u])],
                                 axis=1).astype(BF16)

    tinv = {u: mask_ref[eye] - low[u] * mask_ref[sub_diag[1]] for u in units}
    s_blk = 2
    while s_blk < CHUNK:
        t16 = {u: tinv[u].astype(BF16) for u in units}
        m1 = {u: jnp.dot((low[u] * mask_ref[sub_diag[s_blk]]).astype(BF16), t16[u],
                         preferred_element_type=F32) for u in units}
        tinv = {u: tinv[u] - jnp.dot(t16[u], m1[u].astype(BF16), preferred_element_type=F32) for u in units}
        s_blk *= 2
    y = {u: jnp.dot(tinv[u].astype(BF16), rhs[u], preferred_element_type=F32) for u in units}

    state = [s_ref[h] for h in heads]
    for ci in range(nc):
        ws_qs = [_mm(jnp.concatenate([y[ci, h][:, HEAD_DIM:],
                                      q_ref[rows(ci), cols(h)].astype(F32) * jnp.exp(gc[ci, h])], axis=0), state[h])
                 for h in heads]
        v_new = [(y[ci, h][:, :HEAD_DIM] - ws_qs[h][:CHUNK]).astype(BF16) for h in heads]
        for h in heads:
            o_ref[rows(ci), cols(h)] = (ws_qs[h][CHUNK:] + jnp.dot(qk[ci, h], v_new[h], preferred_element_type=F32)
                                        ).astype(o_ref.dtype)
        state = [state[h] * jnp.exp(glast[ci, h])
                 + _mm_tn(k_ref[rows(ci), cols(h)].astype(F32) * jnp.exp(glast[ci, h] - gc[ci, h]), v_new[h])
                 for h in heads]
    for h in heads:
        s_ref[h] = state[h]


def _delta_out_norm(o_ref, z_ref, dnw):
    parts = []
    for h in range(N_HEADS):
        hs = slice(h * HEAD_DIM, (h + 1) * HEAD_DIM)
        o = o_ref[:, hs].astype(F32)
        on = o * lax.rsqrt(jnp.mean(o * o, axis=-1, keepdims=True) + NORM_EPS) * dnw
        parts.append((on * _silu(z_ref[:, hs].astype(F32))).astype(BF16))
    return jnp.concatenate(parts, axis=1)


def _mixer_out_ffn_prompt_kernel(o_ref, z_ref, sga_ref, mb_ref, x_ref, unperm_ref, dnw_ref, wod_ref, wout_ref,
                                 nfw_ref, wup_ref, fcw_ref, fcb_ref, wdown_ref, finw_ref,
                                 out_ref, nffn_ref, up_buf, act_buf, *, tm, final):
    t = pl.program_id(1)

    @pl.when(t == 0)
    def _():
        up_buf[0:FFN_HIST, :] = jnp.zeros((FFN_HIST, 2 * D_FF), F32)

    on = _delta_out_norm(o_ref, z_ref, dnw_ref[...])
    ya = jnp.dot(on, wod_ref[...], preferred_element_type=F32)
    merged = (sga_ref[...].astype(F32) * ya + mb_ref[...].astype(F32)).astype(BF16)
    merged = jnp.concatenate([jnp.dot(unperm_ref[...], merged[r0:r0 + CHUNK], preferred_element_type=F32)
                              for r0 in range(0, tm, CHUNK)], axis=0).astype(BF16)
    x1 = x_ref[...] + jnp.dot(merged, wout_ref[...], preferred_element_type=F32)

    h2 = _rmsnorm(x1, nfw_ref[...]).astype(BF16)
    up_buf[FFN_HIST:FFN_HIST + tm, :] = jnp.dot(h2, wup_ref[...], preferred_element_type=F32)
    nffn_ref[...] = up_buf[FFN_HIST + tm - (FFN_CONV - 1):FFN_HIST + tm, :]
    base = FFN_HIST - (FFN_CONV - 1)
    for c in range(D_FF // LANES):
        acts = []
        for off in (0, D_FF):
            cs = slice(off + c * LANES, off + (c + 1) * LANES)
            acc = up_buf[base:base + tm, cs] * fcw_ref[0:1, cs]
            for j in range(1, FFN_CONV):
                acc = acc + up_buf[base + j:base + j + tm, cs] * fcw_ref[j:j + 1, cs]
            acts.append(acc + fcb_ref[:, cs])
        act_buf[:, c * LANES:(c + 1) * LANES] = (_silu(acts[0]) * acts[1]).astype(BF16)
    up_buf[0:FFN_HIST, :] = up_buf[tm:tm + FFN_HIST, :]
    x2 = x1 + jnp.dot(act_buf[...], wdown_ref[...], preferred_element_type=F32)
    out_ref[...] = _rmsnorm(x2, finw_ref[...]) if final else x2


def _mixer_in_sample_kernel(x_ref, sqkv_ref, sconf_ref, nw_ref, wqkv_ref, wz_ref, wba_ref, wglu_ref, wga_ref, wgb_ref,
                            cw_ref, alog_ref, dtb_ref, ccw_ref, ccb_ref, lnw_ref, lnb_ref, woc_ref,
                            q_ref, k_ref, v_ref, z_ref, sga_ref, mb_ref, ba_ref, nqkv_ref, nconf_ref):
    h = _rmsnorm(x_ref[...], nw_ref[...]).astype(BF16)
    hist = SHORT_CONV - 1
    for c in range(QKV_DIM // LANES):
        cs = slice(c * LANES, (c + 1) * LANES)
        new = jnp.dot(h, wqkv_ref[:, cs], preferred_element_type=F32)
        acc = new * cw_ref[hist:hist + 1, cs]
        for j in range(hist):
            row = sqkv_ref[:, j * QKV_DIM + c * LANES:j * QKV_DIM + (c + 1) * LANES]
            acc = acc + row * cw_ref[j:j + 1, cs]
            if j > 0:
                nqkv_ref[:, (j - 1) * QKV_DIM + c * LANES:(j - 1) * QKV_DIM + (c + 1) * LANES] = row
        nqkv_ref[:, (hist - 1) * QKV_DIM + c * LANES:(hist - 1) * QKV_DIM + (c + 1) * LANES] = new
        y = _qkv_post(acc, c)
        if c < N_HEADS:
            q_ref[:, cs] = y
        elif c < 2 * N_HEADS:
            k_ref[:, (c - N_HEADS) * LANES:(c - N_HEADS + 1) * LANES] = y
        else:
            v_ref[:, (c - 2 * N_HEADS) * LANES:(c - 2 * N_HEADS + 1) * LANES] = y

    z_ref[...] = jnp.dot(h, wz_ref[...], preferred_element_type=F32)
    ba = jnp.dot(h, wba_ref[...], preferred_element_type=F32)
    beta, g = _beta_decay(ba, alog_ref[...], dtb_ref[...])
    lane = lax.broadcasted_iota(jnp.int32, ba.shape, 1)
    ba_ref[...] = jnp.where(lane < N_HEADS, beta, jnp.exp(g))

    gl = jnp.dot(h, wglu_ref[...], preferred_element_type=F32)
    u = gl[:, :CONF_DIM] * jax.nn.sigmoid(gl[:, CONF_DIM:])
    chist = CONF_CONV - 1
    acc = u * ccw_ref[chist:chist + 1, :]
    for j in range(chist):
        row = sconf_ref[:, j * CONF_DIM:(j + 1) * CONF_DIM]
        acc = acc + row * ccw_ref[j:j + 1, :]
        if j > 0:
            nconf_ref[:, (j - 1) * CONF_DIM:j * CONF_DIM] = row
    nconf_ref[:, (chist - 1) * CONF_DIM:chist * CONF_DIM] = u
    cact = _silu(_layernorm(acc + ccb_ref[...], lnw_ref[...], lnb_ref[...]))
    yb = _mm(cact, woc_ref[...])
    sga_ref[...] = jax.nn.sigmoid(jnp.dot(h, wga_ref[...], preferred_element_type=F32))
    mb_ref[...] = jax.nn.sigmoid(jnp.dot(h, wgb_ref[...], preferred_element_type=F32)) * yb


def _delta_step_kernel(beta_ref, a_ref, qt_ref, kt_ref, v_ref, s_ref, *rest, bt):
    o_ref, snew_ref = rest[-2:]
    b0 = pl.program_id(1) * bt

    @pl.when(pl.program_id(0) > 0)
    def _():
        snew_ref[...] = jnp.zeros(snew_ref.shape, F32)

    @pl.when(pl.program_id(0) == 0)
    def _():
        for i in range(bt):
            kt = kt_ref[i]
            qt = qt_ref[i]
            for h in range(N_HEADS):
                beta = beta_ref[(b0 + i) * N_HEADS + h]
                a = a_ref[(b0 + i) * N_HEADS + h]
                s = s_ref[i, h]
                kcol = kt[:, h:h + 1]
                r = jnp.sum(s * kcol, axis=0, keepdims=True)
                d = beta * (v_ref[i, h:h + 1, :] - a * r)
                sn = a * s + kcol * d
                snew_ref[i, h] = sn
                o_ref[i, h:h + 1, :] = jnp.sum(sn * qt[:, h:h + 1], axis=0, keepdims=True)


def _mixer_out_ffn_sample_kernel(o_ref, z_ref, sga_ref, mb_ref, x_ref, sffn_ref, dnw_ref, wod_ref, wout_ref,
                                 nfw_ref, wup_ref, fcw_ref, fcb_ref, wdown_ref, finw_ref,
                                 out_ref, nffn_ref, act_buf, *, final):
    on = _delta_out_norm(o_ref, z_ref, dnw_ref[...])
    ya = jnp.dot(on, wod_ref[...], preferred_element_type=F32)
    merged = sga_ref[...] * ya + mb_ref[...]
    x1 = x_ref[...] + _mm(merged, wout_ref[...])
    h2 = _rmsnorm(x1, nfw_ref[...]).astype(BF16)
    hist = FFN_CONV - 1
    for c in range(D_FF // LANES):
        acts = []
        for off in (0, D_FF):
            cs = slice(off + c * LANES, off + (c + 1) * LANES)
            new = jnp.dot(h2, wup_ref[:, cs], preferred_element_type=F32)
            acc = new * fcw_ref[hist:hist + 1, cs]
            for j in range(hist):
                row = sffn_ref[:, j * 2 * D_FF + off + c * LANES:j * 2 * D_FF + off + (c + 1) * LANES]
                acc = acc + row * fcw_ref[j:j + 1, cs]
                if j > 0:
                    nffn_ref[:, (j - 1) * 2 * D_FF + off + c * LANES:(j - 1) * 2 * D_FF + off + (c + 1) * LANES] = row
            nffn_ref[:, (hist - 1) * 2 * D_FF + off + c * LANES:(hist - 1) * 2 * D_FF + off + (c + 1) * LANES] = new
            acts.append(acc + fcb_ref[:, cs])
        act_buf[:, c * LANES:(c + 1) * LANES] = (_silu(acts[0]) * acts[1]).astype(BF16)
    x2 = x1 + jnp.dot(act_buf[...], wdown_ref[...], preferred_element_type=F32)
    out_ref[...] = _rmsnorm(x2, finw_ref[...]) if final else x2


def _params(n_grid):
    return pltpu.CompilerParams(dimension_semantics=("arbitrary",) * n_grid, vmem_limit_bytes=VMEM_LIMIT)


def _layer_block(arr, l, width=None, blk=0):
    rows, cols = arr.shape[1:]
    return pl.BlockSpec((None, rows, cols if width is None else width), lambda *_: (l, 0, blk),
                        pipeline_mode=pl.Buffered(1))


def _whole(arr):
    nd = len(arr.shape)
    return pl.BlockSpec(arr.shape, lambda *_: (0,) * nd, pipeline_mode=pl.Buffered(1))


def _mixer_in_consts(pw, l, head_major):
    names = ["norm_mix_w", ("w_in", QKV_DIM, 0), ("w_in", D_MODEL, QKV_DIM // D_MODEL), "w_ba"]
    names += ["w_at"] if head_major else []
    names += [("w_tail", D_MODEL, 0), ("w_tail", D_MODEL, 1), ("w_tail", D_MODEL, 2),
              "conv_qkv_w", "alog_row", "dtb_row"]
    names += ["alog_col", "dtb_col"] if head_major else []
    names += ["conf_conv_w", "conf_conv_b", "conf_ln_w", "conf_ln_b", "w_o_conf"]
    out = []
    for n in names:
        if isinstance(n, tuple):
            out.append((pw[n[0]], _layer_block(pw[n[0]], l, n[1], n[2])))
        else:
            out.append((pw[n], _layer_block(pw[n], l)))
    return out


def _mixer_out_consts(pw, l):
    names = ("delta_norm_w", "w_o_delta", "w_out", "norm_ffn_w", "w_up", "ffn_conv_w", "ffn_conv_b", "w_down")
    return [(pw[n], _layer_block(pw[n], l)) for n in names] + [(pw["norm_final_w"], _layer_block(pw["norm_final_w"], 0))]


def _mixer_in_prompt(x, pw, l):
    nb, seq, _ = x.shape
    tm = TILE_M_IN
    assert seq % tm == 0 and tm % CHUNK == 0
    tok = lambda n: pl.BlockSpec((None, tm, n), lambda b, t: (b, t, 0))
    consts = _mixer_in_consts(pw, l, head_major=True)
    mats = [_perm_matrix(CHUNK, inverse=False), _chunk_tri(CHUNK, upper=False), _chunk_tri(CHUNK, upper=True)]
    act = jax.ShapeDtypeStruct((nb, seq, D_MODEL), BF16)
    return pl.pallas_call(
        functools.partial(_mixer_in_prompt_kernel, tm=tm),
        grid=(nb, seq // tm),
        in_specs=[tok(D_MODEL)] + [_whole(m) for m in mats] + [c[1] for c in consts],
        out_specs=[tok(D_MODEL)] * 6 + [
            tok(LANES),
            pl.BlockSpec((None, N_HEADS, tm), lambda b, t: (b, 0, t)),
            pl.BlockSpec((None, SHORT_CONV - 1, QKV_DIM), lambda b, t: (b, 0, 0)),
            pl.BlockSpec((None, CONF_CONV - 1, CONF_DIM), lambda b, t: (b, 0, 0)),
        ],
        out_shape=[act] * 6 + [
            jax.ShapeDtypeStruct((nb, seq, LANES), F32),
            jax.ShapeDtypeStruct((nb, N_HEADS, seq), F32),
            jax.ShapeDtypeStruct((nb, SHORT_CONV - 1, QKV_DIM), F32),
            jax.ShapeDtypeStruct((nb, CONF_CONV - 1, CONF_DIM), F32),
        ],
        scratch_shapes=[pltpu.VMEM((tm, CONF_DIM), F32),
                        pltpu.VMEM((tm, D_MODEL), F32),
                        pltpu.VMEM((QKV_TAIL, QKV_DIM), F32),
                        pltpu.VMEM((CHUNK, CONF_DIM), F32)],
        compiler_params=_params(2),
        name="mixer_in_prompt",
    )(x, *mats, *[c[0] for c in consts])


def _delta_prompt(q, k, v, bgc, gct):
    nb, seq, _ = q.shape
    tm = DELTA_CHUNKS * CHUNK
    assert seq % tm == 0
    tok = lambda n: pl.BlockSpec((None, tm, n), lambda b, t: (b, t, 0))
    masks = _delta_masks()
    return pl.pallas_call(
        functools.partial(_delta_chunk_kernel, nc=DELTA_CHUNKS),
        grid=(nb, seq // tm),
        in_specs=[tok(D_MODEL)] * 3 + [tok(LANES), pl.BlockSpec((None, N_HEADS, tm), lambda b, t: (b, 0, t)),
                  _whole(masks)],
        out_specs=[tok(D_MODEL),
                   pl.BlockSpec((None, N_HEADS, HEAD_DIM, HEAD_DIM), lambda b, t: (b, 0, 0, 0))],
        out_shape=[jax.ShapeDtypeStruct((nb, seq, D_MODEL), BF16),
                   jax.ShapeDtypeStruct((nb, N_HEADS, HEAD_DIM, HEAD_DIM), F32)],
        compiler_params=_params(2),
        name="delta_prompt",
    )(q, k, v, bgc, gct, masks)


def _mixer_out_ffn_prompt(o, z, sga, mb, x, pw, l, final):
    nb, seq, _ = x.shape
    tm = TILE_M
    assert seq % tm == 0 and tm % CHUNK == 0
    tok = pl.BlockSpec((None, tm, D_MODEL), lambda b, t: (b, t, 0))
    consts = _mixer_out_consts(pw, l)
    unperm = _perm_matrix(CHUNK, inverse=True)
    return pl.pallas_call(
        functools.partial(_mixer_out_ffn_prompt_kernel, tm=tm, final=final),
        grid=(nb, seq // tm),
        in_specs=[tok] * 5 + [_whole(unperm)] + [c[1] for c in consts],
        out_specs=[tok, pl.BlockSpec((None, FFN_CONV - 1, 2 * D_FF), lambda b, t: (b, 0, 0))],
        out_shape=[jax.ShapeDtypeStruct((nb, seq, D_MODEL), F32),
                   jax.ShapeDtypeStruct((nb, FFN_CONV - 1, 2 * D_FF), F32)],
        scratch_shapes=[pltpu.VMEM((FFN_HIST + tm, 2 * D_FF), F32), pltpu.VMEM((tm, D_FF), BF16)],
        compiler_params=_params(2),
        name="mixer_out_ffn_prompt",
    )(o, z, sga, mb, x, unperm, *[c[0] for c in consts])


def _mixer_in_sample(x, sqkv, sconf, pw, l):
    nb = x.shape[0]
    consts = _mixer_in_consts(pw, l, head_major=False)
    act = jax.ShapeDtypeStruct((nb, D_MODEL), F32)
    outs = [act] * 6 + [jax.ShapeDtypeStruct((nb, LANES), F32),
                        jax.ShapeDtypeStruct(sqkv.shape, F32), jax.ShapeDtypeStruct(sconf.shape, F32)]
    return pl.pallas_call(
        _mixer_in_sample_kernel,
        grid=(1,),
        in_specs=[_whole(a) for a in (x, sqkv, sconf)] + [c[1] for c in consts],
        out_specs=[_whole(a) for a in outs],
        out_shape=outs,
        compiler_params=_params(1),
        name="mixer_in_sample",
    )(x, sqkv, sconf, *[c[0] for c in consts])


def _delta_sample(beta, a, qt, kt, v, states, l, new_states):
    depth, nb = states.shape[:2]
    bt = SAMPLE_BATCH_TILE
    assert nb % bt == 0
    n_bt = nb // bt
    first = new_states is None
    assert first == (l == 0)
    smem = pl.BlockSpec(memory_space=pltpu.SMEM)

    def tile(j, i):
        return jnp.where(j == 0, i, n_bt - 1)

    args = [beta, a, qt, kt, v, states] + ([] if first else [new_states])
    return pl.pallas_call(
        functools.partial(_delta_step_kernel, bt=bt),
        grid=(depth if first else 1, n_bt),
        in_specs=[smem, smem,
                  pl.BlockSpec((bt, HEAD_DIM, N_HEADS), lambda j, i: (tile(j, i), 0, 0)),
                  pl.BlockSpec((bt, HEAD_DIM, N_HEADS), lambda j, i: (tile(j, i), 0, 0)),
                  pl.BlockSpec((bt, N_HEADS, HEAD_DIM), lambda j, i: (tile(j, i), 0, 0)),
                  pl.BlockSpec((None, bt, N_HEADS, HEAD_DIM, HEAD_DIM), lambda j, i: (l, tile(j, i), 0, 0, 0))]
                 + ([] if first else [pl.BlockSpec(memory_space=pl.ANY)]),
        out_specs=[pl.BlockSpec((bt, N_HEADS, HEAD_DIM), lambda j, i: (tile(j, i), 0, 0)),
                   pl.BlockSpec((None, bt, N_HEADS, HEAD_DIM, HEAD_DIM), lambda j, i: (l + j, i, 0, 0, 0))],
        out_shape=[jax.ShapeDtypeStruct((nb, N_HEADS, HEAD_DIM), F32),
                   jax.ShapeDtypeStruct(states.shape, F32)],
        input_output_aliases={} if first else {len(args) - 1: 1},
        compiler_params=_params(2),
        name="delta_sample",
    )(*args)


def _mixer_out_ffn_sample(o, z, sga, mb, x, sffn, pw, l, final):
    nb = x.shape[0]
    consts = _mixer_out_consts(pw, l)
    outs = [jax.ShapeDtypeStruct((nb, D_MODEL), F32), jax.ShapeDtypeStruct(sffn.shape, F32)]
    return pl.pallas_call(
        functools.partial(_mixer_out_ffn_sample_kernel, final=final),
        grid=(1,),
        in_specs=[_whole(a) for a in (o, z, sga, mb, x, sffn)] + [c[1] for c in consts],
        out_specs=[_whole(a) for a in outs],
        out_shape=outs,
        scratch_shapes=[pltpu.VMEM((nb, D_FF), BF16)],
        compiler_params=_params(1),
        name="mixer_out_ffn_sample",
    )(o, z, sga, mb, x, sffn, *[c[0] for c in consts])


def _prepare_weights(norm_mix_w, w_in, conv_qkv_w, a_log, dt_bias, delta_norm_w, w_o_delta,
                     conf_conv_w, conf_conv_b, conf_ln_w, conf_ln_b, w_o_conf, w_out,
                     norm_ffn_w, w_up, ffn_conv_w, ffn_conv_b, w_down, norm_final_w):
    depth = w_in.shape[0]
    o_b = QKV_DIM + N_HEADS * HEAD_DIM
    o_a = o_b + N_HEADS
    o_glu = o_a + N_HEADS
    pad = LANES - 2 * N_HEADS
    rows = lambda p: p.reshape(depth, 1, -1)
    lane_pad = lambda p: jnp.pad(p.reshape(depth, 1, N_HEADS), ((0, 0), (0, 0), (N_HEADS, pad)))
    return {
        "norm_mix_w": rows(norm_mix_w),
        "w_in": w_in[:, :, :o_b].astype(BF16),
        "w_tail": w_in[:, :, o_glu:].astype(BF16),
        "w_ba": jnp.pad(w_in[:, :, o_b:o_glu], ((0, 0), (0, 0), (0, pad))).astype(BF16),
        "w_at": jnp.swapaxes(w_in[:, :, o_a:o_glu], 1, 2).astype(BF16),
        "conv_qkv_w": conv_qkv_w,
        "alog_row": lane_pad(a_log), "dtb_row": lane_pad(dt_bias),
        "alog_col": a_log.reshape(depth, N_HEADS, 1), "dtb_col": dt_bias.reshape(depth, N_HEADS, 1),
        "conf_conv_w": conf_conv_w, "conf_conv_b": rows(conf_conv_b),
        "conf_ln_w": rows(conf_ln_w), "conf_ln_b": rows(conf_ln_b),
        "w_o_conf": w_o_conf.astype(BF16),
        "delta_norm_w": rows(delta_norm_w),
        "w_o_delta": w_o_delta.astype(BF16),
        "w_out": w_out.astype(BF16),
        "norm_ffn_w": rows(norm_ffn_w),
        "w_up": w_up.astype(BF16),
        "ffn_conv_w": ffn_conv_w, "ffn_conv_b": rows(ffn_conv_b),
        "w_down": w_down.astype(BF16),
        "norm_final_w": norm_final_w.reshape(1, 1, -1),
    }


def kernel(x_prompt, x_sample, state_delta, state_qkv_conv, state_conf_conv, state_ffn_conv, norm_mix_w, w_in, conv_qkv_w, a_log, dt_bias, delta_norm_w, w_o_delta, conf_conv_w, conf_conv_b, conf_ln_w, conf_ln_b, w_o_conf, w_out, norm_ffn_w, w_up, ffn_conv_w, ffn_conv_b, w_down, norm_final_w):
    depth = w_in.shape[0]
    nb_s = x_sample.shape[0]
    pw = _prepare_weights(norm_mix_w, w_in, conv_qkv_w, a_log, dt_bias, delta_norm_w, w_o_delta,
                          conf_conv_w, conf_conv_b, conf_ln_w, conf_ln_b, w_o_conf, w_out,
                          norm_ffn_w, w_up, ffn_conv_w, ffn_conv_b, w_down, norm_final_w)
    xp = x_prompt
    xs = x_sample.reshape(nb_s, D_MODEL)
    new_p = ([], [], [], [])
    new_s = ([], [], [])
    delta_s = None
    for l in range(depth):
        final = l == depth - 1
        q, k, v, z, sga, mb, bgc, gct, n_qkv, n_conf = _mixer_in_prompt(xp, pw, l)
        o, n_delta = _delta_prompt(q, k, v, bgc, gct)
        xp, n_ffn = _mixer_out_ffn_prompt(o, z, sga, mb, xp, pw, l, final)
        for lst, val in zip(new_p, (n_delta, n_qkv, n_conf, n_ffn)):
            lst.append(val)
        q, k, v, z, sga, mb, ba, n_qkv, n_conf = _mixer_in_sample(
            xs, state_qkv_conv[l].reshape(nb_s, -1), state_conf_conv[l].reshape(nb_s, -1), pw, l)
        heads = lambda a: a.reshape(nb_s, N_HEADS, HEAD_DIM)
        o, delta_s = _delta_sample(ba[:, :N_HEADS].reshape(-1), ba[:, N_HEADS:2 * N_HEADS].reshape(-1),
                                   heads(q).transpose(0, 2, 1), heads(k).transpose(0, 2, 1), heads(v),
                                   state_delta, l, delta_s)
        xs, n_ffn = _mixer_out_ffn_sample(o.reshape(nb_s, D_MODEL), z, sga, mb, xs,
                                          state_ffn_conv[l].reshape(nb_s, -1), pw, l, final)
        for lst, val in zip(new_s, (n_qkv.reshape(state_qkv_conv.shape[1:]),
                                    n_conf.reshape(state_conf_conv.shape[1:]),
                                    n_ffn.reshape(state_ffn_conv.shape[1:]))):
            lst.append(val)
    return (xp, xs.reshape(x_sample.shape),
            jnp.stack(new_p[0]), jnp.stack(new_p[1]), jnp.stack(new_p[2]), jnp.stack(new_p[3]),
            delta_s, jnp.stack(new_s[0]), jnp.stack(new_s[1]), jnp.stack(new_s[2]))
```

```python
import functools

import jax
import jax.numpy as jnp
from jax import lax
from jax.experimental import pallas as pl
from jax.experimental.pallas import tpu as pltpu

F32 = jnp.float32
BF16 = jnp.bfloat16

D_MODEL = 1024
N_HEADS = 8
HEAD_DIM = 128
QKV_DIM = 3 * N_HEADS * HEAD_DIM
SHORT_CONV = 4
CONF_DIM = D_MODEL // 2
CONF_CONV = 31
D_FF = 2816
FFN_CONV = 3
NORM_EPS = 1e-6
Q_SCALE = HEAD_DIM ** -0.5

LANES = 128
SUBLANES = 8
MXU_COLS = 256
CHUNK = 128
SLABS = CHUNK // SUBLANES
DELTA_CHUNKS = 2
TILE_M_IN = 512
TILE_M = 256
QKV_TAIL = SUBLANES * (SHORT_CONV - 1)
FFN_HIST = SUBLANES
SAMPLE_BATCH_TILE = 16
VMEM_LIMIT = 60000 * 1024


def _mm(a, b):
    return jnp.dot(a.astype(BF16), b.astype(BF16), preferred_element_type=F32)


def _mm_nt(a, b):
    return lax.dot_general(a.astype(BF16), b.astype(BF16), (((1,), (1,)), ((), ())),
                           preferred_element_type=F32)


def _mm_tn(a, b):
    return lax.dot_general(a.astype(BF16), b.astype(BF16), (((0,), (0,)), ((), ())),
                           preferred_element_type=F32)


def _split3(x):
    hi = x.astype(BF16)
    r1 = x - hi.astype(F32)
    mid = r1.astype(BF16)
    lo = (r1 - mid.astype(F32)).astype(BF16)
    return hi, mid, lo


def _silu(x):
    return x * jax.nn.sigmoid(x)


def _softplus(x):
    return jnp.maximum(x, 0.0) + jnp.log1p(jnp.exp(-jnp.abs(x)))


def _rmsnorm(x, w):
    return x * lax.rsqrt(jnp.mean(x * x, axis=-1, keepdims=True) + NORM_EPS) * w


def _layernorm(x, w, b):
    mu = jnp.mean(x, axis=-1, keepdims=True)
    xc = x - mu
    return xc * lax.rsqrt(jnp.mean(xc * xc, axis=-1, keepdims=True) + NORM_EPS) * w + b


def _token_of_row(r):
    w = r % CHUNK
    return (r // CHUNK) * CHUNK + (w % SUBLANES) * SLABS + w // SUBLANES


def _row_of_token(tok):
    return (tok % SLABS) * SUBLANES + tok // SLABS


def _perm_matrix(n, inverse):
    i = lax.broadcasted_iota(jnp.int32, (n, n), 0)
    j = lax.broadcasted_iota(jnp.int32, (n, n), 1)
    hit = (i == _token_of_row(j)) if inverse else (j == _token_of_row(i))
    return jnp.where(hit, 1.0, 0.0).astype(BF16)


def _chunk_tri(n, upper):
    r = _token_of_row(lax.broadcasted_iota(jnp.int32, (n, n), 0))
    c = _token_of_row(lax.broadcasted_iota(jnp.int32, (n, n), 1))
    same = (r // CHUNK) == (c // CHUNK)
    tri = (r <= c) if upper else (r >= c)
    return jnp.where(same & tri, 1.0, 0.0).astype(BF16)


def _sublane_shifted(cur, prev, e):
    n = cur.shape[0]
    sub = lax.broadcasted_iota(jnp.int32, cur.shape, 0) % SUBLANES
    return jnp.where(sub >= e, pltpu.roll(cur, e, 0), pltpu.roll(prev, n - SUBLANES + e, 0))


def _delayed(zs, d):
    a, b = divmod(d, SLABS)
    if b == 0:
        return zs[a]
    cut = CHUNK - SUBLANES * b
    return jnp.concatenate([zs[a + 1][cut:], zs[a][:cut]], axis=0)


def _qkv_post(y, c):
    y = _silu(y)
    if c < 2 * N_HEADS:
        y = y * lax.rsqrt(jnp.sum(y * y, axis=-1, keepdims=True) + NORM_EPS)
        if c < N_HEADS:
            y = y * Q_SCALE
    return y


def _beta_decay(ba, alog, dtb):
    lane = lax.broadcasted_iota(jnp.int32, ba.shape, 1)
    beta = jnp.where(lane < N_HEADS, jax.nn.sigmoid(ba), 0.0)
    g = -jnp.exp(alog) * _softplus(ba + dtb)
    g = jnp.where((lane >= N_HEADS) & (lane < 2 * N_HEADS), g, 0.0)
    return beta, g


def _mixer_in_prompt_kernel(x_ref, perm_ref, tril_ref, triu_ref, nw_ref, wqkv_ref, wz_ref, wba_ref, wat_ref, wglu_ref, wga_ref, wgb_ref,
                            cw_ref, alog_ref, dtb_ref, alogc_ref, dtbc_ref,
                            ccw_ref, ccb_ref, lnw_ref, lnb_ref, woc_ref,
                            q_ref, k_ref, v_ref, z_ref, sga_ref, mb_ref, bgc_ref, gct_ref,
                            nqkv_ref, nconf_ref,
                            cc_buf, sgb_buf, qkv_carry, u_carry, *, tm):
    t = pl.program_id(1)

    @pl.when(t == 0)
    def _():
        qkv_carry[...] = jnp.zeros(qkv_carry.shape, F32)
        u_carry[...] = jnp.zeros(u_carry.shape, F32)

    xh = _rmsnorm(x_ref[...], nw_ref[...]).astype(BF16)
    chunks = [slice(r0, r0 + CHUNK) for r0 in range(0, tm, CHUNK)]
    h = jnp.concatenate([jnp.dot(perm_ref[...], xh[ch], preferred_element_type=F32) for ch in chunks],
                        axis=0).astype(BF16)
    last = tm - CHUNK
    slabs_per_group = MXU_COLS // LANES

    def state_rows(n_hist):
        return [last + _row_of_token(CHUNK - n_hist + i) for i in range(n_hist)]

    for g in range(CONF_DIM // MXU_COLS):
        gs = slice(g * MXU_COLS, (g + 1) * MXU_COLS)
        u = (jnp.dot(h, wglu_ref[:, gs], preferred_element_type=F32)
             * jax.nn.sigmoid(jnp.dot(h, wglu_ref[:, CONF_DIM + g * MXU_COLS:CONF_DIM + (g + 1) * MXU_COLS],
                                      preferred_element_type=F32)))
        for i, r in enumerate(state_rows(CONF_CONV - 1)):
            nconf_ref[i:i + 1, gs] = u[r:r + 1, :]
        for ci in range(tm // CHUNK):
            r0 = ci * CHUNK
            for j in range(slabs_per_group):
                cs = slice(g * MXU_COLS + j * LANES, g * MXU_COLS + (j + 1) * LANES)
                js = slice(j * LANES, (j + 1) * LANES)
                cur = u[r0:r0 + CHUNK, js]
                prev = u_carry[:, cs] if ci == 0 else u[r0 - CHUNK:r0, js]
                zs = (cur, _sublane_shifted(cur, prev, 1), _sublane_shifted(cur, prev, 2))
                acc = cur * ccw_ref[CONF_CONV - 1:CONF_CONV, cs]
                for d in range(1, CONF_CONV):
                    acc = acc + _delayed(zs, d) * ccw_ref[CONF_CONV - 1 - d:CONF_CONV - d, cs]
                cc_buf[r0:r0 + CHUNK, cs] = acc + ccb_ref[:, cs]
        u_carry[:, gs] = u[last:, :]
    cact = _silu(_layernorm(cc_buf[...], lnw_ref[...], lnb_ref[...])).astype(BF16)

    for g in range(QKV_DIM // MXU_COLS):
        gs = slice(g * MXU_COLS, (g + 1) * MXU_COLS)
        proj = jnp.dot(h, wqkv_ref[:, gs], preferred_element_type=F32)
        for i, r in enumerate(state_rows(SHORT_CONV - 1)):
            nqkv_ref[i:i + 1, gs] = proj[r:r + 1, :]
        for ci in range(tm // CHUNK):
            r0 = ci * CHUNK
            for j in range(slabs_per_group):
                c = g * slabs_per_group + j
                cs = slice(c * LANES, (c + 1) * LANES)
                js = slice(j * LANES, (j + 1) * LANES)
                cur = proj[r0:r0 + CHUNK, js]
                prev_tail = qkv_carry[:, cs] if ci == 0 else proj[r0 - QKV_TAIL:r0, js]
                z1 = jnp.concatenate([cur[:CHUNK - QKV_TAIL],
                                      _sublane_shifted(cur[CHUNK - QKV_TAIL:], prev_tail, 1)], axis=0)
                acc = cur * cw_ref[SHORT_CONV - 1:SHORT_CONV, cs]
                for d in range(1, SHORT_CONV):
                    acc = acc + _delayed((cur, z1), d) * cw_ref[SHORT_CONV - 1 - d:SHORT_CONV - d, cs]
                y = _qkv_post(acc, c)
                y = y.astype(BF16)
                if c < N_HEADS:
                    q_ref[r0:r0 + CHUNK, cs] = y
                elif c < 2 * N_HEADS:
                    k_ref[r0:r0 + CHUNK, (c - N_HEADS) * LANES:(c - N_HEADS + 1) * LANES] = y
                else:
                    v_ref[r0:r0 + CHUNK, (c - 2 * N_HEADS) * LANES:(c - 2 * N_HEADS + 1) * LANES] = y
        qkv_carry[:, gs] = proj[tm - QKV_TAIL:, :]

    for g in range(D_MODEL // MXU_COLS):
        gs = slice(g * MXU_COLS, (g + 1) * MXU_COLS)
        z_ref[:, gs] = jnp.dot(h, wz_ref[:, gs], preferred_element_type=F32).astype(BF16)

    ba = jnp.dot(h, wba_ref[...], preferred_element_type=F32)
    beta, g_tok = _beta_decay(ba, alog_ref[...], dtb_ref[...])
    at = _mm_nt(wat_ref[...], h)
    g_head = -jnp.exp(alogc_ref[...]) * _softplus(at + dtbc_ref[...])
    for ch in chunks:
        gc = sum(jnp.dot(tril_ref[...], part, preferred_element_type=F32) for part in _split3(g_tok[ch]))
        bgc_ref[ch, :] = beta[ch] + gc
        gct_ref[:, ch] = sum(jnp.dot(part, triu_ref[...], preferred_element_type=F32) for part in _split3(g_head[:, ch]))

    for g in range(D_MODEL // MXU_COLS):
        gs = slice(g * MXU_COLS, (g + 1) * MXU_COLS)
        sga_ref[:, gs] = jax.nn.sigmoid(jnp.dot(h, wga_ref[:, gs], preferred_element_type=F32)).astype(BF16)
        sgb_buf[:, gs] = jax.nn.sigmoid(jnp.dot(h, wgb_ref[:, gs], preferred_element_type=F32))

    mb_ref[...] = (sgb_buf[...] * jnp.dot(cact, woc_ref[...], preferred_element_type=F32)).astype(BF16)


def _delta_masks():
    row = lax.broadcasted_iota(jnp.int32, (CHUNK, CHUNK), 0)
    col = lax.broadcasted_iota(jnp.int32, (CHUNK, CHUNK), 1)
    trow = _token_of_row(row)
    tcol = _token_of_row(col)
    masks = [trow >= tcol, trow > tcol, row == col]
    s_blk = 1
    while s_blk < CHUNK:
        masks.append(((trow // (2 * s_blk)) == (tcol // (2 * s_blk)))
                     & ((trow // s_blk) % 2 == 1) & ((tcol // s_blk) % 2 == 0))
        s_blk *= 2
    return jnp.stack(masks).astype(F32)


def _delta_chunk_kernel(q_ref, k_ref, v_ref, bgc_ref, gct_ref, mask_ref, o_ref, s_ref, *, nc):
    t = pl.program_id(1)

    @pl.when(t == 0)
    def _():
        s_ref[...] = jnp.zeros(s_ref.shape, F32)

    causal, strict, eye = 0, 1, 2
    sub_diag = {1 << i: 3 + i for i in range(CHUNK.bit_length() - 1)}
    heads = range(N_HEADS)
    units = [(ci, h) for ci in range(nc) for h in heads]

    def rows(ci):
        return slice(ci * CHUNK, (ci + 1) * CHUNK)

    def cols(h):
        return slice(h * HEAD_DIM, (h + 1) * HEAD_DIM)

    r_last = _row_of_token(CHUNK - 1)
    beta = {(ci, h): bgc_ref[rows(ci), h:h + 1] for ci, h in units}
    gc = {(ci, h): bgc_ref[rows(ci), N_HEADS + h:N_HEADS + h + 1] for ci, h in units}
    glast = {(ci, h): gct_ref[h:h + 1, ci * CHUNK + r_last:ci * CHUNK + r_last + 1]
             for ci, h in units}

    low, qk, rhs = {}, {}, {}
    for u in units:
        ci, h = u
        k = k_ref[rows(ci), cols(h)].astype(F32)
        kb = k * beta[u]
        decay = jnp.exp((gc[u] - gct_ref[h:h + 1, rows(ci)]) * mask_ref[causal]) * mask_ref[causal]
        kq = _mm_nt(jnp.concatenate([kb, q_ref[rows(ci), cols(h)].astype(F32)], axis=0), k)
        low[u] = kq[:CHUNK] * decay * mask_ref[strict]
        qk[u] = (kq[CHUNK:] * decay).astype(BF16)
        rhs[u] = jnp.concatenate([v_ref[rows(ci), cols(h)].astype(F32) * beta[u], kb * jnp.exp(gc[u])],
                                 axis=1).astype(BF16)

    tinv = {u: mask_ref[eye] - low[u] * mask_ref[sub_diag[1]] for u in units}
    s_blk = 2
    while s_blk < CHUNK:
        t16 = {u: tinv[u].astype(BF16) for u in units}
        m1 = {u: jnp.dot((low[u] * mask_ref[sub_diag[s_blk]]).astype(BF16), t16[u],
                         preferred_element_type=F32) for u in units}
        tinv = {u: tinv[u] - jnp.dot(t16[u], m1[u].astype(BF16), preferred_element_type=F32) for u in units}
        s_blk *= 2
    y = {u: jnp.dot(tinv[u].astype(BF16), rhs[u], preferred_element_type=F32) for u in units}

    state = [s_ref[h] for h in heads]
    for ci in range(nc):
        ws_qs = [_mm(jnp.concatenate([y[ci, h][:, HEAD_DIM:],
                                      q_ref[rows(ci), cols(h)].astype(F32) * jnp.exp(gc[ci, h])], axis=0), state[h])
                 for h in heads]
        v_new = [(y[ci, h][:, :HEAD_DIM] - ws_qs[h][:CHUNK]).astype(BF16) for h in heads]
        for h in heads:
            o_ref[rows(ci), cols(h)] = (ws_qs[h][CHUNK:] + jnp.dot(qk[ci, h], v_new[h], preferred_element_type=F32)
                                        ).astype(o_ref.dtype)
        state = [state[h] * jnp.exp(glast[ci, h])
                 + _mm_tn(k_ref[rows(ci), cols(h)].astype(F32) * jnp.exp(glast[ci, h] - gc[ci, h]), v_new[h])
                 for h in heads]
    for h in heads:
        s_ref[h] = state[h]


def _delta_out_norm(o_ref, z_ref, dnw):
    parts = []
    for h in range(N_HEADS):
        hs = slice(h * HEAD_DIM, (h + 1) * HEAD_DIM)
        o = o_ref[:, hs].astype(F32)
        on = o * lax.rsqrt(jnp.mean(o * o, axis=-1, keepdims=True) + NORM_EPS) * dnw
        parts.append((on * _silu(z_ref[:, hs].astype(F32))).astype(BF16))
    return jnp.concatenate(parts, axis=1)


def _mixer_out_ffn_prompt_kernel(o_ref, z_ref, sga_ref, mb_ref, x_ref, unperm_ref, dnw_ref, wod_ref, wout_ref,
                                 nfw_ref, wup_ref, fcw_ref, fcb_ref, wdown_ref, finw_ref,
                                 out_ref, nffn_ref, up_buf, act_buf, *, tm, final):
    t = pl.program_id(1)

    @pl.when(t == 0)
    def _():
        up_buf[0:FFN_HIST, :] = jnp.zeros((FFN_HIST, 2 * D_FF), F32)

    on = _delta_out_norm(o_ref, z_ref, dnw_ref[...])
    ya = jnp.dot(on, wod_ref[...], preferred_element_type=F32)
    merged = (sga_ref[...].astype(F32) * ya + mb_ref[...].astype(F32)).astype(BF16)
    merged = jnp.concatenate([jnp.dot(unperm_ref[...], merged[r0:r0 + CHUNK], preferred_element_type=F32)
                              for r0 in range(0, tm, CHUNK)], axis=0).astype(BF16)
    x1 = x_ref[...] + jnp.dot(merged, wout_ref[...], preferred_element_type=F32)

    h2 = _rmsnorm(x1, nfw_ref[...]).astype(BF16)
    up_buf[FFN_HIST:FFN_HIST + tm, :] = jnp.dot(h2, wup_ref[...], preferred_element_type=F32)
    nffn_ref[...] = up_buf[FFN_HIST + tm - (FFN_CONV - 1):FFN_HIST + tm, :]
    base = FFN_HIST - (FFN_CONV - 1)
    for c in range(D_FF // LANES):
        acts = []
        for off in (0, D_FF):
            cs = slice(off + c * LANES, off + (c + 1) * LANES)
            acc = up_buf[base:base + tm, cs] * fcw_ref[0:1, cs]
            for j in range(1, FFN_CONV):
                acc = acc + up_buf[base + j:base + j + tm, cs] * fcw_ref[j:j + 1, cs]
            acts.append(acc + fcb_ref[:, cs])
        act_buf[:, c * LANES:(c + 1) * LANES] = (_silu(acts[0]) * acts[1]).astype(BF16)
    up_buf[0:FFN_HIST, :] = up_buf[tm:tm + FFN_HIST, :]
    x2 = x1 + jnp.dot(act_buf[...], wdown_ref[...], preferred_element_type=F32)
    out_ref[...] = _rmsnorm(x2, finw_ref[...]) if final else x2


def _mixer_in_sample_kernel(x_ref, sqkv_ref, sconf_ref, nw_ref, wqkv_ref, wz_ref, wba_ref, wglu_ref, wga_ref, wgb_ref,
                            cw_ref, alog_ref, dtb_ref, ccw_ref, ccb_ref, lnw_ref, lnb_ref, woc_ref,
                            q_ref, k_ref, v_ref, z_ref, sga_ref, mb_ref, ba_ref, nqkv_ref, nconf_ref):
    h = _rmsnorm(x_ref[...], nw_ref[...]).astype(BF16)
    hist = SHORT_CONV - 1
    for c in range(QKV_DIM // LANES):
        cs = slice(c * LANES, (c + 1) * LANES)
        new = jnp.dot(h, wqkv_ref[:, cs], preferred_element_type=F32)
        acc = new * cw_ref[hist:hist + 1, cs]
        for j in range(hist):
            row = sqkv_ref[:, j * QKV_DIM + c * LANES:j * QKV_DIM + (c + 1) * LANES]
            acc = acc + row * cw_ref[j:j + 1, cs]
            if j > 0:
                nqkv_ref[:, (j - 1) * QKV_DIM + c * LANES:(j - 1) * QKV_DIM + (c + 1) * LANES] = row
        nqkv_ref[:, (hist - 1) * QKV_DIM + c * LANES:(hist - 1) * QKV_DIM + (c + 1) * LANES] = new
        y = _qkv_post(acc, c)
        if c < N_HEADS:
            q_ref[:, cs] = y
        elif c < 2 * N_HEADS:
            k_ref[:, (c - N_HEADS) * LANES:(c - N_HEADS + 1) * LANES] = y
        else:
            v_ref[:, (c - 2 * N_HEADS) * LANES:(c - 2 * N_HEADS + 1) * LANES] = y

    z_ref[...] = jnp.dot(h, wz_ref[...], preferred_element_type=F32)
    ba = jnp.dot(h, wba_ref[...], preferred_element_type=F32)
    beta, g = _beta_decay(ba, alog_ref[...], dtb_ref[...])
    lane = lax.broadcasted_iota(jnp.int32, ba.shape, 1)
    ba_ref[...] = jnp.where(lane < N_HEADS, beta, jnp.exp(g))

    gl = jnp.dot(h, wglu_ref[...], preferred_element_type=F32)
    u = gl[:, :CONF_DIM] * jax.nn.sigmoid(gl[:, CONF_DIM:])
    chist = CONF_CONV - 1
    acc = u * ccw_ref[chist:chist + 1, :]
    for j in range(chist):
        row = sconf_ref[:, j * CONF_DIM:(j + 1) * CONF_DIM]
        acc = acc + row * ccw_ref[j:j + 1, :]
        if j > 0:
            nconf_ref[:, (j - 1) * CONF_DIM:j * CONF_DIM] = row
    nconf_ref[:, (chist - 1) * CONF_DIM:chist * CONF_DIM] = u
    cact = _silu(_layernorm(acc + ccb_ref[...], lnw_ref[...], lnb_ref[...]))
    yb = _mm(cact, woc_ref[...])
    sga_ref[...] = jax.nn.sigmoid(jnp.dot(h, wga_ref[...], preferred_element_type=F32))
    mb_ref[...] = jax.nn.sigmoid(jnp.dot(h, wgb_ref[...], preferred_element_type=F32)) * yb


def _delta_step_kernel(beta_ref, a_ref, qt_ref, kt_ref, v_ref, s_ref, *rest, bt):
    o_ref, snew_ref = rest[-2:]
    b0 = pl.program_id(1) * bt

    @pl.when(pl.program_id(0) > 0)
    def _():
        snew_ref[...] = jnp.zeros(snew_ref.shape, F32)

    @pl.when(pl.program_id(0) == 0)
    def _():
        for i in range(bt):
            kt = kt_ref[i]
            qt = qt_ref[i]
            for h in range(N_HEADS):
                beta = beta_ref[(b0 + i) * N_HEADS + h]
                a = a_ref[(b0 + i) * N_HEADS + h]
                s = s_ref[i, h]
                kcol = kt[:, h:h + 1]
                r = jnp.sum(s * kcol, axis=0, keepdims=True)
                d = beta * (v_ref[i, h:h + 1, :] - a * r)
                sn = a * s + kcol * d
                snew_ref[i, h] = sn
                o_ref[i, h:h + 1, :] = jnp.sum(sn * qt[:, h:h + 1], axis=0, keepdims=True)


def _mixer_out_ffn_sample_kernel(o_ref, z_ref, sga_ref, mb_ref, x_ref, sffn_ref, dnw_ref, wod_ref, wout_ref,
                                 nfw_ref, wup_ref, fcw_ref, fcb_ref, wdown_ref, finw_ref,
                                 out_ref, nffn_ref, act_buf, *, final):
    on = _delta_out_norm(o_ref, z_ref, dnw_ref[...])
    ya = jnp.dot(on, wod_ref[...], preferred_element_type=F32)
    merged = sga_ref[...] * ya + mb_ref[...]
    x1 = x_ref[...] + _mm(merged, wout_ref[...])
    h2 = _rmsnorm(x1, nfw_ref[...]).astype(BF16)
    hist = FFN_CONV - 1
    for c in range(D_FF // LANES):
        acts = []
        for off in (0, D_FF):
            cs = slice(off + c * LANES, off + (c + 1) * LANES)
            new = jnp.dot(h2, wup_ref[:, cs], preferred_element_type=F32)
            acc = new * fcw_ref[hist:hist + 1, cs]
            for j in range(hist):
                row = sffn_ref[:, j * 2 * D_FF + off + c * LANES:j * 2 * D_FF + off + (c + 1) * LANES]
                acc = acc + row * fcw_ref[j:j + 1, cs]
                if j > 0:
                    nffn_ref[:, (j - 1) * 2 * D_FF + off + c * LANES:(j - 1) * 2 * D_FF + off + (c + 1) * LANES] = row
            nffn_ref[:, (hist - 1) * 2 * D_FF + off + c * LANES:(hist - 1) * 2 * D_FF + off + (c + 1) * LANES] = new
            acts.append(acc + fcb_ref[:, cs])
        act_buf[:, c * LANES:(c + 1) * LANES] = (_silu(acts[0]) * acts[1]).astype(BF16)
    x2 = x1 + jnp.dot(act_buf[...], wdown_ref[...], preferred_element_type=F32)
    out_ref[...] = _rmsnorm(x2, finw_ref[...]) if final else x2


def _params(n_grid):
    return pltpu.CompilerParams(dimension_semantics=("arbitrary",) * n_grid, vmem_limit_bytes=VMEM_LIMIT)


def _layer_block(arr, l, width=None, blk=0):
    rows, cols = arr.shape[1:]
    return pl.BlockSpec((None, rows, cols if width is None else width), lambda *_: (l, 0, blk),
                        pipeline_mode=pl.Buffered(1))


def _whole(arr):
    nd = len(arr.shape)
    return pl.BlockSpec(arr.shape, lambda *_: (0,) * nd, pipeline_mode=pl.Buffered(1))


def _mixer_in_consts(pw, l, head_major):
    names = ["norm_mix_w", ("w_in", QKV_DIM, 0), ("w_in", D_MODEL, QKV_DIM // D_MODEL), "w_ba"]
    names += ["w_at"] if head_major else []
    names += [("w_tail", D_MODEL, 0), ("w_tail", D_MODEL, 1), ("w_tail", D_MODEL, 2),
              "conv_qkv_w", "alog_row", "dtb_row"]
    names += ["alog_col", "dtb_col"] if head_major else []
    names += ["conf_conv_w", "conf_conv_b", "conf_ln_w", "conf_ln_b", "w_o_conf"]
    out = []
    for n in names:
        if isinstance(n, tuple):
            out.append((pw[n[0]], _layer_block(pw[n[0]], l, n[1], n[2])))
        else:
            out.append((pw[n], _layer_block(pw[n], l)))
    return out


def _mixer_out_consts(pw, l):
    names = ("delta_norm_w", "w_o_delta", "w_out", "norm_ffn_w", "w_up", "ffn_conv_w", "ffn_conv_b", "w_down")
    return [(pw[n], _layer_block(pw[n], l)) for n in names] + [(pw["norm_final_w"], _layer_block(pw["norm_final_w"], 0))]


def _mixer_in_prompt(x, pw, l):
    nb, seq, _ = x.shape
    tm = TILE_M_IN
    assert seq % tm == 0 and tm % CHUNK == 0
    tok = lambda n: pl.BlockSpec((None, tm, n), lambda b, t: (b, t, 0))
    consts = _mixer_in_consts(pw, l, head_major=True)
    mats = [_perm_matrix(CHUNK, inverse=False), _chunk_tri(CHUNK, upper=False), _chunk_tri(CHUNK, upper=True)]
    act = jax.ShapeDtypeStruct((nb, seq, D_MODEL), BF16)
    return pl.pallas_call(
        functools.partial(_mixer_in_prompt_kernel, tm=tm),
        grid=(nb, seq // tm),
        in_specs=[tok(D_MODEL)] + [_whole(m) for m in mats] + [c[1] for c in consts],
        out_specs=[tok(D_MODEL)] * 6 + [
            tok(LANES),
            pl.BlockSpec((None, N_HEADS, tm), lambda b, t: (b, 0, t)),
            pl.BlockSpec((None, SHORT_CONV - 1, QKV_DIM), lambda b, t: (b, 0, 0)),
            pl.BlockSpec((None, CONF_CONV - 1, CONF_DIM), lambda b, t: (b, 0, 0)),
        ],
        out_shape=[act] * 6 + [
            jax.ShapeDtypeStruct((nb, seq, LANES), F32),
            jax.ShapeDtypeStruct((nb, N_HEADS, seq), F32),
            jax.ShapeDtypeStruct((nb, SHORT_CONV - 1, QKV_DIM), F32),
            jax.ShapeDtypeStruct((nb, CONF_CONV - 1, CONF_DIM), F32),
        ],
        scratch_shapes=[pltpu.VMEM((tm, CONF_DIM), F32),
                        pltpu.VMEM((tm, D_MODEL), F32),
                        pltpu.VMEM((QKV_TAIL, QKV_DIM), F32),
                        pltpu.VMEM((CHUNK, CONF_DIM), F32)],
        compiler_params=_params(2),
        name="mixer_in_prompt",
    )(x, *mats, *[c[0] for c in consts])


def _delta_prompt(q, k, v, bgc, gct):
    nb, seq, _ = q.shape
    tm = DELTA_CHUNKS * CHUNK
    assert seq % tm == 0
    tok = lambda n: pl.BlockSpec((None, tm, n), lambda b, t: (b, t, 0))
    masks = _delta_masks()
    return pl.pallas_call(
        functools.partial(_delta_chunk_kernel, nc=DELTA_CHUNKS),
        grid=(nb, seq // tm),
        in_specs=[tok(D_MODEL)] * 3 + [tok(LANES), pl.BlockSpec((None, N_HEADS, tm), lambda b, t: (b, 0, t)),
                  _whole(masks)],
        out_specs=[tok(D_MODEL),
                   pl.BlockSpec((None, N_HEADS, HEAD_DIM, HEAD_DIM), lambda b, t: (b, 0, 0, 0))],
        out_shape=[jax.ShapeDtypeStruct((nb, seq, D_MODEL), BF16),
                   jax.ShapeDtypeStruct((nb, N_HEADS, HEAD_DIM, HEAD_DIM), F32)],
        compiler_params=_params(2),
        name="delta_prompt",
    )(q, k, v, bgc, gct, masks)


def _mixer_out_ffn_prompt(o, z, sga, mb, x, pw, l, final):
    nb, seq, _ = x.shape
    tm = TILE_M
    assert seq % tm == 0 and tm % CHUNK == 0
    tok = pl.BlockSpec((None, tm, D_MODEL), lambda b, t: (b, t, 0))
    consts = _mixer_out_consts(pw, l)
    unperm = _perm_matrix(CHUNK, inverse=True)
    return pl.pallas_call(
        functools.partial(_mixer_out_ffn_prompt_kernel, tm=tm, final=final),
        grid=(nb, seq // tm),
        in_specs=[tok] * 5 + [_whole(unperm)] + [c[1] for c in consts],
        out_specs=[tok, pl.BlockSpec((None, FFN_CONV - 1, 2 * D_FF), lambda b, t: (b, 0, 0))],
        out_shape=[jax.ShapeDtypeStruct((nb, seq, D_MODEL), F32),
                   jax.ShapeDtypeStruct((nb, FFN_CONV - 1, 2 * D_FF), F32)],
        scratch_shapes=[pltpu.VMEM((FFN_HIST + tm, 2 * D_FF), F32), pltpu.VMEM((tm, D_FF), BF16)],
        compiler_params=_params(2),
        name="mixer_out_ffn_prompt",
    )(o, z, sga, mb, x, unperm, *[c[0] for c in consts])


def _mixer_in_sample(x, sqkv, sconf, pw, l):
    nb = x.shape[0]
    consts = _mixer_in_consts(pw, l, head_major=False)
    act = jax.ShapeDtypeStruct((nb, D_MODEL), F32)
    outs = [act] * 6 + [jax.ShapeDtypeStruct((nb, LANES), F32),
                        jax.ShapeDtypeStruct(sqkv.shape, F32), jax.ShapeDtypeStruct(sconf.shape, F32)]
    return pl.pallas_call(
        _mixer_in_sample_kernel,
        grid=(1,),
        in_specs=[_whole(a) for a in (x, sqkv, sconf)] + [c[1] for c in consts],
        out_specs=[_whole(a) for a in outs],
        out_shape=outs,
        compiler_params=_params(1),
        name="mixer_in_sample",
    )(x, sqkv, sconf, *[c[0] for c in consts])


def _delta_sample(beta, a, qt, kt, v, states, l, new_states):
    depth, nb = states.shape[:2]
    bt = SAMPLE_BATCH_TILE
    assert nb % bt == 0
    n_bt = nb // bt
    first = new_states is None
    assert first == (l == 0)
    smem = pl.BlockSpec(memory_space=pltpu.SMEM)

    def tile(j, i):
        return jnp.where(j == 0, i, n_bt - 1)

    args = [beta, a, qt, kt, v, states] + ([] if first else [new_states])
    return pl.pallas_call(
        functools.partial(_delta_step_kernel, bt=bt),
        grid=(depth if first else 1, n_bt),
        in_specs=[smem, smem,
                  pl.BlockSpec((bt, HEAD_DIM, N_HEADS), lambda j, i: (tile(j, i), 0, 0)),
                  pl.BlockSpec((bt, HEAD_DIM, N_HEADS), lambda j, i: (tile(j, i), 0, 0)),
                  pl.BlockSpec((bt, N_HEADS, HEAD_DIM), lambda j, i: (tile(j, i), 0, 0)),
                  pl.BlockSpec((None, bt, N_HEADS, HEAD_DIM, HEAD_DIM), lambda j, i: (l, tile(j, i), 0, 0, 0))]
                 + ([] if first else [pl.BlockSpec(memory_space=pl.ANY)]),
        out_specs=[pl.BlockSpec((bt, N_HEADS, HEAD_DIM), lambda j, i: (tile(j, i), 0, 0)),
                   pl.BlockSpec((None, bt, N_HEADS, HEAD_DIM, HEAD_DIM), lambda j, i: (l + j, i, 0, 0, 0))],
        out_shape=[jax.ShapeDtypeStruct((nb, N_HEADS, HEAD_DIM), F32),
                   jax.ShapeDtypeStruct(states.shape, F32)],
        input_output_aliases={} if first else {len(args) - 1: 1},
        compiler_params=_params(2),
        name="delta_sample",
    )(*args)


def _mixer_out_ffn_sample(o, z, sga, mb, x, sffn, pw, l, final):
    nb = x.shape[0]
    consts = _mixer_out_consts(pw, l)
    outs = [jax.ShapeDtypeStruct((nb, D_MODEL), F32), jax.ShapeDtypeStruct(sffn.shape, F32)]
    return pl.pallas_call(
        functools.partial(_mixer_out_ffn_sample_kernel, final=final),
        grid=(1,),
        in_specs=[_whole(a) for a in (o, z, sga, mb, x, sffn)] + [c[1] for c in consts],
        out_specs=[_whole(a) for a in outs],
        out_shape=outs,
        scratch_shapes=[pltpu.VMEM((nb, D_FF), BF16)],
        compiler_params=_params(1),
        name="mixer_out_ffn_sample",
    )(o, z, sga, mb, x, sffn, *[c[0] for c in consts])


def _prepare_weights(norm_mix_w, w_in, conv_qkv_w, a_log, dt_bias, delta_norm_w, w_o_delta,
                     conf_conv_w, conf_conv_b, conf_ln_w, conf_ln_b, w_o_conf, w_out,
                     norm_ffn_w, w_up, ffn_conv_w, ffn_conv_b, w_down, norm_final_w):
    depth = w_in.shape[0]
    w16 = w_in.astype(BF16)
    o_b = QKV_DIM + N_HEADS * HEAD_DIM
    o_a = o_b + N_HEADS
    o_glu = o_a + N_HEADS
    pad = LANES - 2 * N_HEADS
    rows = lambda p: p.reshape(depth, 1, -1)
    lane_pad = lambda p: jnp.pad(p.reshape(depth, 1, N_HEADS), ((0, 0), (0, 0), (N_HEADS, pad)))
    return {
        "norm_mix_w": rows(norm_mix_w),
        "w_in": w16,
        "w_tail": w16[:, :, o_glu:],
        "w_ba": jnp.pad(w16[:, :, o_b:o_glu], ((0, 0), (0, 0), (0, pad))),
        "w_at": jnp.swapaxes(w16[:, :, o_a:o_glu], 1, 2),
        "conv_qkv_w": conv_qkv_w,
        "alog_row": lane_pad(a_log), "dtb_row": lane_pad(dt_bias),
        "alog_col": a_log.reshape(depth, N_HEADS, 1), "dtb_col": dt_bias.reshape(depth, N_HEADS, 1),
        "conf_conv_w": conf_conv_w, "conf_conv_b": rows(conf_conv_b),
        "conf_ln_w": rows(conf_ln_w), "conf_ln_b": rows(conf_ln_b),
        "w_o_conf": w_o_conf.astype(BF16),
        "delta_norm_w": rows(delta_norm_w),
        "w_o_delta": w_o_delta.astype(BF16),
        "w_out": w_out.astype(BF16),
        "norm_ffn_w": rows(norm_ffn_w),
        "w_up": w_up.astype(BF16),
        "ffn_conv_w": ffn_conv_w, "ffn_conv_b": rows(ffn_conv_b),
        "w_down": w_down.astype(BF16),
        "norm_final_w": norm_final_w.reshape(1, 1, -1),
    }


def kernel(x_prompt, x_sample, state_delta, state_qkv_conv, state_conf_conv, state_ffn_conv, norm_mix_w, w_in, conv_qkv_w, a_log, dt_bias, delta_norm_w, w_o_delta, conf_conv_w, conf_conv_b, conf_ln_w, conf_ln_b, w_o_conf, w_out, norm_ffn_w, w_up, ffn_conv_w, ffn_conv_b, w_down, norm_final_w):
    depth = w_in.shape[0]
    nb_s = x_sample.shape[0]
    pw = _prepare_weights(norm_mix_w, w_in, conv_qkv_w, a_log, dt_bias, delta_norm_w, w_o_delta,
                          conf_conv_w, conf_conv_b, conf_ln_w, conf_ln_b, w_o_conf, w_out,
                          norm_ffn_w, w_up, ffn_conv_w, ffn_conv_b, w_down, norm_final_w)
    xp = x_prompt
    xs = x_sample.reshape(nb_s, D_MODEL)
    new_p = ([], [], [], [])
    new_s = ([], [], [])
    delta_s = None
    for l in range(depth):
        final = l == depth - 1
        q, k, v, z, sga, mb, bgc, gct, n_qkv, n_conf = _mixer_in_prompt(xp, pw, l)
        o, n_delta = _delta_prompt(q, k, v, bgc, gct)
        xp, n_ffn = _mixer_out_ffn_prompt(o, z, sga, mb, xp, pw, l, final)
        for lst, val in zip(new_p, (n_delta, n_qkv, n_conf, n_ffn)):
            lst.append(val)
        q, k, v, z, sga, mb, ba, n_qkv, n_conf = _mixer_in_sample(
            xs, state_qkv_conv[l].reshape(nb_s, -1), state_conf_conv[l].reshape(nb_s, -1), pw, l)
        heads = lambda a: a.reshape(nb_s, N_HEADS, HEAD_DIM)
        o, delta_s = _delta_sample(ba[:, :N_HEADS].reshape(-1), ba[:, N_HEADS:2 * N_HEADS].reshape(-1),
                                   heads(q).transpose(0, 2, 1), heads(k).transpose(0, 2, 1), heads(v),
                                   state_delta, l, delta_s)
        xs, n_ffn = _mixer_out_ffn_sample(o.reshape(nb_s, D_MODEL), z, sga, mb, xs,
                                          state_ffn_conv[l].reshape(nb_s, -1), pw, l, final)
        for lst, val in zip(new_s, (n_qkv.reshape(state_qkv_conv.shape[1:]),
                                    n_conf.reshape(state_conf_conv.shape[1:]),
                                    n_ffn.reshape(state_ffn_conv.shape[1:]))):
            lst.append(val)
    return (xp, xs.reshape(x_sample.shape),
            jnp.stack(new_p[0]), jnp.stack(new_p[1]), jnp.stack(new_p[2]), jnp.stack(new_p[3]),
            delta_s, jnp.stack(new_s[0]), jnp.stack(new_s[1]), jnp.stack(new_s[2]))
```

```python
import functools

import jax
import jax.numpy as jnp
from jax import lax
from jax.experimental import pallas as pl
from jax.experimental.pallas import tpu as pltpu

F32 = jnp.float32
BF16 = jnp.bfloat16

D_MODEL = 1024
N_HEADS = 8
HEAD_DIM = 128
QKV_DIM = 3 * N_HEADS * HEAD_DIM
SHORT_CONV = 4
CONF_DIM = D_MODEL // 2
CONF_CONV = 31
D_FF = 2816
FFN_CONV = 3
NORM_EPS = 1e-6
Q_SCALE = HEAD_DIM ** -0.5

LANES = 128
SUBLANES = 8
MXU_COLS = 256
CHUNK = 128
SLABS = CHUNK // SUBLANES
DELTA_CHUNKS = 2
TILE_M_IN = 512
TILE_M = 256
QKV_TAIL = SUBLANES * (SHORT_CONV - 1)
FFN_HIST = SUBLANES
SAMPLE_BATCH_TILE = 16
VMEM_LIMIT = 60000 * 1024


def _mm(a, b):
    return jnp.dot(a.astype(BF16), b.astype(BF16), preferred_element_type=F32)


def _mm_nt(a, b):
    return lax.dot_general(a.astype(BF16), b.astype(BF16), (((1,), (1,)), ((), ())),
                           preferred_element_type=F32)


def _mm_tn(a, b):
    return lax.dot_general(a.astype(BF16), b.astype(BF16), (((0,), (0,)), ((), ())),
                           preferred_element_type=F32)


def _split3(x):
    hi = x.astype(BF16)
    r1 = x - hi.astype(F32)
    mid = r1.astype(BF16)
    lo = (r1 - mid.astype(F32)).astype(BF16)
    return hi, mid, lo


def _silu(x):
    return x * jax.nn.sigmoid(x)


def _softplus(x):
    return jnp.maximum(x, 0.0) + jnp.log1p(jnp.exp(-jnp.abs(x)))


def _rmsnorm(x, w):
    return x * lax.rsqrt(jnp.mean(x * x, axis=-1, keepdims=True) + NORM_EPS) * w


def _layernorm(x, w, b):
    mu = jnp.mean(x, axis=-1, keepdims=True)
    xc = x - mu
    return xc * lax.rsqrt(jnp.mean(xc * xc, axis=-1, keepdims=True) + NORM_EPS) * w + b


def _token_of_row(r):
    w = r % CHUNK
    return (r // CHUNK) * CHUNK + (w % SUBLANES) * SLABS + w // SUBLANES


def _row_of_token(tok):
    return (tok % SLABS) * SUBLANES + tok // SLABS


def _perm_matrix(n, inverse):
    i = lax.broadcasted_iota(jnp.int32, (n, n), 0)
    j = lax.broadcasted_iota(jnp.int32, (n, n), 1)
    hit = (i == _token_of_row(j)) if inverse else (j == _token_of_row(i))
    return jnp.where(hit, 1.0, 0.0).astype(BF16)


def _chunk_tri(n, upper):
    r = _token_of_row(lax.broadcasted_iota(jnp.int32, (n, n), 0))
    c = _token_of_row(lax.broadcasted_iota(jnp.int32, (n, n), 1))
    same = (r // CHUNK) == (c // CHUNK)
    tri = (r <= c) if upper else (r >= c)
    return jnp.where(same & tri, 1.0, 0.0).astype(BF16)


def _sublane_shifted(cur, prev, e):
    n = cur.shape[0]
    sub = lax.broadcasted_iota(jnp.int32, cur.shape, 0) % SUBLANES
    return jnp.where(sub >= e, pltpu.roll(cur, e, 0), pltpu.roll(prev, n - SUBLANES + e, 0))


def _delayed(zs, d):
    a, b = divmod(d, SLABS)
    if b == 0:
        return zs[a]
    cut = CHUNK - SUBLANES * b
    return jnp.concatenate([zs[a + 1][cut:], zs[a][:cut]], axis=0)


def _qkv_post(y, c):
    y = _silu(y)
    if c < 2 * N_HEADS:
        y = y * lax.rsqrt(jnp.sum(y * y, axis=-1, keepdims=True) + NORM_EPS)
        if c < N_HEADS:
            y = y * Q_SCALE
    return y


def _beta_decay(ba, alog, dtb):
    lane = lax.broadcasted_iota(jnp.int32, ba.shape, 1)
    beta = jnp.where(lane < N_HEADS, jax.nn.sigmoid(ba), 0.0)
    g = -jnp.exp(alog) * _softplus(ba + dtb)
    g = jnp.where((lane >= N_HEADS) & (lane < 2 * N_HEADS), g, 0.0)
    return beta, g


def _mixer_in_prompt_kernel(x_ref, perm_ref, tril_ref, triu_ref, nw_ref, wqkv_ref, wz_ref, wba_ref, wat_ref, wglu_ref, wga_ref, wgb_ref,
                            cw_ref, alog_ref, dtb_ref, alogc_ref, dtbc_ref,
                            ccw_ref, ccb_ref, lnw_ref, lnb_ref, woc_ref,
                            q_ref, k_ref, v_ref, z_ref, sga_ref, mb_ref, bgc_ref, gct_ref,
                            nqkv_ref, nconf_ref,
                            cc_buf, sgb_buf, qkv_carry, u_carry, *, tm):
    t = pl.program_id(1)

    @pl.when(t == 0)
    def _():
        qkv_carry[...] = jnp.zeros(qkv_carry.shape, F32)
        u_carry[...] = jnp.zeros(u_carry.shape, F32)

    xh = _rmsnorm(x_ref[...], nw_ref[...]).astype(BF16)
    chunks = [slice(r0, r0 + CHUNK) for r0 in range(0, tm, CHUNK)]
    h = jnp.concatenate([jnp.dot(perm_ref[...], xh[ch], preferred_element_type=F32) for ch in chunks],
                        axis=0).astype(BF16)
    slabs_per_group = MXU_COLS // LANES

    for g in range(CONF_DIM // MXU_COLS):
        gs = slice(g * MXU_COLS, (g + 1) * MXU_COLS)
        w_a = wglu_ref[:, gs]
        w_b = wglu_ref[:, CONF_DIM + g * MXU_COLS:CONF_DIM + (g + 1) * MXU_COLS]
        u = None
        for ci in range(tm // CHUNK):
            r0 = ci * CHUNK
            u_prev = u
            u = (jnp.dot(h[r0:r0 + CHUNK], w_a, preferred_element_type=F32)
                 * jax.nn.sigmoid(jnp.dot(h[r0:r0 + CHUNK], w_b, preferred_element_type=F32)))
            for j in range(slabs_per_group):
                cs = slice(g * MXU_COLS + j * LANES, g * MXU_COLS + (j + 1) * LANES)
                js = slice(j * LANES, (j + 1) * LANES)
                cur = u[:, js]
                prev = u_carry[:, cs] if ci == 0 else u_prev[:, js]
                zs = (cur, _sublane_shifted(cur, prev, 1), _sublane_shifted(cur, prev, 2))
                acc = cur * ccw_ref[CONF_CONV - 1:CONF_CONV, cs]
                for d in range(1, CONF_CONV):
                    acc = acc + _delayed(zs, d) * ccw_ref[CONF_CONV - 1 - d:CONF_CONV - d, cs]
                cc_buf[r0:r0 + CHUNK, cs] = acc + ccb_ref[:, cs]
        for i in range(CONF_CONV - 1):
            r = _row_of_token(CHUNK - (CONF_CONV - 1) + i)
            nconf_ref[i:i + 1, gs] = u[r:r + 1, :]
        u_carry[:, gs] = u
    cact = _silu(_layernorm(cc_buf[...], lnw_ref[...], lnb_ref[...])).astype(BF16)

    for g in range(QKV_DIM // MXU_COLS):
        gs = slice(g * MXU_COLS, (g + 1) * MXU_COLS)
        w_g = wqkv_ref[:, gs]
        tail = None
        for ci in range(tm // CHUNK):
            r0 = ci * CHUNK
            proj = jnp.dot(h[r0:r0 + CHUNK], w_g, preferred_element_type=F32)
            for j in range(slabs_per_group):
                c = g * slabs_per_group + j
                cs = slice(c * LANES, (c + 1) * LANES)
                js = slice(j * LANES, (j + 1) * LANES)
                cur = proj[:, js]
                prev_tail = qkv_carry[:, cs] if ci == 0 else tail[:, js]
                z1 = jnp.concatenate([cur[:CHUNK - QKV_TAIL],
                                      _sublane_shifted(cur[CHUNK - QKV_TAIL:], prev_tail, 1)], axis=0)
                acc = cur * cw_ref[SHORT_CONV - 1:SHORT_CONV, cs]
                for d in range(1, SHORT_CONV):
                    acc = acc + _delayed((cur, z1), d) * cw_ref[SHORT_CONV - 1 - d:SHORT_CONV - d, cs]
                y = _qkv_post(acc, c)
                y = y.astype(BF16)
                if c < N_HEADS:
                    q_ref[r0:r0 + CHUNK, cs] = y
                elif c < 2 * N_HEADS:
                    k_ref[r0:r0 + CHUNK, (c - N_HEADS) * LANES:(c - N_HEADS + 1) * LANES] = y
                else:
                    v_ref[r0:r0 + CHUNK, (c - 2 * N_HEADS) * LANES:(c - 2 * N_HEADS + 1) * LANES] = y
            tail = proj[CHUNK - QKV_TAIL:, :]
        for i in range(SHORT_CONV - 1):
            r = _row_of_token(CHUNK - (SHORT_CONV - 1) + i)
            nqkv_ref[i:i + 1, gs] = proj[r:r + 1, :]
        qkv_carry[:, gs] = tail

    for g in range(D_MODEL // MXU_COLS):
        gs = slice(g * MXU_COLS, (g + 1) * MXU_COLS)
        w_g = wz_ref[:, gs]
        for ch in chunks:
            z_ref[ch, gs] = jnp.dot(h[ch], w_g, preferred_element_type=F32).astype(BF16)

    ba = jnp.dot(h, wba_ref[...], preferred_element_type=F32)
    beta, g_tok = _beta_decay(ba, alog_ref[...], dtb_ref[...])
    at = _mm_nt(wat_ref[...], h)
    g_head = -jnp.exp(alogc_ref[...]) * _softplus(at + dtbc_ref[...])
    for ch in chunks:
        gc = sum(jnp.dot(tril_ref[...], part, preferred_element_type=F32) for part in _split3(g_tok[ch]))
        bgc_ref[ch, :] = beta[ch] + gc
        gct_ref[:, ch] = sum(jnp.dot(part, triu_ref[...], preferred_element_type=F32) for part in _split3(g_head[:, ch]))

    for g in range(D_MODEL // MXU_COLS):
        gs = slice(g * MXU_COLS, (g + 1) * MXU_COLS)
        w_a = wga_ref[:, gs]
        w_b = wgb_ref[:, gs]
        for ch in chunks:
            sga_ref[ch, gs] = jax.nn.sigmoid(jnp.dot(h[ch], w_a, preferred_element_type=F32)).astype(BF16)
            sgb_buf[ch, gs] = jax.nn.sigmoid(jnp.dot(h[ch], w_b, preferred_element_type=F32))

    mb_ref[...] = (sgb_buf[...] * jnp.dot(cact, woc_ref[...], preferred_element_type=F32)).astype(BF16)


def _delta_masks():
    row = lax.broadcasted_iota(jnp.int32, (CHUNK, CHUNK), 0)
    col = lax.broadcasted_iota(jnp.int32, (CHUNK, CHUNK), 1)
    trow = _token_of_row(row)
    tcol = _token_of_row(col)
    masks = [trow >= tcol, trow > tcol, row == col]
    s_blk = 1
    while s_blk < CHUNK:
        masks.append(((trow // (2 * s_blk)) == (tcol // (2 * s_blk)))
                     & ((trow // s_blk) % 2 == 1) & ((tcol // s_blk) % 2 == 0))
        s_blk *= 2
    return jnp.stack(masks).astype(F32)


def _delta_chunk_kernel(q_ref, k_ref, v_ref, bgc_ref, gct_ref, mask_ref, o_ref, s_ref, *, nc):
    t = pl.program_id(1)

    @pl.when(t == 0)
    def _():
        s_ref[...] = jnp.zeros(s_ref.shape, F32)

    causal, strict, eye = 0, 1, 2
    sub_diag = {1 << i: 3 + i for i in range(CHUNK.bit_length() - 1)}
    heads = range(N_HEADS)
    units = [(ci, h) for ci in range(nc) for h in heads]

    def rows(ci):
        return slice(ci * CHUNK, (ci + 1) * CHUNK)

    def cols(h):
        return slice(h * HEAD_DIM, (h + 1) * HEAD_DIM)

    r_last = _row_of_token(CHUNK - 1)
    beta = {(ci, h): bgc_ref[rows(ci), h:h + 1] for ci, h in units}
    gc = {(ci, h): bgc_ref[rows(ci), N_HEADS + h:N_HEADS + h + 1] for ci, h in units}
    glast = {(ci, h): gct_ref[h:h + 1, ci * CHUNK + r_last:ci * CHUNK + r_last + 1]
             for ci, h in units}

    low, qk, rhs = {}, {}, {}
    for u in units:
        ci, h = u
        k = k_ref[rows(ci), cols(h)].astype(F32)
        kb = k * beta[u]
        decay = jnp.exp((gc[u] - gct_ref[h:h + 1, rows(ci)]) * mask_ref[causal]) * mask_ref[causal]
        kq = _mm_nt(jnp.concatenate([kb, q_ref[rows(ci), cols(h)].astype(F32)], axis=0), k)
        low[u] = kq[:CHUNK] * decay * mask_ref[strict]
        qk[u] = (kq[CHUNK:] * decay).astype(BF16)
        rhs[u] = jnp.concatenate([v_ref[rows(ci), cols(h)].astype(F32) * beta[u], kb * jnp.exp(gc[u])],
                                 axis=1).astype(BF16)

    tinv = {u: mask_ref[eye] - low[u] * mask_ref[sub_diag[1]] for u in units}
    s_blk = 2
    while s_blk < CHUNK:
        t16 = {u: tinv[u].astype(BF16) for u in units}
        m1 = {u: jnp.dot((low[u] * mask_ref[sub_diag[s_blk]]).astype(BF16), t16[u],
                         preferred_element_type=F32) for u in units}
        tinv = {u: tinv[u] - jnp.dot(t16[u], m1[u].astype(BF16), preferred_element_type=F32) for u in units}
        s_blk *= 2
    y = {u: jnp.dot(tinv[u].astype(BF16), rhs[u], preferred_element_type=F32) for u in units}

    state = [s_ref[h] for h in heads]
    for ci in range(nc):
        ws_qs = [_mm(jnp.concatenate([y[ci, h][:, HEAD_DIM:],
                                      q_ref[rows(ci), cols(h)].astype(F32) * jnp.exp(gc[ci, h])], axis=0), state[h])
                 for h in heads]
        v_new = [(y[ci, h][:, :HEAD_DIM] - ws_qs[h][:CHUNK]).astype(BF16) for h in heads]
        for h in heads:
            o_ref[rows(ci), cols(h)] = (ws_qs[h][CHUNK:] + jnp.dot(qk[ci, h], v_new[h], preferred_element_type=F32)
                                        ).astype(o_ref.dtype)
        state = [state[h] * jnp.exp(glast[ci, h])
                 + _mm_tn(k_ref[rows(ci), cols(h)].astype(F32) * jnp.exp(glast[ci, h] - gc[ci, h]), v_new[h])
                 for h in heads]
    for h in heads:
        s_ref[h] = state[h]


def _delta_out_norm(o_ref, z_ref, dnw):
    parts = []
    for h in range(N_HEADS):
        hs = slice(h * HEAD_DIM, (h + 1) * HEAD_DIM)
        o = o_ref[:, hs].astype(F32)
        on = o * lax.rsqrt(jnp.mean(o * o, axis=-1, keepdims=True) + NORM_EPS) * dnw
        parts.append((on * _silu(z_ref[:, hs].astype(F32))).astype(BF16))
    return jnp.concatenate(parts, axis=1)


def _mixer_out_ffn_prompt_kernel(o_ref, z_ref, sga_ref, mb_ref, x_ref, unperm_ref, dnw_ref, wod_ref, wout_ref,
                                 nfw_ref, wup_ref, fcw_ref, fcb_ref, wdown_ref, finw_ref,
                                 out_ref, nffn_ref, up_buf, act_buf, *, tm, final):
    t = pl.program_id(1)

    @pl.when(t == 0)
    def _():
        up_buf[0:FFN_HIST, :] = jnp.zeros((FFN_HIST, 2 * D_FF), F32)

    on = _delta_out_norm(o_ref, z_ref, dnw_ref[...])
    ya = jnp.dot(on, wod_ref[...], preferred_element_type=F32)
    merged = (sga_ref[...].astype(F32) * ya + mb_ref[...].astype(F32)).astype(BF16)
    merged = jnp.concatenate([jnp.dot(unperm_ref[...], merged[r0:r0 + CHUNK], preferred_element_type=F32)
                              for r0 in range(0, tm, CHUNK)], axis=0).astype(BF16)
    x1 = x_ref[...] + jnp.dot(merged, wout_ref[...], preferred_element_type=F32)

    h2 = _rmsnorm(x1, nfw_ref[...]).astype(BF16)
    up_buf[FFN_HIST:FFN_HIST + tm, :] = jnp.dot(h2, wup_ref[...], preferred_element_type=F32)
    nffn_ref[...] = up_buf[FFN_HIST + tm - (FFN_CONV - 1):FFN_HIST + tm, :]
    base = FFN_HIST - (FFN_CONV - 1)
    for c in range(D_FF // LANES):
        acts = []
        for off in (0, D_FF):
            cs = slice(off + c * LANES, off + (c + 1) * LANES)
            acc = up_buf[base:base + tm, cs] * fcw_ref[0:1, cs]
            for j in range(1, FFN_CONV):
                acc = acc + up_buf[base + j:base + j + tm, cs] * fcw_ref[j:j + 1, cs]
            acts.append(acc + fcb_ref[:, cs])
        act_buf[:, c * LANES:(c + 1) * LANES] = (_silu(acts[0]) * acts[1]).astype(BF16)
    up_buf[0:FFN_HIST, :] = up_buf[tm:tm + FFN_HIST, :]
    x2 = x1 + jnp.dot(act_buf[...], wdown_ref[...], preferred_element_type=F32)
    out_ref[...] = _rmsnorm(x2, finw_ref[...]) if final else x2


def _mixer_in_sample_kernel(x_ref, sqkv_ref, sconf_ref, nw_ref, wqkv_ref, wz_ref, wba_ref, wglu_ref, wga_ref, wgb_ref,
                            cw_ref, alog_ref, dtb_ref, ccw_ref, ccb_ref, lnw_ref, lnb_ref, woc_ref,
                            q_ref, k_ref, v_ref, z_ref, sga_ref, mb_ref, ba_ref, nqkv_ref, nconf_ref):
    h = _rmsnorm(x_ref[...], nw_ref[...]).astype(BF16)
    hist = SHORT_CONV - 1
    for c in range(QKV_DIM // LANES):
        cs = slice(c * LANES, (c + 1) * LANES)
        new = jnp.dot(h, wqkv_ref[:, cs], preferred_element_type=F32)
        acc = new * cw_ref[hist:hist + 1, cs]
        for j in range(hist):
            row = sqkv_ref[:, j * QKV_DIM + c * LANES:j * QKV_DIM + (c + 1) * LANES]
            acc = acc + row * cw_ref[j:j + 1, cs]
            if j > 0:
                nqkv_ref[:, (j - 1) * QKV_DIM + c * LANES:(j - 1) * QKV_DIM + (c + 1) * LANES] = row
        nqkv_ref[:, (hist - 1) * QKV_DIM + c * LANES:(hist - 1) * QKV_DIM + (c + 1) * LANES] = new
        y = _qkv_post(acc, c)
        if c < N_HEADS:
            q_ref[:, cs] = y
        elif c < 2 * N_HEADS:
            k_ref[:, (c - N_HEADS) * LANES:(c - N_HEADS + 1) * LANES] = y
        else:
            v_ref[:, (c - 2 * N_HEADS) * LANES:(c - 2 * N_HEADS + 1) * LANES] = y

    z_ref[...] = jnp.dot(h, wz_ref[...], preferred_element_type=F32)
    ba = jnp.dot(h, wba_ref[...], preferred_element_type=F32)
    beta, g = _beta_decay(ba, alog_ref[...], dtb_ref[...])
    lane = lax.broadcasted_iota(jnp.int32, ba.shape, 1)
    ba_ref[...] = jnp.where(lane < N_HEADS, beta, jnp.exp(g))

    gl = jnp.dot(h, wglu_ref[...], preferred_element_type=F32)
    u = gl[:, :CONF_DIM] * jax.nn.sigmoid(gl[:, CONF_DIM:])
    chist = CONF_CONV - 1
    acc = u * ccw_ref[chist:chist + 1, :]
    for j in range(chist):
        row = sconf_ref[:, j * CONF_DIM:(j + 1) * CONF_DIM]
        acc = acc + row * ccw_ref[j:j + 1, :]
        if j > 0:
            nconf_ref[:, (j - 1) * CONF_DIM:j * CONF_DIM] = row
    nconf_ref[:, (chist - 1) * CONF_DIM:chist * CONF_DIM] = u
    cact = _silu(_layernorm(acc + ccb_ref[...], lnw_ref[...], lnb_ref[...]))
    yb = _mm(cact, woc_ref[...])
    sga_ref[...] = jax.nn.sigmoid(jnp.dot(h, wga_ref[...], preferred_element_type=F32))
    mb_ref[...] = jax.nn.sigmoid(jnp.dot(h, wgb_ref[...], preferred_element_type=F32)) * yb


def _delta_step_kernel(beta_ref, a_ref, qt_ref, kt_ref, v_ref, s_ref, *rest, bt):
    o_ref, snew_ref = rest[-2:]
    b0 = pl.program_id(1) * bt

    @pl.when(pl.program_id(0) > 0)
    def _():
        snew_ref[...] = jnp.zeros(snew_ref.shape, F32)

    @pl.when(pl.program_id(0) == 0)
    def _():
        for i in range(bt):
            kt = kt_ref[i]
            qt = qt_ref[i]
            for h in range(N_HEADS):
                beta = beta_ref[(b0 + i) * N_HEADS + h]
                a = a_ref[(b0 + i) * N_HEADS + h]
                s = s_ref[i, h]
                kcol = kt[:, h:h + 1]
                r = jnp.sum(s * kcol, axis=0, keepdims=True)
                d = beta * (v_ref[i, h:h + 1, :] - a * r)
                sn = a * s + kcol * d
                snew_ref[i, h] = sn
                o_ref[i, h:h + 1, :] = jnp.sum(sn * qt[:, h:h + 1], axis=0, keepdims=True)


def _mixer_out_ffn_sample_kernel(o_ref, z_ref, sga_ref, mb_ref, x_ref, sffn_ref, dnw_ref, wod_ref, wout_ref,
                                 nfw_ref, wup_ref, fcw_ref, fcb_ref, wdown_ref, finw_ref,
                                 out_ref, nffn_ref, act_buf, *, final):
    on = _delta_out_norm(o_ref, z_ref, dnw_ref[...])
    ya = jnp.dot(on, wod_ref[...], preferred_element_type=F32)
    merged = sga_ref[...] * ya + mb_ref[...]
    x1 = x_ref[...] + _mm(merged, wout_ref[...])
    h2 = _rmsnorm(x1, nfw_ref[...]).astype(BF16)
    hist = FFN_CONV - 1
    for c in range(D_FF // LANES):
        acts = []
        for off in (0, D_FF):
            cs = slice(off + c * LANES, off + (c + 1) * LANES)
            new = jnp.dot(h2, wup_ref[:, cs], preferred_element_type=F32)
            acc = new * fcw_ref[hist:hist + 1, cs]
            for j in range(hist):
                row = sffn_ref[:, j * 2 * D_FF + off + c * LANES:j * 2 * D_FF + off + (c + 1) * LANES]
                acc = acc + row * fcw_ref[j:j + 1, cs]
                if j > 0:
                    nffn_ref[:, (j - 1) * 2 * D_FF + off + c * LANES:(j - 1) * 2 * D_FF + off + (c + 1) * LANES] = row
            nffn_ref[:, (hist - 1) * 2 * D_FF + off + c * LANES:(hist - 1) * 2 * D_FF + off + (c + 1) * LANES] = new
            acts.append(acc + fcb_ref[:, cs])
        act_buf[:, c * LANES:(c + 1) * LANES] = (_silu(acts[0]) * acts[1]).astype(BF16)
    x2 = x1 + jnp.dot(act_buf[...], wdown_ref[...], preferred_element_type=F32)
    out_ref[...] = _rmsnorm(x2, finw_ref[...]) if final else x2


def _params(n_grid):
    return pltpu.CompilerParams(dimension_semantics=("arbitrary",) * n_grid, vmem_limit_bytes=VMEM_LIMIT)


def _layer_block(arr, l, width=None, blk=0):
    rows, cols = arr.shape[1:]
    return pl.BlockSpec((None, rows, cols if width is None else width), lambda *_: (l, 0, blk),
                        pipeline_mode=pl.Buffered(1))


def _whole(arr):
    nd = len(arr.shape)
    return pl.BlockSpec(arr.shape, lambda *_: (0,) * nd, pipeline_mode=pl.Buffered(1))


def _mixer_in_consts(pw, l, head_major):
    names = ["norm_mix_w", ("w_in", QKV_DIM, 0), ("w_in", D_MODEL, QKV_DIM // D_MODEL), "w_ba"]
    names += ["w_at"] if head_major else []
    names += [("w_tail", D_MODEL, 0), ("w_tail", D_MODEL, 1), ("w_tail", D_MODEL, 2),
              "conv_qkv_w", "alog_row", "dtb_row"]
    names += ["alog_col", "dtb_col"] if head_major else []
    names += ["conf_conv_w", "conf_conv_b", "conf_ln_w", "conf_ln_b", "w_o_conf"]
    out = []
    for n in names:
        if isinstance(n, tuple):
            out.append((pw[n[0]], _layer_block(pw[n[0]], l, n[1], n[2])))
        else:
            out.append((pw[n], _layer_block(pw[n], l)))
    return out


def _mixer_out_consts(pw, l):
    names = ("delta_norm_w", "w_o_delta", "w_out", "norm_ffn_w", "w_up", "ffn_conv_w", "ffn_conv_b", "w_down")
    return [(pw[n], _layer_block(pw[n], l)) for n in names] + [(pw["norm_final_w"], _layer_block(pw["norm_final_w"], 0))]


def _mixer_in_prompt(x, pw, l):
    nb, seq, _ = x.shape
    tm = TILE_M_IN
    assert seq % tm == 0 and tm % CHUNK == 0
    tok = lambda n: pl.BlockSpec((None, tm, n), lambda b, t: (b, t, 0))
    consts = _mixer_in_consts(pw, l, head_major=True)
    mats = [_perm_matrix(CHUNK, inverse=False), _chunk_tri(CHUNK, upper=False), _chunk_tri(CHUNK, upper=True)]
    act = jax.ShapeDtypeStruct((nb, seq, D_MODEL), BF16)
    return pl.pallas_call(
        functools.partial(_mixer_in_prompt_kernel, tm=tm),
        grid=(nb, seq // tm),
        in_specs=[tok(D_MODEL)] + [_whole(m) for m in mats] + [c[1] for c in consts],
        out_specs=[tok(D_MODEL)] * 6 + [
            tok(LANES),
            pl.BlockSpec((None, N_HEADS, tm), lambda b, t: (b, 0, t)),
            pl.BlockSpec((None, SHORT_CONV - 1, QKV_DIM), lambda b, t: (b, 0, 0)),
            pl.BlockSpec((None, CONF_CONV - 1, CONF_DIM), lambda b, t: (b, 0, 0)),
        ],
        out_shape=[act] * 6 + [
            jax.ShapeDtypeStruct((nb, seq, LANES), F32),
            jax.ShapeDtypeStruct((nb, N_HEADS, seq), F32),
            jax.ShapeDtypeStruct((nb, SHORT_CONV - 1, QKV_DIM), F32),
            jax.ShapeDtypeStruct((nb, CONF_CONV - 1, CONF_DIM), F32),
        ],
        scratch_shapes=[pltpu.VMEM((tm, CONF_DIM), F32),
                        pltpu.VMEM((tm, D_MODEL), F32),
                        pltpu.VMEM((QKV_TAIL, QKV_DIM), F32),
                        pltpu.VMEM((CHUNK, CONF_DIM), F32)],
        compiler_params=_params(2),
        name="mixer_in_prompt",
    )(x, *mats, *[c[0] for c in consts])


def _delta_prompt(q, k, v, bgc, gct):
    nb, seq, _ = q.shape
    tm = DELTA_CHUNKS * CHUNK
    assert seq % tm == 0
    tok = lambda n: pl.BlockSpec((None, tm, n), lambda b, t: (b, t, 0))
    masks = _delta_masks()
    return pl.pallas_call(
        functools.partial(_delta_chunk_kernel, nc=DELTA_CHUNKS),
        grid=(nb, seq // tm),
        in_specs=[tok(D_MODEL)] * 3 + [tok(LANES), pl.BlockSpec((None, N_HEADS, tm), lambda b, t: (b, 0, t)),
                  _whole(masks)],
        out_specs=[tok(D_MODEL),
                   pl.BlockSpec((None, N_HEADS, HEAD_DIM, HEAD_DIM), lambda b, t: (b, 0, 0, 0))],
        out_shape=[jax.ShapeDtypeStruct((nb, seq, D_MODEL), BF16),
                   jax.ShapeDtypeStruct((nb, N_HEADS, HEAD_DIM, HEAD_DIM), F32)],
        compiler_params=_params(2),
        name="delta_prompt",
    )(q, k, v, bgc, gct, masks)


def _mixer_out_ffn_prompt(o, z, sga, mb, x, pw, l, final):
    nb, seq, _ = x.shape
    tm = TILE_M
    assert seq % tm == 0 and tm % CHUNK == 0
    tok = pl.BlockSpec((None, tm, D_MODEL), lambda b, t: (b, t, 0))
    consts = _mixer_out_consts(pw, l)
    unperm = _perm_matrix(CHUNK, inverse=True)
    return pl.pallas_call(
        functools.partial(_mixer_out_ffn_prompt_kernel, tm=tm, final=final),
        grid=(nb, seq // tm),
        in_specs=[tok] * 5 + [_whole(unperm)] + [c[1] for c in consts],
        out_specs=[tok, pl.BlockSpec((None, FFN_CONV - 1, 2 * D_FF), lambda b, t: (b, 0, 0))],
        out_shape=[jax.ShapeDtypeStruct((nb, seq, D_MODEL), F32),
                   jax.ShapeDtypeStruct((nb, FFN_CONV - 1, 2 * D_FF), F32)],
        scratch_shapes=[pltpu.VMEM((FFN_HIST + tm, 2 * D_FF), F32), pltpu.VMEM((tm, D_FF), BF16)],
        compiler_params=_params(2),
        name="mixer_out_ffn_prompt",
    )(o, z, sga, mb, x, unperm, *[c[0] for c in consts])


def _mixer_in_sample(x, sqkv, sconf, pw, l):
    nb = x.shape[0]
    consts = _mixer_in_consts(pw, l, head_major=False)
    act = jax.ShapeDtypeStruct((nb, D_MODEL), F32)
    outs = [act] * 6 + [jax.ShapeDtypeStruct((nb, LANES), F32),
                        jax.ShapeDtypeStruct(sqkv.shape, F32), jax.ShapeDtypeStruct(sconf.shape, F32)]
    return pl.pallas_call(
        _mixer_in_sample_kernel,
        grid=(1,),
        in_specs=[_whole(a) for a in (x, sqkv, sconf)] + [c[1] for c in consts],
        out_specs=[_whole(a) for a in outs],
        out_shape=outs,
        compiler_params=_params(1),
        name="mixer_in_sample",
    )(x, sqkv, sconf, *[c[0] for c in consts])


def _delta_sample(beta, a, qt, kt, v, states, l, new_states):
    depth, nb = states.shape[:2]
    bt = SAMPLE_BATCH_TILE
    assert nb % bt == 0
    n_bt = nb // bt
    first = new_states is None
    assert first == (l == 0)
    smem = pl.BlockSpec(memory_space=pltpu.SMEM)

    def tile(j, i):
        return jnp.where(j == 0, i, n_bt - 1)

    args = [beta, a, qt, kt, v, states] + ([] if first else [new_states])
    return pl.pallas_call(
        functools.partial(_delta_step_kernel, bt=bt),
        grid=(depth if first else 1, n_bt),
        in_specs=[smem, smem,
                  pl.BlockSpec((bt, HEAD_DIM, N_HEADS), lambda j, i: (tile(j, i), 0, 0)),
                  pl.BlockSpec((bt, HEAD_DIM, N_HEADS), lambda j, i: (tile(j, i), 0, 0)),
                  pl.BlockSpec((bt, N_HEADS, HEAD_DIM), lambda j, i: (tile(j, i), 0, 0)),
                  pl.BlockSpec((None, bt, N_HEADS, HEAD_DIM, HEAD_DIM), lambda j, i: (l, tile(j, i), 0, 0, 0))]
                 + ([] if first else [pl.BlockSpec(memory_space=pl.ANY)]),
        out_specs=[pl.BlockSpec((bt, N_HEADS, HEAD_DIM), lambda j, i: (tile(j, i), 0, 0)),
                   pl.BlockSpec((None, bt, N_HEADS, HEAD_DIM, HEAD_DIM), lambda j, i: (l + j, i, 0, 0, 0))],
        out_shape=[jax.ShapeDtypeStruct((nb, N_HEADS, HEAD_DIM), F32),
                   jax.ShapeDtypeStruct(states.shape, F32)],
        input_output_aliases={} if first else {len(args) - 1: 1},
        compiler_params=_params(2),
        name="delta_sample",
    )(*args)


def _mixer_out_ffn_sample(o, z, sga, mb, x, sffn, pw, l, final):
    nb = x.shape[0]
    consts = _mixer_out_consts(pw, l)
    outs = [jax.ShapeDtypeStruct((nb, D_MODEL), F32), jax.ShapeDtypeStruct(sffn.shape, F32)]
    return pl.pallas_call(
        functools.partial(_mixer_out_ffn_sample_kernel, final=final),
        grid=(1,),
        in_specs=[_whole(a) for a in (o, z, sga, mb, x, sffn)] + [c[1] for c in consts],
        out_specs=[_whole(a) for a in outs],
        out_shape=outs,
        scratch_shapes=[pltpu.VMEM((nb, D_FF), BF16)],
        compiler_params=_params(1),
        name="mixer_out_ffn_sample",
    )(o, z, sga, mb, x, sffn, *[c[0] for c in consts])


def _prepare_weights(norm_mix_w, w_in, conv_qkv_w, a_log, dt_bias, delta_norm_w, w_o_delta,
                     conf_conv_w, conf_conv_b, conf_ln_w, conf_ln_b, w_o_conf, w_out,
                     norm_ffn_w, w_up, ffn_conv_w, ffn_conv_b, w_down, norm_final_w):
    depth = w_in.shape[0]
    w16 = w_in.astype(BF16)
    o_b = QKV_DIM + N_HEADS * HEAD_DIM
    o_a = o_b + N_HEADS
    o_glu = o_a + N_HEADS
    pad = LANES - 2 * N_HEADS
    rows = lambda p: p.reshape(depth, 1, -1)
    lane_pad = lambda p: jnp.pad(p.reshape(depth, 1, N_HEADS), ((0, 0), (0, 0), (N_HEADS, pad)))
    return {
        "norm_mix_w": rows(norm_mix_w),
        "w_in": w16,
        "w_tail": w16[:, :, o_glu:],
        "w_ba": jnp.pad(w16[:, :, o_b:o_glu], ((0, 0), (0, 0), (0, pad))),
        "w_at": jnp.swapaxes(w16[:, :, o_a:o_glu], 1, 2),
        "conv_qkv_w": conv_qkv_w,
        "alog_row": lane_pad(a_log), "dtb_row": lane_pad(dt_bias),
        "alog_col": a_log.reshape(depth, N_HEADS, 1), "dtb_col": dt_bias.reshape(depth, N_HEADS, 1),
        "conf_conv_w": conf_conv_w, "conf_conv_b": rows(conf_conv_b),
        "conf_ln_w": rows(conf_ln_w), "conf_ln_b": rows(conf_ln_b),
        "w_o_conf": w_o_conf.astype(BF16),
        "delta_norm_w": rows(delta_norm_w),
        "w_o_delta": w_o_delta.astype(BF16),
        "w_out": w_out.astype(BF16),
        "norm_ffn_w": rows(norm_ffn_w),
        "w_up": w_up.astype(BF16),
        "ffn_conv_w": ffn_conv_w, "ffn_conv_b": rows(ffn_conv_b),
        "w_down": w_down.astype(BF16),
        "norm_final_w": norm_final_w.reshape(1, 1, -1),
    }


def kernel(x_prompt, x_sample, state_delta, state_qkv_conv, state_conf_conv, state_ffn_conv, norm_mix_w, w_in, conv_qkv_w, a_log, dt_bias, delta_norm_w, w_o_delta, conf_conv_w, conf_conv_b, conf_ln_w, conf_ln_b, w_o_conf, w_out, norm_ffn_w, w_up, ffn_conv_w, ffn_conv_b, w_down, norm_final_w):
    depth = w_in.shape[0]
    nb_s = x_sample.shape[0]
    pw = _prepare_weights(norm_mix_w, w_in, conv_qkv_w, a_log, dt_bias, delta_norm_w, w_o_delta,
                          conf_conv_w, conf_conv_b, conf_ln_w, conf_ln_b, w_o_conf, w_out,
                          norm_ffn_w, w_up, ffn_conv_w, ffn_conv_b, w_down, norm_final_w)
    xp = x_prompt
    xs = x_sample.reshape(nb_s, D_MODEL)
    new_p = ([], [], [], [])
    new_s = ([], [], [])
    delta_s = None
    for l in range(depth):
        final = l == depth - 1
        q, k, v, z, sga, mb, bgc, gct, n_qkv, n_conf = _mixer_in_prompt(xp, pw, l)
        o, n_delta = _delta_prompt(q, k, v, bgc, gct)
        xp, n_ffn = _mixer_out_ffn_prompt(o, z, sga, mb, xp, pw, l, final)
        for lst, val in zip(new_p, (n_delta, n_qkv, n_conf, n_ffn)):
            lst.append(val)
        q, k, v, z, sga, mb, ba, n_qkv, n_conf = _mixer_in_sample(
            xs, state_qkv_conv[l].reshape(nb_s, -1), state_conf_conv[l].reshape(nb_s, -1), pw, l)
        heads = lambda a: a.reshape(nb_s, N_HEADS, HEAD_DIM)
        o, delta_s = _delta_sample(ba[:, :N_HEADS].reshape(-1), ba[:, N_HEADS:2 * N_HEADS].reshape(-1),
                                   heads(q).transpose(0, 2, 1), heads(k).transpose(0, 2, 1), heads(v),
                                   state_delta, l, delta_s)
        xs, n_ffn = _mixer_out_ffn_sample(o.reshape(nb_s, D_MODEL), z, sga, mb, xs,
                                          state_ffn_conv[l].reshape(nb_s, -1), pw, l, final)
        for lst, val in zip(new_s, (n_qkv.reshape(state_qkv_conv.shape[1:]),
                                    n_conf.reshape(state_conf_conv.shape[1:]),
                                    n_ffn.reshape(state_ffn_conv.shape[1:]))):
            lst.append(val)
    return (xp, xs.reshape(x_sample.shape),
            jnp.stack(new_p[0]), jnp.stack(new_p[1]), jnp.stack(new_p[2]), jnp.stack(new_p[3]),
            delta_s, jnp.stack(new_s[0]), jnp.stack(new_s[1]), jnp.stack(new_s[2]))
```

```python
import functools

import jax
import jax.numpy as jnp
from jax import lax
from jax.experimental import pallas as pl
from jax.experimental.pallas import tpu as pltpu

F32 = jnp.float32
BF16 = jnp.bfloat16

D_MODEL = 1024
N_HEADS = 8
HEAD_DIM = 128
QKV_DIM = 3 * N_HEADS * HEAD_DIM
SHORT_CONV = 4
CONF_DIM = D_MODEL // 2
CONF_CONV = 31
D_FF = 2816
FFN_CONV = 3
NORM_EPS = 1e-6
Q_SCALE = HEAD_DIM ** -0.5

LANES = 128
SUBLANES = 8
MXU_COLS = 256
CHUNK = 128
SLABS = CHUNK // SUBLANES
DELTA_CHUNKS = 4
TILE_M_IN = 512
TILE_M = 256
QKV_TAIL = SUBLANES * (SHORT_CONV - 1)
FFN_HIST = SUBLANES
SAMPLE_BATCH_TILE = 16
VMEM_LIMIT = 60000 * 1024


def _mm(a, b):
    return jnp.dot(a.astype(BF16), b.astype(BF16), preferred_element_type=F32)


def _mm_nt(a, b):
    return lax.dot_general(a.astype(BF16), b.astype(BF16), (((1,), (1,)), ((), ())),
                           preferred_element_type=F32)


def _mm_tn(a, b):
    return lax.dot_general(a.astype(BF16), b.astype(BF16), (((0,), (0,)), ((), ())),
                           preferred_element_type=F32)


def _split3(x):
    hi = x.astype(BF16)
    r1 = x - hi.astype(F32)
    mid = r1.astype(BF16)
    lo = (r1 - mid.astype(F32)).astype(BF16)
    return hi, mid, lo


def _silu(x):
    return x * jax.nn.sigmoid(x)


def _softplus(x):
    return jnp.maximum(x, 0.0) + jnp.log1p(jnp.exp(-jnp.abs(x)))


def _rmsnorm(x, w):
    return x * lax.rsqrt(jnp.mean(x * x, axis=-1, keepdims=True) + NORM_EPS) * w


def _layernorm(x, w, b):
    mu = jnp.mean(x, axis=-1, keepdims=True)
    xc = x - mu
    return xc * lax.rsqrt(jnp.mean(xc * xc, axis=-1, keepdims=True) + NORM_EPS) * w + b


def _token_of_row(r):
    w = r % CHUNK
    return (r // CHUNK) * CHUNK + (w % SUBLANES) * SLABS + w // SUBLANES


def _row_of_token(tok):
    return (tok % SLABS) * SUBLANES + tok // SLABS


def _perm_matrix(n, inverse):
    i = lax.broadcasted_iota(jnp.int32, (n, n), 0)
    j = lax.broadcasted_iota(jnp.int32, (n, n), 1)
    hit = (i == _token_of_row(j)) if inverse else (j == _token_of_row(i))
    return jnp.where(hit, 1.0, 0.0).astype(BF16)


def _chunk_tri(n, upper):
    r = _token_of_row(lax.broadcasted_iota(jnp.int32, (n, n), 0))
    c = _token_of_row(lax.broadcasted_iota(jnp.int32, (n, n), 1))
    same = (r // CHUNK) == (c // CHUNK)
    tri = (r <= c) if upper else (r >= c)
    return jnp.where(same & tri, 1.0, 0.0).astype(BF16)


def _sublane_shifted(cur, prev, e):
    n = cur.shape[0]
    sub = lax.broadcasted_iota(jnp.int32, cur.shape, 0) % SUBLANES
    return jnp.where(sub >= e, pltpu.roll(cur, e, 0), pltpu.roll(prev, n - SUBLANES + e, 0))


def _delayed(zs, d):
    a, b = divmod(d, SLABS)
    if b == 0:
        return zs[a]
    cut = CHUNK - SUBLANES * b
    return jnp.concatenate([zs[a + 1][cut:], zs[a][:cut]], axis=0)


def _qkv_post(y, c):
    y = _silu(y)
    if c < 2 * N_HEADS:
        y = y * lax.rsqrt(jnp.sum(y * y, axis=-1, keepdims=True) + NORM_EPS)
        if c < N_HEADS:
            y = y * Q_SCALE
    return y


def _beta_decay(ba, alog, dtb):
    lane = lax.broadcasted_iota(jnp.int32, ba.shape, 1)
    beta = jnp.where(lane < N_HEADS, jax.nn.sigmoid(ba), 0.0)
    g = -jnp.exp(alog) * _softplus(ba + dtb)
    g = jnp.where((lane >= N_HEADS) & (lane < 2 * N_HEADS), g, 0.0)
    return beta, g


def _mixer_in_prompt_kernel(x_ref, perm_ref, tril_ref, triu_ref, nw_ref, wqkv_ref, wz_ref, wba_ref, wat_ref, wglu_ref, wga_ref, wgb_ref,
                            cw_ref, alog_ref, dtb_ref, alogc_ref, dtbc_ref,
                            ccw_ref, ccb_ref, lnw_ref, lnb_ref, woc_ref,
                            q_ref, k_ref, v_ref, z_ref, sga_ref, mb_ref, bgc_ref, gct_ref,
                            nqkv_ref, nconf_ref,
                            cc_buf, sgb_buf, qkv_carry, u_carry, *, tm):
    t = pl.program_id(1)

    @pl.when(t == 0)
    def _():
        qkv_carry[...] = jnp.zeros(qkv_carry.shape, F32)
        u_carry[...] = jnp.zeros(u_carry.shape, F32)

    xh = _rmsnorm(x_ref[...], nw_ref[...]).astype(BF16)
    chunks = [slice(r0, r0 + CHUNK) for r0 in range(0, tm, CHUNK)]
    h = jnp.concatenate([jnp.dot(perm_ref[...], xh[ch], preferred_element_type=F32) for ch in chunks],
                        axis=0).astype(BF16)
    slabs_per_group = MXU_COLS // LANES

    for g in range(CONF_DIM // MXU_COLS):
        gs = slice(g * MXU_COLS, (g + 1) * MXU_COLS)
        w_a = wglu_ref[:, gs]
        w_b = wglu_ref[:, CONF_DIM + g * MXU_COLS:CONF_DIM + (g + 1) * MXU_COLS]
        u = None
        for ci in range(tm // CHUNK):
            r0 = ci * CHUNK
            u_prev = u
            u = (jnp.dot(h[r0:r0 + CHUNK], w_a, preferred_element_type=F32)
                 * jax.nn.sigmoid(jnp.dot(h[r0:r0 + CHUNK], w_b, preferred_element_type=F32)))
            for j in range(slabs_per_group):
                cs = slice(g * MXU_COLS + j * LANES, g * MXU_COLS + (j + 1) * LANES)
                js = slice(j * LANES, (j + 1) * LANES)
                cur = u[:, js]
                prev = u_carry[:, cs] if ci == 0 else u_prev[:, js]
                zs = (cur, _sublane_shifted(cur, prev, 1), _sublane_shifted(cur, prev, 2))
                acc = cur * ccw_ref[CONF_CONV - 1:CONF_CONV, cs]
                for d in range(1, CONF_CONV):
                    acc = acc + _delayed(zs, d) * ccw_ref[CONF_CONV - 1 - d:CONF_CONV - d, cs]
                cc_buf[r0:r0 + CHUNK, cs] = acc + ccb_ref[:, cs]
        for i in range(CONF_CONV - 1):
            r = _row_of_token(CHUNK - (CONF_CONV - 1) + i)
            nconf_ref[i:i + 1, gs] = u[r:r + 1, :]
        u_carry[:, gs] = u
    cact = _silu(_layernorm(cc_buf[...], lnw_ref[...], lnb_ref[...])).astype(BF16)

    for g in range(QKV_DIM // MXU_COLS):
        gs = slice(g * MXU_COLS, (g + 1) * MXU_COLS)
        w_g = wqkv_ref[:, gs]
        tail = None
        for ci in range(tm // CHUNK):
            r0 = ci * CHUNK
            proj = jnp.dot(h[r0:r0 + CHUNK], w_g, preferred_element_type=F32)
            for j in range(slabs_per_group):
                c = g * slabs_per_group + j
                cs = slice(c * LANES, (c + 1) * LANES)
                js = slice(j * LANES, (j + 1) * LANES)
                cur = proj[:, js]
                prev_tail = qkv_carry[:, cs] if ci == 0 else tail[:, js]
                z1 = jnp.concatenate([cur[:CHUNK - QKV_TAIL],
                                      _sublane_shifted(cur[CHUNK - QKV_TAIL:], prev_tail, 1)], axis=0)
                acc = cur * cw_ref[SHORT_CONV - 1:SHORT_CONV, cs]
                for d in range(1, SHORT_CONV):
                    acc = acc + _delayed((cur, z1), d) * cw_ref[SHORT_CONV - 1 - d:SHORT_CONV - d, cs]
                y = _qkv_post(acc, c)
                y = y.astype(BF16)
                if c < N_HEADS:
                    q_ref[r0:r0 + CHUNK, cs] = y
                elif c < 2 * N_HEADS:
                    k_ref[r0:r0 + CHUNK, (c - N_HEADS) * LANES:(c - N_HEADS + 1) * LANES] = y
                else:
                    v_ref[r0:r0 + CHUNK, (c - 2 * N_HEADS) * LANES:(c - 2 * N_HEADS + 1) * LANES] = y
            tail = proj[CHUNK - QKV_TAIL:, :]
        for i in range(SHORT_CONV - 1):
            r = _row_of_token(CHUNK - (SHORT_CONV - 1) + i)
            nqkv_ref[i:i + 1, gs] = proj[r:r + 1, :]
        qkv_carry[:, gs] = tail

    for g in range(D_MODEL // MXU_COLS):
        gs = slice(g * MXU_COLS, (g + 1) * MXU_COLS)
        w_g = wz_ref[:, gs]
        for ch in chunks:
            z_ref[ch, gs] = jnp.dot(h[ch], w_g, preferred_element_type=F32).astype(BF16)

    ba = jnp.dot(h, wba_ref[...], preferred_element_type=F32)
    beta, g_tok = _beta_decay(ba, alog_ref[...], dtb_ref[...])
    at = _mm_nt(wat_ref[...], h)
    g_head = -jnp.exp(alogc_ref[...]) * _softplus(at + dtbc_ref[...])
    for ch in chunks:
        gc = sum(jnp.dot(tril_ref[...], part, preferred_element_type=F32) for part in _split3(g_tok[ch]))
        bgc_ref[ch, :] = beta[ch] + gc
        gct_ref[:, ch] = sum(jnp.dot(part, triu_ref[...], preferred_element_type=F32) for part in _split3(g_head[:, ch]))

    for g in range(D_MODEL // MXU_COLS):
        gs = slice(g * MXU_COLS, (g + 1) * MXU_COLS)
        w_a = wga_ref[:, gs]
        w_b = wgb_ref[:, gs]
        for ch in chunks:
            sga_ref[ch, gs] = jax.nn.sigmoid(jnp.dot(h[ch], w_a, preferred_element_type=F32)).astype(BF16)
            sgb_buf[ch, gs] = jax.nn.sigmoid(jnp.dot(h[ch], w_b, preferred_element_type=F32))

    mb_ref[...] = (sgb_buf[...] * jnp.dot(cact, woc_ref[...], preferred_element_type=F32)).astype(BF16)


def _delta_masks():
    row = lax.broadcasted_iota(jnp.int32, (CHUNK, CHUNK), 0)
    col = lax.broadcasted_iota(jnp.int32, (CHUNK, CHUNK), 1)
    trow = _token_of_row(row)
    tcol = _token_of_row(col)
    masks = [trow >= tcol, trow > tcol, row == col]
    s_blk = 1
    while s_blk < CHUNK:
        masks.append(((trow // (2 * s_blk)) == (tcol // (2 * s_blk)))
                     & ((trow // s_blk) % 2 == 1) & ((tcol // s_blk) % 2 == 0))
        s_blk *= 2
    return jnp.stack(masks).astype(F32)


def _delta_chunk_kernel(q_ref, k_ref, v_ref, bgc_ref, gct_ref, mask_ref, o_ref, s_ref, *, nc):
    t = pl.program_id(1)

    @pl.when(t == 0)
    def _():
        s_ref[...] = jnp.zeros(s_ref.shape, F32)

    causal, strict, eye = 0, 1, 2
    sub_diag = {1 << i: 3 + i for i in range(CHUNK.bit_length() - 1)}
    heads = range(N_HEADS)
    units = [(ci, h) for ci in range(nc) for h in heads]

    def rows(ci):
        return slice(ci * CHUNK, (ci + 1) * CHUNK)

    def cols(h):
        return slice(h * HEAD_DIM, (h + 1) * HEAD_DIM)

    r_last = _row_of_token(CHUNK - 1)
    beta = {(ci, h): bgc_ref[rows(ci), h:h + 1] for ci, h in units}
    gc = {(ci, h): bgc_ref[rows(ci), N_HEADS + h:N_HEADS + h + 1] for ci, h in units}
    glast = {(ci, h): gct_ref[h:h + 1, ci * CHUNK + r_last:ci * CHUNK + r_last + 1]
             for ci, h in units}

    low, qk, rhs = {}, {}, {}
    for u in units:
        ci, h = u
        k = k_ref[rows(ci), cols(h)].astype(F32)
        kb = k * beta[u]
        decay = jnp.exp((gc[u] - gct_ref[h:h + 1, rows(ci)]) * mask_ref[causal]) * mask_ref[causal]
        kq = _mm_nt(jnp.concatenate([kb, q_ref[rows(ci), cols(h)].astype(F32)], axis=0), k)
        low[u] = kq[:CHUNK] * decay * mask_ref[strict]
        qk[u] = (kq[CHUNK:] * decay).astype(BF16)
        rhs[u] = jnp.concatenate([v_ref[rows(ci), cols(h)].astype(F32) * beta[u], kb * jnp.exp(gc[u])],
                                 axis=1).astype(BF16)

    tinv = {u: mask_ref[eye] - low[u] * mask_ref[sub_diag[1]] for u in units}
    s_blk = 2
    while s_blk < CHUNK:
        t16 = {u: tinv[u].astype(BF16) for u in units}
        m1 = {u: jnp.dot((low[u] * mask_ref[sub_diag[s_blk]]).astype(BF16), t16[u],
                         preferred_element_type=F32) for u in units}
        tinv = {u: tinv[u] - jnp.dot(t16[u], m1[u].astype(BF16), preferred_element_type=F32) for u in units}
        s_blk *= 2
    y = {u: jnp.dot(tinv[u].astype(BF16), rhs[u], preferred_element_type=F32) for u in units}

    state = [s_ref[h] for h in heads]
    for ci in range(nc):
        ws_qs = [_mm(jnp.concatenate([y[ci, h][:, HEAD_DIM:],
                                      q_ref[rows(ci), cols(h)].astype(F32) * jnp.exp(gc[ci, h])], axis=0), state[h])
                 for h in heads]
        v_new = [(y[ci, h][:, :HEAD_DIM] - ws_qs[h][:CHUNK]).astype(BF16) for h in heads]
        for h in heads:
            o_ref[rows(ci), cols(h)] = (ws_qs[h][CHUNK:] + jnp.dot(qk[ci, h], v_new[h], preferred_element_type=F32)
                                        ).astype(o_ref.dtype)
        state = [state[h] * jnp.exp(glast[ci, h])
                 + _mm_tn(k_ref[rows(ci), cols(h)].astype(F32) * jnp.exp(glast[ci, h] - gc[ci, h]), v_new[h])
                 for h in heads]
    for h in heads:
        s_ref[h] = state[h]


def _delta_out_norm(o_ref, z_ref, dnw):
    parts = []
    for h in range(N_HEADS):
        hs = slice(h * HEAD_DIM, (h + 1) * HEAD_DIM)
        o = o_ref[:, hs].astype(F32)
        on = o * lax.rsqrt(jnp.mean(o * o, axis=-1, keepdims=True) + NORM_EPS) * dnw
        parts.append((on * _silu(z_ref[:, hs].astype(F32))).astype(BF16))
    return jnp.concatenate(parts, axis=1)


def _mixer_out_ffn_prompt_kernel(o_ref, z_ref, sga_ref, mb_ref, x_ref, unperm_ref, dnw_ref, wod_ref, wout_ref,
                                 nfw_ref, wup_ref, fcw_ref, fcb_ref, wdown_ref, finw_ref,
                                 out_ref, nffn_ref, up_buf, act_buf, *, tm, final):
    t = pl.program_id(1)

    @pl.when(t == 0)
    def _():
        up_buf[0:FFN_HIST, :] = jnp.zeros((FFN_HIST, 2 * D_FF), F32)

    on = _delta_out_norm(o_ref, z_ref, dnw_ref[...])
    ya = jnp.dot(on, wod_ref[...], preferred_element_type=F32)
    merged = (sga_ref[...].astype(F32) * ya + mb_ref[...].astype(F32)).astype(BF16)
    merged = jnp.concatenate([jnp.dot(unperm_ref[...], merged[r0:r0 + CHUNK], preferred_element_type=F32)
                              for r0 in range(0, tm, CHUNK)], axis=0).astype(BF16)
    x1 = x_ref[...] + jnp.dot(merged, wout_ref[...], preferred_element_type=F32)

    h2 = _rmsnorm(x1, nfw_ref[...]).astype(BF16)
    up_buf[FFN_HIST:FFN_HIST + tm, :] = jnp.dot(h2, wup_ref[...], preferred_element_type=F32)
    nffn_ref[...] = up_buf[FFN_HIST + tm - (FFN_CONV - 1):FFN_HIST + tm, :]
    base = FFN_HIST - (FFN_CONV - 1)
    for c in range(D_FF // LANES):
        acts = []
        for off in (0, D_FF):
            cs = slice(off + c * LANES, off + (c + 1) * LANES)
            acc = up_buf[base:base + tm, cs] * fcw_ref[0:1, cs]
            for j in range(1, FFN_CONV):
                acc = acc + up_buf[base + j:base + j + tm, cs] * fcw_ref[j:j + 1, cs]
            acts.append(acc + fcb_ref[:, cs])
        act_buf[:, c * LANES:(c + 1) * LANES] = (_silu(acts[0]) * acts[1]).astype(BF16)
    up_buf[0:FFN_HIST, :] = up_buf[tm:tm + FFN_HIST, :]
    x2 = x1 + jnp.dot(act_buf[...], wdown_ref[...], preferred_element_type=F32)
    out_ref[...] = _rmsnorm(x2, finw_ref[...]) if final else x2


def _mixer_in_sample_kernel(x_ref, sqkv_ref, sconf_ref, nw_ref, wqkv_ref, wz_ref, wba_ref, wglu_ref, wga_ref, wgb_ref,
                            cw_ref, alog_ref, dtb_ref, ccw_ref, ccb_ref, lnw_ref, lnb_ref, woc_ref,
                            q_ref, k_ref, v_ref, z_ref, sga_ref, mb_ref, ba_ref, nqkv_ref, nconf_ref):
    h = _rmsnorm(x_ref[...], nw_ref[...]).astype(BF16)
    hist = SHORT_CONV - 1
    for c in range(QKV_DIM // LANES):
        cs = slice(c * LANES, (c + 1) * LANES)
        new = jnp.dot(h, wqkv_ref[:, cs], preferred_element_type=F32)
        acc = new * cw_ref[hist:hist + 1, cs]
        for j in range(hist):
            row = sqkv_ref[:, j * QKV_DIM + c * LANES:j * QKV_DIM + (c + 1) * LANES]
            acc = acc + row * cw_ref[j:j + 1, cs]
            if j > 0:
                nqkv_ref[:, (j - 1) * QKV_DIM + c * LANES:(j - 1) * QKV_DIM + (c + 1) * LANES] = row
        nqkv_ref[:, (hist - 1) * QKV_DIM + c * LANES:(hist - 1) * QKV_DIM + (c + 1) * LANES] = new
        y = _qkv_post(acc, c)
        if c < N_HEADS:
            q_ref[:, cs] = y
        elif c < 2 * N_HEADS:
            k_ref[:, (c - N_HEADS) * LANES:(c - N_HEADS + 1) * LANES] = y
        else:
            v_ref[:, (c - 2 * N_HEADS) * LANES:(c - 2 * N_HEADS + 1) * LANES] = y

    z_ref[...] = jnp.dot(h, wz_ref[...], preferred_element_type=F32)
    ba = jnp.dot(h, wba_ref[...], preferred_element_type=F32)
    beta, g = _beta_decay(ba, alog_ref[...], dtb_ref[...])
    lane = lax.broadcasted_iota(jnp.int32, ba.shape, 1)
    ba_ref[...] = jnp.where(lane < N_HEADS, beta, jnp.exp(g))

    gl = jnp.dot(h, wglu_ref[...], preferred_element_type=F32)
    u = gl[:, :CONF_DIM] * jax.nn.sigmoid(gl[:, CONF_DIM:])
    chist = CONF_CONV - 1
    acc = u * ccw_ref[chist:chist + 1, :]
    for j in range(chist):
        row = sconf_ref[:, j * CONF_DIM:(j + 1) * CONF_DIM]
        acc = acc + row * ccw_ref[j:j + 1, :]
        if j > 0:
            nconf_ref[:, (j - 1) * CONF_DIM:j * CONF_DIM] = row
    nconf_ref[:, (chist - 1) * CONF_DIM:chist * CONF_DIM] = u
    cact = _silu(_layernorm(acc + ccb_ref[...], lnw_ref[...], lnb_ref[...]))
    yb = _mm(cact, woc_ref[...])
    sga_ref[...] = jax.nn.sigmoid(jnp.dot(h, wga_ref[...], preferred_element_type=F32))
    mb_ref[...] = jax.nn.sigmoid(jnp.dot(h, wgb_ref[...], preferred_element_type=F32)) * yb


def _delta_step_kernel(beta_ref, a_ref, qt_ref, kt_ref, v_ref, s_ref, *rest, bt):
    o_ref, snew_ref = rest[-2:]
    b0 = pl.program_id(1) * bt

    @pl.when(pl.program_id(0) > 0)
    def _():
        snew_ref[...] = jnp.zeros(snew_ref.shape, F32)

    @pl.when(pl.program_id(0) == 0)
    def _():
        for i in range(bt):
            kt = kt_ref[i]
            qt = qt_ref[i]
            for h in range(N_HEADS):
                beta = beta_ref[(b0 + i) * N_HEADS + h]
                a = a_ref[(b0 + i) * N_HEADS + h]
                s = s_ref[i, h]
                kcol = kt[:, h:h + 1]
                r = jnp.sum(s * kcol, axis=0, keepdims=True)
                d = beta * (v_ref[i, h:h + 1, :] - a * r)
                sn = a * s + kcol * d
                snew_ref[i, h] = sn
                o_ref[i, h:h + 1, :] = jnp.sum(sn * qt[:, h:h + 1], axis=0, keepdims=True)


def _mixer_out_ffn_sample_kernel(o_ref, z_ref, sga_ref, mb_ref, x_ref, sffn_ref, dnw_ref, wod_ref, wout_ref,
                                 nfw_ref, wup_ref, fcw_ref, fcb_ref, wdown_ref, finw_ref,
                                 out_ref, nffn_ref, act_buf, *, final):
    on = _delta_out_norm(o_ref, z_ref, dnw_ref[...])
    ya = jnp.dot(on, wod_ref[...], preferred_element_type=F32)
    merged = sga_ref[...] * ya + mb_ref[...]
    x1 = x_ref[...] + _mm(merged, wout_ref[...])
    h2 = _rmsnorm(x1, nfw_ref[...]).astype(BF16)
    hist = FFN_CONV - 1
    for c in range(D_FF // LANES):
        acts = []
        for off in (0, D_FF):
            cs = slice(off + c * LANES, off + (c + 1) * LANES)
            new = jnp.dot(h2, wup_ref[:, cs], preferred_element_type=F32)
            acc = new * fcw_ref[hist:hist + 1, cs]
            for j in range(hist):
                row = sffn_ref[:, j * 2 * D_FF + off + c * LANES:j * 2 * D_FF + off + (c + 1) * LANES]
                acc = acc + row * fcw_ref[j:j + 1, cs]
                if j > 0:
                    nffn_ref[:, (j - 1) * 2 * D_FF + off + c * LANES:(j - 1) * 2 * D_FF + off + (c + 1) * LANES] = row
            nffn_ref[:, (hist - 1) * 2 * D_FF + off + c * LANES:(hist - 1) * 2 * D_FF + off + (c + 1) * LANES] = new
            acts.append(acc + fcb_ref[:, cs])
        act_buf[:, c * LANES:(c + 1) * LANES] = (_silu(acts[0]) * acts[1]).astype(BF16)
    x2 = x1 + jnp.dot(act_buf[...], wdown_ref[...], preferred_element_type=F32)
    out_ref[...] = _rmsnorm(x2, finw_ref[...]) if final else x2


def _params(n_grid):
    return pltpu.CompilerParams(dimension_semantics=("arbitrary",) * n_grid, vmem_limit_bytes=VMEM_LIMIT)


def _layer_block(arr, l, width=None, blk=0):
    rows, cols = arr.shape[1:]
    return pl.BlockSpec((None, rows, cols if width is None else width), lambda *_: (l, 0, blk),
                        pipeline_mode=pl.Buffered(1))


def _whole(arr):
    nd = len(arr.shape)
    return pl.BlockSpec(arr.shape, lambda *_: (0,) * nd, pipeline_mode=pl.Buffered(1))


def _mixer_in_consts(pw, l, head_major):
    names = ["norm_mix_w", ("w_in", QKV_DIM, 0), ("w_in", D_MODEL, QKV_DIM // D_MODEL), "w_ba"]
    names += ["w_at"] if head_major else []
    names += [("w_tail", D_MODEL, 0), ("w_tail", D_MODEL, 1), ("w_tail", D_MODEL, 2),
              "conv_qkv_w", "alog_row", "dtb_row"]
    names += ["alog_col", "dtb_col"] if head_major else []
    names += ["conf_conv_w", "conf_conv_b", "conf_ln_w", "conf_ln_b", "w_o_conf"]
    out = []
    for n in names:
        if isinstance(n, tuple):
            out.append((pw[n[0]], _layer_block(pw[n[0]], l, n[1], n[2])))
        else:
            out.append((pw[n], _layer_block(pw[n], l)))
    return out


def _mixer_out_consts(pw, l):
    names = ("delta_norm_w", "w_o_delta", "w_out", "norm_ffn_w", "w_up", "ffn_conv_w", "ffn_conv_b", "w_down")
    return [(pw[n], _layer_block(pw[n], l)) for n in names] + [(pw["norm_final_w"], _layer_block(pw["norm_final_w"], 0))]


def _mixer_in_prompt(x, pw, l):
    nb, seq, _ = x.shape
    tm = TILE_M_IN
    assert seq % tm == 0 and tm % CHUNK == 0
    tok = lambda n: pl.BlockSpec((None, tm, n), lambda b, t: (b, t, 0))
    consts = _mixer_in_consts(pw, l, head_major=True)
    mats = [_perm_matrix(CHUNK, inverse=False), _chunk_tri(CHUNK, upper=False), _chunk_tri(CHUNK, upper=True)]
    act = jax.ShapeDtypeStruct((nb, seq, D_MODEL), BF16)
    return pl.pallas_call(
        functools.partial(_mixer_in_prompt_kernel, tm=tm),
        grid=(nb, seq // tm),
        in_specs=[tok(D_MODEL)] + [_whole(m) for m in mats] + [c[1] for c in consts],
        out_specs=[tok(D_MODEL)] * 6 + [
            tok(LANES),
            pl.BlockSpec((None, N_HEADS, tm), lambda b, t: (b, 0, t)),
            pl.BlockSpec((None, SHORT_CONV - 1, QKV_DIM), lambda b, t: (b, 0, 0)),
            pl.BlockSpec((None, CONF_CONV - 1, CONF_DIM), lambda b, t: (b, 0, 0)),
        ],
        out_shape=[act] * 6 + [
            jax.ShapeDtypeStruct((nb, seq, LANES), F32),
            jax.ShapeDtypeStruct((nb, N_HEADS, seq), F32),
            jax.ShapeDtypeStruct((nb, SHORT_CONV - 1, QKV_DIM), F32),
            jax.ShapeDtypeStruct((nb, CONF_CONV - 1, CONF_DIM), F32),
        ],
        scratch_shapes=[pltpu.VMEM((tm, CONF_DIM), F32),
                        pltpu.VMEM((tm, D_MODEL), F32),
                        pltpu.VMEM((QKV_TAIL, QKV_DIM), F32),
                        pltpu.VMEM((CHUNK, CONF_DIM), F32)],
        compiler_params=_params(2),
        name="mixer_in_prompt",
    )(x, *mats, *[c[0] for c in consts])


def _delta_prompt(q, k, v, bgc, gct):
    nb, seq, _ = q.shape
    tm = DELTA_CHUNKS * CHUNK
    assert seq % tm == 0
    tok = lambda n: pl.BlockSpec((None, tm, n), lambda b, t: (b, t, 0))
    masks = _delta_masks()
    return pl.pallas_call(
        functools.partial(_delta_chunk_kernel, nc=DELTA_CHUNKS),
        grid=(nb, seq // tm),
        in_specs=[tok(D_MODEL)] * 3 + [tok(LANES), pl.BlockSpec((None, N_HEADS, tm), lambda b, t: (b, 0, t)),
                  _whole(masks)],
        out_specs=[tok(D_MODEL),
                   pl.BlockSpec((None, N_HEADS, HEAD_DIM, HEAD_DIM), lambda b, t: (b, 0, 0, 0))],
        out_shape=[jax.ShapeDtypeStruct((nb, seq, D_MODEL), BF16),
                   jax.ShapeDtypeStruct((nb, N_HEADS, HEAD_DIM, HEAD_DIM), F32)],
        compiler_params=_params(2),
        name="delta_prompt",
    )(q, k, v, bgc, gct, masks)


def _mixer_out_ffn_prompt(o, z, sga, mb, x, pw, l, final):
    nb, seq, _ = x.shape
    tm = TILE_M
    assert seq % tm == 0 and tm % CHUNK == 0
    tok = pl.BlockSpec((None, tm, D_MODEL), lambda b, t: (b, t, 0))
    consts = _mixer_out_consts(pw, l)
    unperm = _perm_matrix(CHUNK, inverse=True)
    return pl.pallas_call(
        functools.partial(_mixer_out_ffn_prompt_kernel, tm=tm, final=final),
        grid=(nb, seq // tm),
        in_specs=[tok] * 5 + [_whole(unperm)] + [c[1] for c in consts],
        out_specs=[tok, pl.BlockSpec((None, FFN_CONV - 1, 2 * D_FF), lambda b, t: (b, 0, 0))],
        out_shape=[jax.ShapeDtypeStruct((nb, seq, D_MODEL), F32),
                   jax.ShapeDtypeStruct((nb, FFN_CONV - 1, 2 * D_FF), F32)],
        scratch_shapes=[pltpu.VMEM((FFN_HIST + tm, 2 * D_FF), F32), pltpu.VMEM((tm, D_FF), BF16)],
        compiler_params=_params(2),
        name="mixer_out_ffn_prompt",
    )(o, z, sga, mb, x, unperm, *[c[0] for c in consts])


def _mixer_in_sample(x, sqkv, sconf, pw, l):
    nb = x.shape[0]
    consts = _mixer_in_consts(pw, l, head_major=False)
    act = jax.ShapeDtypeStruct((nb, D_MODEL), F32)
    outs = [act] * 6 + [jax.ShapeDtypeStruct((nb, LANES), F32),
                        jax.ShapeDtypeStruct(sqkv.shape, F32), jax.ShapeDtypeStruct(sconf.shape, F32)]
    return pl.pallas_call(
        _mixer_in_sample_kernel,
        grid=(1,),
        in_specs=[_whole(a) for a in (x, sqkv, sconf)] + [c[1] for c in consts],
        out_specs=[_whole(a) for a in outs],
        out_shape=outs,
        compiler_params=_params(1),
        name="mixer_in_sample",
    )(x, sqkv, sconf, *[c[0] for c in consts])


def _delta_sample(beta, a, qt, kt, v, states, l, new_states):
    depth, nb = states.shape[:2]
    bt = SAMPLE_BATCH_TILE
    assert nb % bt == 0
    n_bt = nb // bt
    first = new_states is None
    assert first == (l == 0)
    smem = pl.BlockSpec(memory_space=pltpu.SMEM)

    def tile(j, i):
        return jnp.where(j == 0, i, n_bt - 1)

    args = [beta, a, qt, kt, v, states] + ([] if first else [new_states])
    return pl.pallas_call(
        functools.partial(_delta_step_kernel, bt=bt),
        grid=(depth if first else 1, n_bt),
        in_specs=[smem, smem,
                  pl.BlockSpec((bt, HEAD_DIM, N_HEADS), lambda j, i: (tile(j, i), 0, 0)),
                  pl.BlockSpec((bt, HEAD_DIM, N_HEADS), lambda j, i: (tile(j, i), 0, 0)),
                  pl.BlockSpec((bt, N_HEADS, HEAD_DIM), lambda j, i: (tile(j, i), 0, 0)),
                  pl.BlockSpec((None, bt, N_HEADS, HEAD_DIM, HEAD_DIM), lambda j, i: (l, tile(j, i), 0, 0, 0))]
                 + ([] if first else [pl.BlockSpec(memory_space=pl.ANY)]),
        out_specs=[pl.BlockSpec((bt, N_HEADS, HEAD_DIM), lambda j, i: (tile(j, i), 0, 0)),
                   pl.BlockSpec((None, bt, N_HEADS, HEAD_DIM, HEAD_DIM), lambda j, i: (l + j, i, 0, 0, 0))],
        out_shape=[jax.ShapeDtypeStruct((nb, N_HEADS, HEAD_DIM), F32),
                   jax.ShapeDtypeStruct(states.shape, F32)],
        input_output_aliases={} if first else {len(args) - 1: 1},
        compiler_params=_params(2),
        name="delta_sample",
    )(*args)


def _mixer_out_ffn_sample(o, z, sga, mb, x, sffn, pw, l, final):
    nb = x.shape[0]
    consts = _mixer_out_consts(pw, l)
    outs = [jax.ShapeDtypeStruct((nb, D_MODEL), F32), jax.ShapeDtypeStruct(sffn.shape, F32)]
    return pl.pallas_call(
        functools.partial(_mixer_out_ffn_sample_kernel, final=final),
        grid=(1,),
        in_specs=[_whole(a) for a in (o, z, sga, mb, x, sffn)] + [c[1] for c in consts],
        out_specs=[_whole(a) for a in outs],
        out_shape=outs,
        scratch_shapes=[pltpu.VMEM((nb, D_FF), BF16)],
        compiler_params=_params(1),
        name="mixer_out_ffn_sample",
    )(o, z, sga, mb, x, sffn, *[c[0] for c in consts])


def _prepare_weights(norm_mix_w, w_in, conv_qkv_w, a_log, dt_bias, delta_norm_w, w_o_delta,
                     conf_conv_w, conf_conv_b, conf_ln_w, conf_ln_b, w_o_conf, w_out,
                     norm_ffn_w, w_up, ffn_conv_w, ffn_conv_b, w_down, norm_final_w):
    depth = w_in.shape[0]
    w16 = w_in.astype(BF16)
    o_b = QKV_DIM + N_HEADS * HEAD_DIM
    o_a = o_b + N_HEADS
    o_glu = o_a + N_HEADS
    pad = LANES - 2 * N_HEADS
    rows = lambda p: p.reshape(depth, 1, -1)
    lane_pad = lambda p: jnp.pad(p.reshape(depth, 1, N_HEADS), ((0, 0), (0, 0), (N_HEADS, pad)))
    return {
        "norm_mix_w": rows(norm_mix_w),
        "w_in": w16,
        "w_tail": w16[:, :, o_glu:],
        "w_ba": jnp.pad(w16[:, :, o_b:o_glu], ((0, 0), (0, 0), (0, pad))),
        "w_at": jnp.swapaxes(w16[:, :, o_a:o_glu], 1, 2),
        "conv_qkv_w": conv_qkv_w,
        "alog_row": lane_pad(a_log), "dtb_row": lane_pad(dt_bias),
        "alog_col": a_log.reshape(depth, N_HEADS, 1), "dtb_col": dt_bias.reshape(depth, N_HEADS, 1),
        "conf_conv_w": conf_conv_w, "conf_conv_b": rows(conf_conv_b),
        "conf_ln_w": rows(conf_ln_w), "conf_ln_b": rows(conf_ln_b),
        "w_o_conf": w_o_conf.astype(BF16),
        "delta_norm_w": rows(delta_norm_w),
        "w_o_delta": w_o_delta.astype(BF16),
        "w_out": w_out.astype(BF16),
        "norm_ffn_w": rows(norm_ffn_w),
        "w_up": w_up.astype(BF16),
        "ffn_conv_w": ffn_conv_w, "ffn_conv_b": rows(ffn_conv_b),
        "w_down": w_down.astype(BF16),
        "norm_final_w": norm_final_w.reshape(1, 1, -1),
    }


def kernel(x_prompt, x_sample, state_delta, state_qkv_conv, state_conf_conv, state_ffn_conv, norm_mix_w, w_in, conv_qkv_w, a_log, dt_bias, delta_norm_w, w_o_delta, conf_conv_w, conf_conv_b, conf_ln_w, conf_ln_b, w_o_conf, w_out, norm_ffn_w, w_up, ffn_conv_w, ffn_conv_b, w_down, norm_final_w):
    depth = w_in.shape[0]
    nb_s = x_sample.shape[0]
    pw = _prepare_weights(norm_mix_w, w_in, conv_qkv_w, a_log, dt_bias, delta_norm_w, w_o_delta,
                          conf_conv_w, conf_conv_b, conf_ln_w, conf_ln_b, w_o_conf, w_out,
                          norm_ffn_w, w_up, ffn_conv_w, ffn_conv_b, w_down, norm_final_w)
    xp = x_prompt
    xs = x_sample.reshape(nb_s, D_MODEL)
    new_p = ([], [], [], [])
    new_s = ([], [], [])
    delta_s = None
    for l in range(depth):
        final = l == depth - 1
        q, k, v, z, sga, mb, bgc, gct, n_qkv, n_conf = _mixer_in_prompt(xp, pw, l)
        o, n_delta = _delta_prompt(q, k, v, bgc, gct)
        xp, n_ffn = _mixer_out_ffn_prompt(o, z, sga, mb, xp, pw, l, final)
        for lst, val in zip(new_p, (n_delta, n_qkv, n_conf, n_ffn)):
            lst.append(val)
        q, k, v, z, sga, mb, ba, n_qkv, n_conf = _mixer_in_sample(
            xs, state_qkv_conv[l].reshape(nb_s, -1), state_conf_conv[l].reshape(nb_s, -1), pw, l)
        heads = lambda a: a.reshape(nb_s, N_HEADS, HEAD_DIM)
        o, delta_s = _delta_sample(ba[:, :N_HEADS].reshape(-1), ba[:, N_HEADS:2 * N_HEADS].reshape(-1),
                                   heads(q).transpose(0, 2, 1), heads(k).transpose(0, 2, 1), heads(v),
                                   state_delta, l, delta_s)
        xs, n_ffn = _mixer_out_ffn_sample(o.reshape(nb_s, D_MODEL), z, sga, mb, xs,
                                          state_ffn_conv[l].reshape(nb_s, -1), pw, l, final)
        for lst, val in zip(new_s, (n_qkv.reshape(state_qkv_conv.shape[1:]),
                                    n_conf.reshape(state_conf_conv.shape[1:]),
                                    n_ffn.reshape(state_ffn_conv.shape[1:]))):
            lst.append(val)
    return (xp, xs.reshape(x_sample.shape),
            jnp.stack(new_p[0]), jnp.stack(new_p[1]), jnp.stack(new_p[2]), jnp.stack(new_p[3]),
            delta_s, jnp.stack(new_s[0]), jnp.stack(new_s[1]), jnp.stack(new_s[2]))
```
